```python
import jax, jax.numpy as jnp
from jax import lax
import numpy as np

D_MODEL = 2048
BATCH = 4
SEQ = 2048
DEPTH = 4

A_WIDTH = D_MODEL // 2
A_GROUPS = 8
A_GROUP_DIM = A_WIDTH // A_GROUPS
CHUNK = 128
B_HEAD_DIM = 64
B_Q_HEADS = (D_MODEL // 2) // B_HEAD_DIM
B_KV_HEADS = 2
B_WIDTH = B_Q_HEADS * B_HEAD_DIM
B_KV_WIDTH = B_KV_HEADS * B_HEAD_DIM
WINDOW = 128
ROT_DIM = B_HEAD_DIM // 4
ROPE_THETA = 500000.0
C_WIDTH = D_MODEL // 2
C_HEADS = 8
C_HEAD_DIM = C_WIDTH // C_HEADS
CONV_WIDTH = 4
LRU_C = 8.0
D_WIDTH = D_MODEL // 2
POOL_WINDOWS = (2, 4, 8, 16)
D_GROUPS = len(POOL_WINDOWS)
D_GROUP_DIM = D_WIDTH // D_GROUPS

EVEN_SIZES = (A_WIDTH, A_WIDTH, A_WIDTH, B_WIDTH, B_KV_WIDTH, B_KV_WIDTH, B_WIDTH)
ODD_SIZES = (C_WIDTH, C_WIDTH, D_WIDTH, D_WIDTH)
EVEN_IN = sum(EVEN_SIZES)
ODD_IN = sum(ODD_SIZES)
EVEN_MIX = A_WIDTH + B_WIDTH
ODD_MIX = C_WIDTH + D_WIDTH
N_EVEN = (DEPTH + 1) // 2
N_ODD = DEPTH // 2
DN_ALPHA = (2 * DEPTH) ** 0.25
DN_BETA = (8 * DEPTH) ** -0.25
LN_EPS = 1e-5

kernel_name = "hybrid_gmlp_swa_rglru_pool_deepnorm"


def _split(h, sizes):
    idx = [int(i) for i in np.cumsum(sizes)[:-1]]
    return jnp.split(h, idx, axis=-1)


def layer_norm(x, g, b):
    xf = x.astype(jnp.float32)
    mu = jnp.mean(xf, axis=-1, keepdims=True)
    xc = xf - mu
    var = jnp.mean(xc * xc, axis=-1, keepdims=True)
    y = xc * lax.rsqrt(var + LN_EPS) * g.astype(jnp.float32) + b.astype(jnp.float32)
    return y.astype(x.dtype)


def partial_rope(t, cos, sin):
    half = ROT_DIM // 2
    t1, t2, rest = t[..., :half], t[..., half:ROT_DIM], t[..., ROT_DIM:]
    return jnp.concatenate([t1 * cos - t2 * sin, t2 * cos + t1 * sin, rest], axis=-1)


def chunked_gmlp(u, v, ln_g, ln_b, w_s, b_s):
    bsz, s, _ = v.shape
    nc = s // CHUNK
    v = layer_norm(v, ln_g, ln_b)
    vb = v.reshape(bsz, nc, CHUNK, A_GROUPS, A_GROUP_DIM)
    causal = jnp.tril(jnp.ones((CHUNK, CHUNK), dtype=bool))
    ws = jnp.where(causal[None], w_s, jnp.zeros_like(w_s))
    mixed = jnp.einsum('gts,bcsgd->bctgd', ws, vb) + b_s.T[None, None, :, :, None]
    return u * mixed.reshape(bsz, s, A_WIDTH)


def swa_with_sinks(q, k, v, sinks):
    bsz, s, hq, dh = q.shape
    grp = hq // B_KV_HEADS
    nb = s // WINDOW
    qb = q.reshape(bsz, nb, WINDOW, B_KV_HEADS, grp, dh)
    pad = ((0, 0), (1, 0), (0, 0), (0, 0), (0, 0))
    kb = k.reshape(bsz, nb, WINDOW, B_KV_HEADS, dh)
    vb = v.reshape(bsz, nb, WINDOW, B_KV_HEADS, dh)
    kband = jnp.concatenate([jnp.pad(kb[:, :-1], pad), kb], axis=2)
    vband = jnp.concatenate([jnp.pad(vb[:, :-1], pad), vb], axis=2)
    scores = jnp.einsum('bnqhgd,bnkhd->bnhgqk', qb, kband).astype(jnp.float32) * (dh ** -0.5)
    qi = jnp.arange(WINDOW)[:, None]
    kj = jnp.arange(2 * WINDOW)[None, :]
    diff = qi + WINDOW - kj
    band = (diff >= 0) & (diff < WINDOW)
    blk = jnp.arange(nb)[:, None, None]
    mask = band[None] & ((blk > 0) | (kj[None] >= WINDOW))
    scores = jnp.where(mask[None, :, None, None], scores, -jnp.inf)
    sink = jnp.broadcast_to(
        sinks.astype(jnp.float32).reshape(B_KV_HEADS, grp)[None, None, :, :, None, None],
        scores.shape[:-1] + (1,))
    probs = jax.nn.softmax(jnp.concatenate([scores, sink], axis=-1), axis=-1)[..., :-1]
    out = jnp.einsum('bnhgqk,bnkhd->bnqhgd', probs.astype(v.dtype), vband)
    return out.reshape(bsz, s, hq * dh)


def rg_lru(xc, conv_w, conv_b, w_a, b_a, w_x, b_x, lam):
    bsz, s, _ = xc.shape
    xconv = lax.conv_general_dilated(
        xc, conv_w[:, None, :], window_strides=(1,), padding=[(CONV_WIDTH - 1, 0)],
        dimension_numbers=('NWC', 'WIO', 'NWC'), feature_group_count=C_WIDTH) + conv_b
    xh = xconv.reshape(bsz, s, C_HEADS, C_HEAD_DIM)
    r = jax.nn.sigmoid(jnp.einsum('bshi,hij->bshj', xh, w_a).reshape(bsz, s, C_WIDTH) + b_a)
    i = jax.nn.sigmoid(jnp.einsum('bshi,hij->bshj', xh, w_x).reshape(bsz, s, C_WIDTH) + b_x)
    log_a = -LRU_C * r.astype(jnp.float32) * jax.nn.softplus(-lam.astype(jnp.float32))
    a = jnp.exp(log_a)
    mult = jnp.sqrt(-jnp.expm1(2.0 * log_a))
    bterm = mult * (i * xconv).astype(jnp.float32)

    def combine(left, right):
        a1, b1 = left
        a2, b2 = right
        return a1 * a2, a2 * b1 + b2

    _, h = lax.associative_scan(combine, (a, bterm), axis=1)
    return h.astype(xc.dtype)


def multiscale_pool(xd, w_pool, d_scale):
    bsz, s, _ = xd.shape
    xf = xd.astype(jnp.float32)
    csum = jnp.cumsum(xf, axis=1)
    pos1 = jnp.arange(s, dtype=jnp.float32)[None, :, None] + 1.0
    outs = []
    for g, w in enumerate(POOL_WINDOWS):
        sl = slice(g * D_GROUP_DIM, (g + 1) * D_GROUP_DIM)
        cg = csum[..., sl]
        shifted = jnp.pad(cg[:, :-w], ((0, 0), (w, 0), (0, 0)))
        mean = (cg - shifted) / jnp.minimum(pos1, float(w))
        outs.append(mean - xf[..., sl])
    pooled = jnp.stack(outs, axis=2).astype(xd.dtype)
    mixed = jnp.einsum('bsgi,gij->bsgj', pooled, w_pool).reshape(bsz, s, D_WIDTH)
    return mixed * d_scale


def even_layer(x, cos, sin, w_in, a_ln_g, a_ln_b, a_ws, a_bs, b_sinks, w_out, ln_g, ln_b):
    bsz, s, _ = x.shape
    h = x @ w_in
    u, v, a_gate, q, k, vv, b_gate = _split(h, EVEN_SIZES)
    a_out = chunked_gmlp(u, v, a_ln_g, a_ln_b, a_ws, a_bs) * jax.nn.silu(a_gate)
    q = partial_rope(q.reshape(bsz, s, B_Q_HEADS, B_HEAD_DIM), cos, sin)
    k = partial_rope(k.reshape(bsz, s, B_KV_HEADS, B_HEAD_DIM), cos, sin)
    vv = vv.reshape(bsz, s, B_KV_HEADS, B_HEAD_DIM)
    b_out = swa_with_sinks(q, k, vv, b_sinks) * jax.nn.silu(b_gate)
    y = jnp.concatenate([a_out, b_out], axis=-1) @ w_out
    return layer_norm(DN_ALPHA * x + y, ln_g, ln_b)


def odd_layer(x, w_in, conv_w, conv_b, w_a, b_a, w_x, b_x, lam, w_pool, d_scale,
              w_out, ln_g, ln_b):
    h = x @ w_in
    xc, c_gate, xd, d_gate = _split(h, ODD_SIZES)
    c_out = rg_lru(xc, conv_w, conv_b, w_a, b_a, w_x, b_x, lam) * jax.nn.silu(c_gate)
    d_out = multiscale_pool(xd, w_pool, d_scale) * jax.nn.silu(d_gate)
    y = jnp.concatenate([c_out, d_out], axis=-1) @ w_out
    return layer_norm(DN_ALPHA * x + y, ln_g, ln_b)


def setup_inputs(seed: int = 0) -> dict:
    key = jax.random.key(seed)
    ks = jax.random.split(key, 26)
    f32 = jnp.float32
    nrm = lambda k, shp, sc: sc * jax.random.normal(k, shp, f32)
    ne, no = N_EVEN, N_ODD
    u = jax.random.uniform(ks[25], (no, C_WIDTH), f32, 0.9, 0.999)
    a_base = u ** (1.0 / LRU_C)
    lam = jnp.log(a_base) - jnp.log1p(-a_base)
    return {
        "x": nrm(ks[0], (BATCH, SEQ, D_MODEL), 1.0),
        "positions": jnp.broadcast_to(jnp.arange(SEQ, dtype=jnp.int32)[None], (BATCH, SEQ)),
        "even_w_in": nrm(ks[1], (ne, D_MODEL, EVEN_IN), D_MODEL ** -0.5),
        "even_a_ln_g": 1.0 + nrm(ks[2], (ne, A_WIDTH), 0.01),
        "even_a_ln_b": nrm(ks[3], (ne, A_WIDTH), 0.01),
        "even_a_ws": nrm(ks[4], (ne, A_GROUPS, CHUNK, CHUNK), CHUNK ** -0.5),
        "even_a_bs": 1.0 + nrm(ks[5], (ne, A_GROUPS, CHUNK), 0.1),
        "even_b_sinks": nrm(ks[6], (ne, B_Q_HEADS), 0.5),
        "even_w_out": nrm(ks[7], (ne, EVEN_MIX, D_MODEL), DN_BETA * EVEN_MIX ** -0.5),
        "even_ln_g": 1.0 + nrm(ks[8], (ne, D_MODEL), 0.01),
        "even_ln_b": nrm(ks[9], (ne, D_MODEL), 0.01),
        "odd_w_in": nrm(ks[10], (no, D_MODEL, ODD_IN), D_MODEL ** -0.5),
        "odd_conv_w": nrm(ks[11], (no, CONV_WIDTH, C_WIDTH), CONV_WIDTH ** -0.5),
        "odd_conv_b": nrm(ks[12], (no, C_WIDTH), 0.01),
        "odd_w_a": nrm(ks[13], (no, C_HEADS, C_HEAD_DIM, C_HEAD_DIM), C_HEAD_DIM ** -0.5),
        "odd_b_a": nrm(ks[14], (no, C_WIDTH), 0.01),
        "odd_w_x": nrm(ks[15], (no, C_HEADS, C_HEAD_DIM, C_HEAD_DIM), C_HEAD_DIM ** -0.5),
        "odd_b_x": nrm(ks[16], (no, C_WIDTH), 0.01),
        "odd_lam": lam,
        "odd_w_pool": nrm(ks[17], (no, D_GROUPS, D_GROUP_DIM, D_GROUP_DIM), D_GROUP_DIM ** -0.5),
        "odd_d_scale": 1.0 + nrm(ks[18], (no, D_WIDTH), 0.1),
        "odd_w_out": nrm(ks[19], (no, ODD_MIX, D_MODEL), DN_BETA * ODD_MIX ** -0.5),
        "odd_ln_g": 1.0 + nrm(ks[20], (no, D_MODEL), 0.01),
        "odd_ln_b": nrm(ks[21], (no, D_MODEL), 0.01),
    }


def reference(x, positions, even_w_in, even_a_ln_g, even_a_ln_b, even_a_ws, even_a_bs,
              even_b_sinks, even_w_out, even_ln_g, even_ln_b, odd_w_in, odd_conv_w,
              odd_conv_b, odd_w_a, odd_b_a, odd_w_x, odd_b_x, odd_lam, odd_w_pool,
              odd_d_scale, odd_w_out, odd_ln_g, odd_ln_b):
    inv_freq = ROPE_THETA ** (-jnp.arange(0, ROT_DIM, 2, dtype=jnp.float32) / ROT_DIM)
    ang = positions.astype(jnp.float32)[..., None] * inv_freq
    cos = jnp.cos(ang)[:, :, None, :].astype(x.dtype)
    sin = jnp.sin(ang)[:, :, None, :].astype(x.dtype)
    for layer in range(DEPTH):
        j = layer // 2
        if layer % 2 == 0:
            x = even_layer(x, cos, sin, even_w_in[j], even_a_ln_g[j], even_a_ln_b[j],
                           even_a_ws[j], even_a_bs[j], even_b_sinks[j], even_w_out[j],
                           even_ln_g[j], even_ln_b[j])
        else:
            x = odd_layer(x, odd_w_in[j], odd_conv_w[j], odd_conv_b[j], odd_w_a[j],
                          odd_b_a[j], odd_w_x[j], odd_b_x[j], odd_lam[j], odd_w_pool[j],
                          odd_d_scale[j], odd_w_out[j], odd_ln_g[j], odd_ln_b[j])
    return x
```

```python
from functools import partial

import jax
import jax.numpy as jnp
from jax import lax
from jax.experimental import pallas as pl
from jax.experimental.pallas import tpu as pltpu

D_MODEL = 2048
DEPTH = 4
A_WIDTH = 1024
A_GROUPS = 8
CHUNK = 128
B_HEAD_DIM = 64
B_Q_HEADS = 16
B_KV_HEADS = 2
B_WIDTH = 1024
B_KV_WIDTH = 128
WINDOW = 128
ROT_DIM = 16
ROPE_THETA = 500000.0
C_WIDTH = 1024
C_HEADS = 8
C_HEAD_DIM = 128
CONV_WIDTH = 4
LRU_C = 8.0
D_WIDTH = 1024
POOL_WINDOWS = (2, 4, 8, 16)
D_GROUP_DIM = 256
EVEN_IN = 3 * A_WIDTH + 2 * B_WIDTH + 2 * B_KV_WIDTH
ODD_IN = 2 * C_WIDTH + 2 * D_WIDTH
MIX_WIDTH = 2048
DN_ALPHA = (2 * DEPTH) ** 0.25
LN_EPS = 1e-5

LANES = 128
SUBLANES = 8
VMEM_LIMIT_BYTES = 48 * 1024 * 1024

PROJ_TM = 512
EVEN_TN = 768
ODD_TN = 1024
OUT_TM = 256
MIX_T = 256
CONV_HALO = SUBLANES
POOL_HALO = 16

F32 = jnp.float32
BF16 = jnp.bfloat16


def _silu(x):
    return x * jax.nn.sigmoid(x)


def _proj_in_kernel(x_ref, w_ref, o_ref, xb_ref):
    @pl.when(pl.program_id(1) == 0)
    def _():
        xb_ref[...] = x_ref[...].astype(BF16)

    o_ref[...] = jnp.dot(xb_ref[...], w_ref[...], preferred_element_type=F32)


def _proj_in(x, w, tn):
    m, k = x.shape
    n = w.shape[1]
    return pl.pallas_call(
        _proj_in_kernel,
        grid=(m // PROJ_TM, n // tn),
        in_specs=[
            pl.BlockSpec((PROJ_TM, k), lambda i, j: (i, 0)),
            pl.BlockSpec((k, tn), lambda i, j: (0, j)),
        ],
        out_specs=pl.BlockSpec((PROJ_TM, tn), lambda i, j: (i, j)),
        out_shape=jax.ShapeDtypeStruct((m, n), F32),
        scratch_shapes=[pltpu.VMEM((PROJ_TM, k), BF16)],
        compiler_params=pltpu.CompilerParams(
            dimension_semantics=("parallel", "arbitrary"),
            vmem_limit_bytes=VMEM_LIMIT_BYTES),
        name="proj_in",
    )(x, w)


def _proj_out_ln_kernel(mix_ref, w_ref, x_ref, g_ref, b_ref, o_ref):
    y = jnp.dot(mix_ref[...], w_ref[...], preferred_element_type=F32)
    z = DN_ALPHA * x_ref[...] + y
    mu = jnp.mean(z, axis=-1, keepdims=True)
    zc = z - mu
    var = jnp.mean(zc * zc, axis=-1, keepdims=True)
    o_ref[...] = zc * lax.rsqrt(var + LN_EPS) * g_ref[...] + b_ref[...]


def _proj_out_ln(mix, w, x, g, b):
    m, k = mix.shape
    n = w.shape[1]
    return pl.pallas_call(
        _proj_out_ln_kernel,
        grid=(m // OUT_TM,),
        in_specs=[
            pl.BlockSpec((OUT_TM, k), lambda i: (i, 0)),
            pl.BlockSpec((k, n), lambda i: (0, 0)),
            pl.BlockSpec((OUT_TM, n), lambda i: (i, 0)),
            pl.BlockSpec((1, n), lambda i: (0, 0)),
            pl.BlockSpec((1, n), lambda i: (0, 0)),
        ],
        out_specs=pl.BlockSpec((OUT_TM, n), lambda i: (i, 0)),
        out_shape=jax.ShapeDtypeStruct((m, n), F32),
        compiler_params=pltpu.CompilerParams(
            dimension_semantics=("parallel",),
            vmem_limit_bytes=VMEM_LIMIT_BYTES),
        name="proj_out_ln",
    )(mix, w, x, g, b)


def _rope_tile(t, cos, sin, take_upper):
    upper = pltpu.roll(t, LANES - ROT_DIM // 2, 1)
    lower = pltpu.roll(t, ROT_DIM // 2, 1)
    return t * cos + jnp.where(take_upper, upper, lower) * sin


def _even_mix_kernel(sinks_ref, u_ref, v_ref, ag_ref, q_ref, bg_ref, kc_ref, vc_ref, kp_ref, vp_ref,
                     cosc_ref, sinc_ref, cosp_ref, sinp_ref, lng_ref, lnb_ref, ws_ref, bst_ref,
                     o_ref):
    t_rows = u_ref.shape[0]
    n_chunks = t_rows // CHUNK

    v = v_ref[...]
    mu = jnp.mean(v, axis=-1, keepdims=True)
    vc = v - mu
    var = jnp.mean(vc * vc, axis=-1, keepdims=True)
    vln = (vc * lax.rsqrt(var + LN_EPS) * lng_ref[...] + lnb_ref[...]).astype(BF16)
    causal = (lax.broadcasted_iota(jnp.int32, (CHUNK, CHUNK), 0)
              >= lax.broadcasted_iota(jnp.int32, (CHUNK, CHUNK), 1))
    for g in range(A_GROUPS):
        cols = slice(g * CHUNK, (g + 1) * CHUNK)
        w = jnp.where(causal, ws_ref[g], 0.0).astype(BF16)
        rhs = jnp.concatenate(
            [vln[c * CHUNK:(c + 1) * CHUNK, cols] for c in range(n_chunks)], axis=1)
        mixed = jnp.dot(w, rhs, preferred_element_type=F32) + bst_ref[:, g:g + 1]
        for c in range(n_chunks):
            rows = slice(c * CHUNK, (c + 1) * CHUNK)
            gate = _silu(ag_ref[rows, cols])
            o_ref[rows, cols] = (u_ref[rows, cols] * mixed[:, c * CHUNK:(c + 1) * CHUNK]
                                 * gate).astype(o_ref.dtype)

    lane = lax.broadcasted_iota(jnp.int32, (1, LANES), 1)
    take_upper = (lane % B_HEAD_DIM) < (ROT_DIM // 2)
    low_half = lane < B_HEAD_DIM
    cosc = cosc_ref[...]
    sinc = sinc_ref[...]
    k_all = jnp.concatenate(
        [_rope_tile(kp_ref[...], cosp_ref[...], sinp_ref[...], take_upper),
         _rope_tile(kc_ref[...], cosc, sinc, take_upper)], axis=0)
    v_all = jnp.concatenate([vp_ref[...], vc_ref[...]], axis=0)
    k_swap = pltpu.roll(k_all, B_HEAD_DIM, 1)
    v_swap = pltpu.roll(v_all, B_HEAD_DIM, 1)
    zero = jnp.zeros_like(k_all)
    k_lo = (jnp.where(low_half, k_all, zero).astype(BF16), jnp.where(low_half, k_swap, zero).astype(BF16))
    k_hi = (jnp.where(low_half, zero, k_swap).astype(BF16), jnp.where(low_half, zero, k_all).astype(BF16))
    v_lo = (jnp.where(low_half, v_all, zero).astype(BF16), jnp.where(low_half, v_swap, zero).astype(BF16))
    v_hi = (jnp.where(low_half, zero, v_swap).astype(BF16), jnp.where(low_half, zero, v_all).astype(BF16))

    scale = B_HEAD_DIM ** -0.5
    q_tiles = [
        (_rope_tile(q_ref[:, j * LANES:(j + 1) * LANES], cosc, sinc, take_upper) * scale).astype(BF16)
        for j in range(B_WIDTH // LANES)]

    tiles_per_kv = B_WIDTH // LANES // B_KV_HEADS
    rows_s = tiles_per_kv * WINDOW
    qi = lax.broadcasted_iota(jnp.int32, (rows_s, 2 * WINDOW), 0) % WINDOW
    kj = lax.broadcasted_iota(jnp.int32, (rows_s, 2 * WINDOW), 1)
    diff = qi + WINDOW - kj
    band = (diff >= 0) & (diff < WINDOW)
    first_key = jnp.where(pl.program_id(1) == 0, WINDOW, 0)

    for n in range(t_rows // WINDOW):
        rows = slice(n * WINDOW, (n + 1) * WINDOW)
        band_rows = slice(n * WINDOW, (n + 2) * WINDOW)
        valid = band
        if n == 0:
            valid = band & (kj >= first_key)
        for kv in range(B_KV_HEADS):
            q4 = jnp.concatenate(
                [q_tiles[kv * tiles_per_kv + j][rows] for j in range(tiles_per_kv)], axis=0)
            k_cat = jnp.concatenate([k_lo[kv][band_rows], k_hi[kv][band_rows]], axis=0)
            s = lax.dot_general(q4, k_cat, (((1,), (1,)), ((), ())),
                                preferred_element_type=F32)
            p_halves = []
            inv_l = []
            for half in range(2):
                sh = jnp.where(valid, s[:, half * 2 * WINDOW:(half + 1) * 2 * WINDOW], -jnp.inf)
                sink = jnp.concatenate(
                    [jnp.full((WINDOW, 1), sinks_ref[2 * (kv * tiles_per_kv + j) + half], F32)
                     for j in range(tiles_per_kv)], axis=0)
                m = jnp.maximum(jnp.max(sh, axis=-1, keepdims=True), sink)
                p = jnp.exp(sh - m)
                l = jnp.sum(p, axis=-1, keepdims=True) + jnp.exp(sink - m)
                p_halves.append(p.astype(BF16))
                inv_l.append(1.0 / l)
            p_cat = jnp.concatenate(p_halves, axis=1)
            v_cat = jnp.concatenate([v_lo[kv][band_rows], v_hi[kv][band_rows]], axis=0)
            o = jnp.dot(p_cat, v_cat, preferred_element_type=F32)
            o = o * jnp.where(low_half, inv_l[0], inv_l[1])
            for j in range(tiles_per_kv):
                jj = kv * tiles_per_kv + j
                cols = slice(jj * LANES, (jj + 1) * LANES)
                out_cols = slice(A_WIDTH + jj * LANES, A_WIDTH + (jj + 1) * LANES)
                o_ref[rows, out_cols] = (o[j * WINDOW:(j + 1) * WINDOW]
                                         * _silu(bg_ref[rows, cols])).astype(o_ref.dtype)


def _even_mix(h, sinks, cos_t, sin_t, ln_g, ln_b, ws, bst, batch, seq):
    m = h.shape[0]
    t = MIX_T
    n_s = seq // t
    blocks_per_tile = t // WINDOW
    kcol = (3 * A_WIDTH + 2 * B_WIDTH) // B_KV_WIDTH
    vcol = kcol + 1

    def row(b, s):
        return b * n_s + s

    def prev(b, s):
        return jnp.maximum(row(b, s) * blocks_per_tile - 1, 0)

    wide = lambda c: pl.BlockSpec((t, A_WIDTH), lambda b, s: (row(b, s), c))
    return pl.pallas_call(
        _even_mix_kernel,
        grid=(batch, n_s),
        in_specs=[
            pl.BlockSpec(memory_space=pltpu.SMEM),
            wide(0), wide(1), wide(2), wide(3), wide(4),
            pl.BlockSpec((t, B_KV_WIDTH), lambda b, s: (row(b, s), kcol)),
            pl.BlockSpec((t, B_KV_WIDTH), lambda b, s: (row(b, s), vcol)),
            pl.BlockSpec((WINDOW, B_KV_WIDTH), lambda b, s: (prev(b, s), kcol)),
            pl.BlockSpec((WINDOW, B_KV_WIDTH), lambda b, s: (prev(b, s), vcol)),
            pl.BlockSpec((t, LANES), lambda b, s: (row(b, s), 0)),
            pl.BlockSpec((t, LANES), lambda b, s: (row(b, s), 0)),
            pl.BlockSpec((WINDOW, LANES), lambda b, s: (prev(b, s), 0)),
            pl.BlockSpec((WINDOW, LANES), lambda b, s: (prev(b, s), 0)),
            pl.BlockSpec((1, A_WIDTH), lambda b, s: (0, 0)),
            pl.BlockSpec((1, A_WIDTH), lambda b, s: (0, 0)),
            pl.BlockSpec((A_GROUPS, CHUNK, CHUNK), lambda b, s: (0, 0, 0)),
            pl.BlockSpec((CHUNK, A_GROUPS), lambda b, s: (0, 0)),
        ],
        out_specs=pl.BlockSpec((t, MIX_WIDTH), lambda b, s: (row(b, s), 0)),
        out_shape=jax.ShapeDtypeStruct((m, MIX_WIDTH), BF16),
        compiler_params=pltpu.CompilerParams(
            dimension_semantics=("parallel", "parallel"),
            vmem_limit_bytes=VMEM_LIMIT_BYTES),
        name="even_mix",
    )(sinks, h, h, h, h, h, h, h, h, h, cos_t, sin_t, cos_t, sin_t, ln_g, ln_b, ws, bst)


def _odd_mix_kernel(xc_ref, cg_ref, xd_ref, dg_ref, convw_ref, convb_ref, wax_ref, ba_ref, bx_ref,
                    lam_ref, wpool_ref, dscale_ref, o_ref,
                    halo_c, halo_d, hstate, a_s, b_s, h_s):
    t_rows = xc_ref.shape[0]
    s_idx = pl.program_id(1)

    @pl.when(s_idx == 0)
    def _():
        halo_c[...] = jnp.zeros_like(halo_c)
        halo_d[...] = jnp.zeros_like(halo_d)
        hstate[...] = jnp.zeros_like(hstate)

    xc = xc_ref[...]
    xcp = jnp.concatenate([halo_c[...], xc], axis=0)
    halo_c[...] = xc[t_rows - CONV_HALO:]
    acc = xcp * convw_ref[CONV_WIDTH - 1:CONV_WIDTH, :]
    for j in range(CONV_WIDTH - 1):
        acc = acc + pltpu.roll(xcp, CONV_WIDTH - 1 - j, 0) * convw_ref[j:j + 1, :]
    xconv = acc[CONV_HALO:] + convb_ref[...]

    z = -lam_ref[...]
    softplus = jnp.maximum(z, 0.0) + jnp.log1p(jnp.exp(-jnp.abs(z)))
    for hd in range(C_HEADS):
        cols = slice(hd * C_HEAD_DIM, (hd + 1) * C_HEAD_DIM)
        xh = xconv[:, cols]
        ri = jnp.dot(xh.astype(BF16), wax_ref[hd], preferred_element_type=F32)
        r = jax.nn.sigmoid(ri[:, :C_HEAD_DIM] + ba_ref[:, cols])
        i = jax.nn.sigmoid(ri[:, C_HEAD_DIM:] + bx_ref[:, cols])
        log_a = -LRU_C * r * softplus[:, cols]
        a = jnp.exp(log_a)
        a_s[:, cols] = a
        b_s[:, cols] = jnp.sqrt(-jnp.tanh(log_a) * (a * a + 1.0)) * (i * xh)

    def step(t, h):
        h = a_s[pl.ds(t, 1), :] * h + b_s[pl.ds(t, 1), :]
        h_s[pl.ds(t, 1), :] = h
        return h

    h_last = lax.fori_loop(0, t_rows, step, hstate[0:1, :], unroll=8)
    hstate[0:1, :] = h_last
    o_ref[:, :C_WIDTH] = (h_s[...] * _silu(cg_ref[...])).astype(o_ref.dtype)

    xd = xd_ref[...]
    xdp = jnp.concatenate([halo_d[...], xd], axis=0)
    halo_d[...] = xd[t_rows - POOL_HALO:]
    pos1 = (s_idx * t_rows + lax.broadcasted_iota(jnp.int32, (t_rows, 1), 0) + 1).astype(F32)
    for g, w in enumerate(POOL_WINDOWS):
        cols = slice(g * D_GROUP_DIM, (g + 1) * D_GROUP_DIM)
        win = xdp[:, cols]
        sh = 1
        while sh < w:
            win = win + pltpu.roll(win, sh, 0)
            sh *= 2
        mean = win[POOL_HALO:] / jnp.minimum(pos1, float(w))
        pooled = (mean - xd[:, cols]).astype(BF16)
        mixed = jnp.dot(pooled, wpool_ref[g], preferred_element_type=F32)
        out_cols = slice(C_WIDTH + g * D_GROUP_DIM, C_WIDTH + (g + 1) * D_GROUP_DIM)
        o_ref[:, out_cols] = (mixed * dscale_ref[:, cols] * _silu(dg_ref[:, cols])).astype(o_ref.dtype)


def _odd_mix(h, conv_w, conv_b, wax, b_a, b_x, lam, w_pool, d_scale, batch, seq):
    m = h.shape[0]
    t = MIX_T
    n_s = seq // t
    wide = lambda c: pl.BlockSpec((t, C_WIDTH), lambda b, s: (b * n_s + s, c))
    vec = pl.BlockSpec((1, C_WIDTH), lambda b, s: (0, 0))
    return pl.pallas_call(
        _odd_mix_kernel,
        grid=(batch, n_s),
        in_specs=[
            wide(0), wide(1), wide(2), wide(3),
            pl.BlockSpec((CONV_WIDTH, C_WIDTH), lambda b, s: (0, 0)),
            vec,
            pl.BlockSpec((C_HEADS, C_HEAD_DIM, 2 * C_HEAD_DIM), lambda b, s: (0, 0, 0)),
            vec, vec, vec,
            pl.BlockSpec((len(POOL_WINDOWS), D_GROUP_DIM, D_GROUP_DIM), lambda b, s: (0, 0, 0)),
            vec,
        ],
        out_specs=pl.BlockSpec((t, MIX_WIDTH), lambda b, s: (b * n_s + s, 0)),
        out_shape=jax.ShapeDtypeStruct((m, MIX_WIDTH), BF16),
        scratch_shapes=[
            pltpu.VMEM((CONV_HALO, C_WIDTH), F32),
            pltpu.VMEM((POOL_HALO, D_WIDTH), F32),
            pltpu.VMEM((SUBLANES, C_WIDTH), F32),
            pltpu.VMEM((t, C_WIDTH), F32),
            pltpu.VMEM((t, C_WIDTH), F32),
            pltpu.VMEM((t, C_WIDTH), F32),
        ],
        compiler_params=pltpu.CompilerParams(
            dimension_semantics=("parallel", "arbitrary"),
            vmem_limit_bytes=VMEM_LIMIT_BYTES),
        name="odd_mix",
    )(h, h, h, h, conv_w, conv_b, wax, b_a, b_x, lam, w_pool, d_scale)


def _rope_tables(positions):
    half = ROT_DIM // 2
    inv_freq = ROPE_THETA ** (-jnp.arange(0, ROT_DIM, 2, dtype=F32) / ROT_DIM)
    ang = positions.astype(F32).reshape(-1, 1) * inv_freq
    cos = jnp.cos(ang)
    sin = jnp.sin(ang)
    ones = jnp.ones((ang.shape[0], B_HEAD_DIM - ROT_DIM), F32)
    cos_head = jnp.concatenate([cos, cos, ones], axis=1)
    sin_head = jnp.concatenate([-sin, sin, 0.0 * ones], axis=1)
    return jnp.tile(cos_head, (1, LANES // B_HEAD_DIM)), jnp.tile(sin_head, (1, LANES // B_HEAD_DIM))


def kernel(x, positions, even_w_in, even_a_ln_g, even_a_ln_b, even_a_ws, even_a_bs, even_b_sinks, even_w_out, even_ln_g, even_ln_b, odd_w_in, odd_conv_w, odd_conv_b, odd_w_a, odd_b_a, odd_w_x, odd_b_x, odd_lam, odd_w_pool, odd_d_scale, odd_w_out, odd_ln_g, odd_ln_b):
    batch, seq, d = x.shape
    m = batch * seq
    xf = x.reshape(m, d)
    cos_t, sin_t = _rope_tables(positions)
    gate_b = 3 * A_WIDTH + B_WIDTH + 2 * B_KV_WIDTH
    for layer in range(DEPTH):
        j = layer // 2
        if layer % 2 == 0:
            w = even_w_in[j]
            w_in = jnp.concatenate(
                [w[:, :3 * A_WIDTH + B_WIDTH], w[:, gate_b:], w[:, 3 * A_WIDTH + B_WIDTH:gate_b]],
                axis=1).astype(BF16)
            h = _proj_in(xf, w_in, EVEN_TN)
            mix = _even_mix(
                h, even_b_sinks[j], cos_t, sin_t,
                even_a_ln_g[j].reshape(1, -1), even_a_ln_b[j].reshape(1, -1),
                even_a_ws[j], even_a_bs[j].T, batch, seq)
            xf = _proj_out_ln(mix, even_w_out[j].astype(BF16), xf,
                              even_ln_g[j].reshape(1, -1), even_ln_b[j].reshape(1, -1))
        else:
            h = _proj_in(xf, odd_w_in[j].astype(BF16), ODD_TN)
            wax = jnp.concatenate([odd_w_a[j], odd_w_x[j]], axis=-1).astype(BF16)
            mix = _odd_mix(
                h, odd_conv_w[j], odd_conv_b[j].reshape(1, -1), wax,
                odd_b_a[j].reshape(1, -1), odd_b_x[j].reshape(1, -1), odd_lam[j].reshape(1, -1),
                odd_w_pool[j].astype(BF16), odd_d_scale[j].reshape(1, -1), batch, seq)
            xf = _proj_out_ln(mix, odd_w_out[j].astype(BF16), xf,
                              odd_ln_g[j].reshape(1, -1), odd_ln_b[j].reshape(1, -1))
    return xf.reshape(batch, seq, d)
```

```python
import jax
import jax.numpy as jnp
from jax import lax
from jax.experimental import pallas as pl
from jax.experimental.pallas import tpu as pltpu

D_MODEL = 2048
DEPTH = 4
A_WIDTH = 1024
A_GROUPS = 8
CHUNK = 128
B_HEAD_DIM = 64
B_Q_HEADS = 16
B_KV_HEADS = 2
B_WIDTH = 1024
B_KV_WIDTH = 128
WINDOW = 128
ROT_DIM = 16
ROPE_THETA = 500000.0
C_WIDTH = 1024
C_HEADS = 8
C_HEAD_DIM = 128
CONV_WIDTH = 4
LRU_C = 8.0
D_WIDTH = 1024
POOL_WINDOWS = (2, 4, 8, 16)
D_GROUP_DIM = 256
MIX_WIDTH = 2048
DN_ALPHA = (2 * DEPTH) ** 0.25
LN_EPS = 1e-5

LANES = 128
SUBLANES = 8
VMEM_LIMIT_BYTES = 56 * 1024 * 1024

PROJ_TM = 256
OUT_TM = 512
MIX_T = 256
CONV_HALO = SUBLANES
POOL_HALO = 32

F32 = jnp.float32
BF16 = jnp.bfloat16


def _sigmoid(x):
    return 0.5 * jnp.tanh(0.5 * x) + 0.5


def _silu(x):
    hx = 0.5 * x
    return hx * jnp.tanh(hx) + hx


def _resident(shape):
    return pl.BlockSpec(shape, lambda *_: (0,) * len(shape), pipeline_mode=pl.Buffered(1))


def _proj_in_kernel(x_ref, w_ref, o_ref):
    o_ref[...] = jnp.dot(x_ref[...].astype(BF16), w_ref[...], preferred_element_type=F32)


def _proj_in(x, w):
    m, k = x.shape
    n = w.shape[1]
    return pl.pallas_call(
        _proj_in_kernel,
        grid=(m // PROJ_TM,),
        in_specs=[pl.BlockSpec((PROJ_TM, k), lambda i: (i, 0)), _resident((k, n))],
        out_specs=pl.BlockSpec((PROJ_TM, n), lambda i: (i, 0)),
        out_shape=jax.ShapeDtypeStruct((m, n), F32),
        compiler_params=pltpu.CompilerParams(
            dimension_semantics=("parallel",), vmem_limit_bytes=VMEM_LIMIT_BYTES),
        name="proj_in",
    )(x, w)


def _proj_out_ln_kernel(mix_ref, w_ref, x_ref, g_ref, b_ref, o_ref):
    y = jnp.dot(mix_ref[...], w_ref[...], preferred_element_type=F32)
    z = DN_ALPHA * x_ref[...] + y
    mu = jnp.mean(z, axis=-1, keepdims=True)
    zc = z - mu
    var = jnp.mean(zc * zc, axis=-1, keepdims=True)
    o_ref[...] = zc * lax.rsqrt(var + LN_EPS) * g_ref[...] + b_ref[...]


def _proj_out_ln(mix, w, x, g, b):
    m, k = mix.shape
    n = w.shape[1]
    return pl.pallas_call(
        _proj_out_ln_kernel,
        grid=(m // OUT_TM,),
        in_specs=[
            pl.BlockSpec((OUT_TM, k), lambda i: (i, 0)),
            _resident((k, n)),
            pl.BlockSpec((OUT_TM, n), lambda i: (i, 0)),
            _resident((1, n)),
            _resident((1, n)),
        ],
        out_specs=pl.BlockSpec((OUT_TM, n), lambda i: (i, 0)),
        out_shape=jax.ShapeDtypeStruct((m, n), F32),
        compiler_params=pltpu.CompilerParams(
            dimension_semantics=("parallel",), vmem_limit_bytes=VMEM_LIMIT_BYTES),
        name="proj_out_ln",
    )(mix, w, x, g, b)


def _rope_tile(t, cos, sin, take_upper):
    upper = pltpu.roll(t, LANES - ROT_DIM // 2, 1)
    lower = pltpu.roll(t, ROT_DIM // 2, 1)
    return t * cos + jnp.where(take_upper, upper, lower) * sin


def _even_mix_kernel(sinks_ref, u_ref, v_ref, ag_ref, q_ref, bg_ref, kc_ref, vc_ref, kp_ref, vp_ref,
                     cosc_ref, sinc_ref, cosp_ref, sinp_ref, lng_ref, lnb_ref, ws_ref, bst_ref,
                     o_ref):
    t_rows = u_ref.shape[0]
    n_chunks = t_rows // CHUNK

    v = v_ref[...]
    mu = jnp.mean(v, axis=-1, keepdims=True)
    vc = v - mu
    var = jnp.mean(vc * vc, axis=-1, keepdims=True)
    vln = (vc * lax.rsqrt(var + LN_EPS) * lng_ref[...] + lnb_ref[...]).astype(BF16)
    causal = (lax.broadcasted_iota(jnp.int32, (CHUNK, CHUNK), 0)
              >= lax.broadcasted_iota(jnp.int32, (CHUNK, CHUNK), 1))
    for g in range(A_GROUPS):
        cols = slice(g * CHUNK, (g + 1) * CHUNK)
        w = jnp.where(causal, ws_ref[g], 0.0).astype(BF16)
        rhs = jnp.concatenate(
            [vln[c * CHUNK:(c + 1) * CHUNK, cols] for c in range(n_chunks)], axis=1)
        mixed = jnp.dot(w, rhs, preferred_element_type=F32) + bst_ref[:, g:g + 1]
        for c in range(n_chunks):
            rows = slice(c * CHUNK, (c + 1) * CHUNK)
            gate = _silu(ag_ref[rows, cols])
            o_ref[rows, cols] = (u_ref[rows, cols] * mixed[:, c * CHUNK:(c + 1) * CHUNK]
                                 * gate).astype(o_ref.dtype)

    lane = lax.broadcasted_iota(jnp.int32, (1, LANES), 1)
    take_upper = (lane % B_HEAD_DIM) < (ROT_DIM // 2)
    low_half = lane < B_HEAD_DIM
    cosc = cosc_ref[...]
    sinc = sinc_ref[...]
    k_all = jnp.concatenate(
        [_rope_tile(kp_ref[...], cosp_ref[...], sinp_ref[...], take_upper),
         _rope_tile(kc_ref[...], cosc, sinc, take_upper)], axis=0)
    v_all = jnp.concatenate([vp_ref[...], vc_ref[...]], axis=0)
    k_swap = pltpu.roll(k_all, B_HEAD_DIM, 1)
    v_swap = pltpu.roll(v_all, B_HEAD_DIM, 1)
    zero = jnp.zeros_like(k_all)
    k_lo = (jnp.where(low_half, k_all, zero).astype(BF16), jnp.where(low_half, k_swap, zero).astype(BF16))
    k_hi = (jnp.where(low_half, zero, k_swap).astype(BF16), jnp.where(low_half, zero, k_all).astype(BF16))
    v_lo = (jnp.where(low_half, v_all, zero).astype(BF16), jnp.where(low_half, v_swap, zero).astype(BF16))
    v_hi = (jnp.where(low_half, zero, v_swap).astype(BF16), jnp.where(low_half, zero, v_all).astype(BF16))

    scale = B_HEAD_DIM ** -0.5
    q_tiles = [
        (_rope_tile(q_ref[:, j * LANES:(j + 1) * LANES], cosc, sinc, take_upper) * scale).astype(BF16)
        for j in range(B_WIDTH // LANES)]

    tiles_per_kv = B_WIDTH // LANES // B_KV_HEADS
    rows_s = tiles_per_kv * WINDOW
    qi = lax.broadcasted_iota(jnp.int32, (rows_s, 2 * WINDOW), 0) % WINDOW
    kj = lax.broadcasted_iota(jnp.int32, (rows_s, 2 * WINDOW), 1)
    diff = qi + WINDOW - kj
    band = (diff >= 0) & (diff < WINDOW)
    first_key = jnp.where(pl.program_id(1) == 0, WINDOW, 0)

    for n in range(t_rows // WINDOW):
        rows = slice(n * WINDOW, (n + 1) * WINDOW)
        band_rows = slice(n * WINDOW, (n + 2) * WINDOW)
        valid = band
        if n == 0:
            valid = band & (kj >= first_key)
        for kv in range(B_KV_HEADS):
            q4 = jnp.concatenate(
                [q_tiles[kv * tiles_per_kv + j][rows] for j in range(tiles_per_kv)], axis=0)
            k_cat = jnp.concatenate([k_lo[kv][band_rows], k_hi[kv][band_rows]], axis=0)
            s = lax.dot_general(q4, k_cat, (((1,), (1,)), ((), ())),
                                preferred_element_type=F32)
            p_halves = []
            inv_l = []
            for half in range(2):
                sh = jnp.where(valid, s[:, half * 2 * WINDOW:(half + 1) * 2 * WINDOW], -jnp.inf)
                sink = jnp.concatenate(
                    [jnp.full((WINDOW, 1), sinks_ref[2 * (kv * tiles_per_kv + j) + half], F32)
                     for j in range(tiles_per_kv)], axis=0)
                m = jnp.maximum(jnp.max(sh, axis=-1, keepdims=True), sink)
                p = jnp.exp(sh - m)
                l = jnp.sum(p, axis=-1, keepdims=True) + jnp.exp(sink - m)
                p_halves.append(p.astype(BF16))
                inv_l.append(1.0 / l)
            p_cat = jnp.concatenate(p_halves, axis=1)
            v_cat = jnp.concatenate([v_lo[kv][band_rows], v_hi[kv][band_rows]], axis=0)
            o = jnp.dot(p_cat, v_cat, preferred_element_type=F32)
            o = o * jnp.where(low_half, inv_l[0], inv_l[1])
            for j in range(tiles_per_kv):
                jj = kv * tiles_per_kv + j
                cols = slice(jj * LANES, (jj + 1) * LANES)
                out_cols = slice(A_WIDTH + jj * LANES, A_WIDTH + (jj + 1) * LANES)
                o_ref[rows, out_cols] = (o[j * WINDOW:(j + 1) * WINDOW]
                                         * _silu(bg_ref[rows, cols])).astype(o_ref.dtype)


def _even_mix(h, sinks, cos_t, sin_t, ln_g, ln_b, ws, bst, batch, seq):
    m = h.shape[0]
    t = MIX_T
    n_s = seq // t
    blocks_per_tile = t // WINDOW
    kcol = (3 * A_WIDTH + 2 * B_WIDTH) // B_KV_WIDTH
    vcol = kcol + 1

    def row(b, s):
        return b * n_s + s

    def prev(b, s):
        return jnp.maximum(row(b, s) * blocks_per_tile - 1, 0)

    wide = lambda c: pl.BlockSpec((t, A_WIDTH), lambda b, s: (row(b, s), c))
    return pl.pallas_call(
        _even_mix_kernel,
        grid=(batch, n_s),
        in_specs=[
            pl.BlockSpec(memory_space=pltpu.SMEM),
            wide(0), wide(1), wide(2), wide(3), wide(4),
            pl.BlockSpec((t, B_KV_WIDTH), lambda b, s: (row(b, s), kcol)),
            pl.BlockSpec((t, B_KV_WIDTH), lambda b, s: (row(b, s), vcol)),
            pl.BlockSpec((WINDOW, B_KV_WIDTH), lambda b, s: (prev(b, s), kcol)),
            pl.BlockSpec((WINDOW, B_KV_WIDTH), lambda b, s: (prev(b, s), vcol)),
            pl.BlockSpec((t, LANES), lambda b, s: (row(b, s), 0)),
            pl.BlockSpec((t, LANES), lambda b, s: (row(b, s), 0)),
            pl.BlockSpec((WINDOW, LANES), lambda b, s: (prev(b, s), 0)),
            pl.BlockSpec((WINDOW, LANES), lambda b, s: (prev(b, s), 0)),
            pl.BlockSpec((1, A_WIDTH), lambda b, s: (0, 0)),
            pl.BlockSpec((1, A_WIDTH), lambda b, s: (0, 0)),
            pl.BlockSpec((A_GROUPS, CHUNK, CHUNK), lambda b, s: (0, 0, 0)),
            pl.BlockSpec((CHUNK, A_GROUPS), lambda b, s: (0, 0)),
        ],
        out_specs=pl.BlockSpec((t, MIX_WIDTH), lambda b, s: (row(b, s), 0)),
        out_shape=jax.ShapeDtypeStruct((m, MIX_WIDTH), BF16),
        compiler_params=pltpu.CompilerParams(
            dimension_semantics=("parallel", "parallel"),
            vmem_limit_bytes=VMEM_LIMIT_BYTES),
        name="even_mix",
    )(sinks, h, h, h, h, h, h, h, h, h, cos_t, sin_t, cos_t, sin_t, ln_g, ln_b, ws, bst)


def _odd_mix_kernel(xc_ref, cg_ref, xd_ref, dg_ref, convw_ref, convb_ref, wax_ref, ba_ref, bx_ref,
                    lam_ref, wpool_ref, dscale_ref, o_ref,
                    ext_c, ext_d, lvl2, lvl4, hstate, sa, sb, sh):
    t_rows = xc_ref.shape[0]
    n_groups = t_rows // SUBLANES
    s_idx = pl.program_id(1)

    @pl.when(s_idx == 0)
    def _():
        ext_c[:, 0:CONV_HALO, :] = jnp.zeros((C_HEADS, CONV_HALO, LANES), F32)
        ext_d[:, 0:POOL_HALO, :] = jnp.zeros((D_WIDTH // LANES, POOL_HALO, LANES), F32)
        hstate[...] = jnp.zeros_like(hstate)

    @pl.when(s_idx > 0)
    def _():
        ext_c[:, 0:CONV_HALO, :] = ext_c[:, t_rows:t_rows + CONV_HALO, :]
        ext_d[:, 0:POOL_HALO, :] = ext_d[:, t_rows:t_rows + POOL_HALO, :]

    z = -lam_ref[...]
    softplus = jnp.maximum(z, 0.0) + jnp.log1p(jnp.exp(-jnp.abs(z)))
    for hd in range(C_HEADS):
        cols = slice(hd * C_HEAD_DIM, (hd + 1) * C_HEAD_DIM)
        ext_c[hd, CONV_HALO:, :] = xc_ref[:, cols]
        xconv = convb_ref[:, cols]
        for j in range(CONV_WIDTH):
            lo = CONV_HALO - (CONV_WIDTH - 1) + j
            xconv = xconv + ext_c[hd, lo:lo + t_rows, :] * convw_ref[j:j + 1, cols]
        ri = jnp.dot(xconv.astype(BF16), wax_ref[hd], preferred_element_type=F32)
        r = _sigmoid(ri[:, :C_HEAD_DIM] + ba_ref[:, cols])
        i = _sigmoid(ri[:, C_HEAD_DIM:] + bx_ref[:, cols])
        log_a = -LRU_C * r * softplus[:, cols]
        a = jnp.exp(log_a)
        m2 = -jnp.tanh(log_a) * (a * a + 1.0)
        mult = jnp.where(m2 == 0.0, 0.0, m2 * lax.rsqrt(m2))
        b = mult * (i * xconv)
        for g in range(n_groups):
            base = (g * C_HEADS + hd) * SUBLANES
            sa[base:base + SUBLANES, :] = a[g * SUBLANES:(g + 1) * SUBLANES]
            sb[base:base + SUBLANES, :] = b[g * SUBLANES:(g + 1) * SUBLANES]

    h = hstate[...]
    for t in range(t_rows):
        g, s = divmod(t, SUBLANES)
        step = pl.ds(g * C_HEADS * SUBLANES + s, C_HEADS, stride=SUBLANES)
        h = sa[step, :] * h + sb[step, :]
        sh[step, :] = h
    hstate[...] = h
    for hd in range(C_HEADS):
        cols = slice(hd * C_HEAD_DIM, (hd + 1) * C_HEAD_DIM)
        hcol = jnp.concatenate(
            [sh[(g * C_HEADS + hd) * SUBLANES:(g * C_HEADS + hd + 1) * SUBLANES, :]
             for g in range(n_groups)], axis=0)
        o_ref[:, cols] = (hcol * _silu(cg_ref[:, cols])).astype(o_ref.dtype)

    n_ext = POOL_HALO + t_rows
    tiles_per_group = D_GROUP_DIM // LANES
    pos1 = s_idx * t_rows + lax.broadcasted_iota(jnp.int32, (t_rows, LANES), 0) + 1
    for g, w in enumerate(POOL_WINDOWS):
        inv = 1.0 / jnp.minimum(pos1, w).astype(F32)
        pooled = []
        for k in range(tiles_per_group):
            tile = g * tiles_per_group + k
            cols = slice(tile * LANES, (tile + 1) * LANES)
            ext_d[tile, POOL_HALO:, :] = xd_ref[:, cols]
            lo = 8
            win = ext_d[tile, lo:n_ext, :] + ext_d[tile, lo - 1:n_ext - 1, :]
            if w >= 4:
                lvl2[k, lo:n_ext, :] = win
                lo = 16
                win = lvl2[k, lo:n_ext, :] + lvl2[k, lo - 2:n_ext - 2, :]
            if w >= 8:
                lvl4[k, lo:n_ext, :] = win
                lo = 24
                win = lvl4[k, lo:n_ext, :] + lvl4[k, lo - 4:n_ext - 4, :]
            if w >= 16:
                lo = 32
                win = win[SUBLANES:] + win[:-SUBLANES]
            pooled.append((win[POOL_HALO - lo:] * inv - xd_ref[:, cols]).astype(BF16))
        cols = slice(g * D_GROUP_DIM, (g + 1) * D_GROUP_DIM)
        mixed = jnp.dot(jnp.concatenate(pooled, axis=1), wpool_ref[g], preferred_element_type=F32)
        out_cols = slice(C_WIDTH + g * D_GROUP_DIM, C_WIDTH + (g + 1) * D_GROUP_DIM)
        o_ref[:, out_cols] = (mixed * dscale_ref[:, cols] * _silu(dg_ref[:, cols])).astype(o_ref.dtype)


def _odd_mix(h, conv_w, conv_b, wax, b_a, b_x, lam, w_pool, d_scale, batch, seq):
    m = h.shape[0]
    t = MIX_T
    n_s = seq // t
    wide = lambda c: pl.BlockSpec((t, C_WIDTH), lambda b, s: (b * n_s + s, c))
    vec = pl.BlockSpec((1, C_WIDTH), lambda b, s: (0, 0))
    return pl.pallas_call(
        _odd_mix_kernel,
        grid=(batch, n_s),
        in_specs=[
            wide(0), wide(1), wide(2), wide(3),
            pl.BlockSpec((CONV_WIDTH, C_WIDTH), lambda b, s: (0, 0)),
            vec,
            pl.BlockSpec((C_HEADS, C_HEAD_DIM, 2 * C_HEAD_DIM), lambda b, s: (0, 0, 0)),
            vec, vec, vec,
            pl.BlockSpec((len(POOL_WINDOWS), D_GROUP_DIM, D_GROUP_DIM), lambda b, s: (0, 0, 0)),
            vec,
        ],
        out_specs=pl.BlockSpec((t, MIX_WIDTH), lambda b, s: (b * n_s + s, 0)),
        out_shape=jax.ShapeDtypeStruct((m, MIX_WIDTH), BF16),
        scratch_shapes=[
            pltpu.VMEM((C_WIDTH // LANES, CONV_HALO + t, LANES), F32),
            pltpu.VMEM((D_WIDTH // LANES, POOL_HALO + t, LANES), F32),
            pltpu.VMEM((D_GROUP_DIM // LANES, POOL_HALO + t, LANES), F32),
            pltpu.VMEM((D_GROUP_DIM // LANES, POOL_HALO + t, LANES), F32),
            pltpu.VMEM((SUBLANES, LANES), F32),
            pltpu.VMEM((t * C_HEADS, LANES), F32),
            pltpu.VMEM((t * C_HEADS, LANES), F32),
            pltpu.VMEM((t * C_HEADS, LANES), F32),
        ],
        compiler_params=pltpu.CompilerParams(
            dimension_semantics=("parallel", "arbitrary"),
            vmem_limit_bytes=VMEM_LIMIT_BYTES),
        name="odd_mix",
    )(h, h, h, h, conv_w, conv_b, wax, b_a, b_x, lam, w_pool, d_scale)


def _rope_tables(positions):
    inv_freq = ROPE_THETA ** (-jnp.arange(0, ROT_DIM, 2, dtype=F32) / ROT_DIM)
    ang = positions.astype(F32).reshape(-1, 1) * inv_freq
    cos = jnp.cos(ang)
    sin = jnp.sin(ang)
    ones = jnp.ones((ang.shape[0], B_HEAD_DIM - ROT_DIM), F32)
    cos_head = jnp.concatenate([cos, cos, ones], axis=1)
    sin_head = jnp.concatenate([-sin, sin, 0.0 * ones], axis=1)
    return jnp.tile(cos_head, (1, LANES // B_HEAD_DIM)), jnp.tile(sin_head, (1, LANES // B_HEAD_DIM))


def kernel(x, positions, even_w_in, even_a_ln_g, even_a_ln_b, even_a_ws, even_a_bs, even_b_sinks, even_w_out, even_ln_g, even_ln_b, odd_w_in, odd_conv_w, odd_conv_b, odd_w_a, odd_b_a, odd_w_x, odd_b_x, odd_lam, odd_w_pool, odd_d_scale, odd_w_out, odd_ln_g, odd_ln_b):
    batch, seq, d = x.shape
    m = batch * seq
    xf = x.reshape(m, d)
    cos_t, sin_t = _rope_tables(positions)
    gate_b = 3 * A_WIDTH + B_WIDTH + 2 * B_KV_WIDTH
    for layer in range(DEPTH):
        j = layer // 2
        if layer % 2 == 0:
            w = even_w_in[j]
            w_in = jnp.concatenate(
                [w[:, :3 * A_WIDTH + B_WIDTH], w[:, gate_b:], w[:, 3 * A_WIDTH + B_WIDTH:gate_b]],
                axis=1).astype(BF16)
            h = _proj_in(xf, w_in)
            mix = _even_mix(
                h, even_b_sinks[j], cos_t, sin_t,
                even_a_ln_g[j].reshape(1, -1), even_a_ln_b[j].reshape(1, -1),
                even_a_ws[j], even_a_bs[j].T, batch, seq)
            xf = _proj_out_ln(mix, even_w_out[j].astype(BF16), xf,
                              even_ln_g[j].reshape(1, -1), even_ln_b[j].reshape(1, -1))
        else:
            h = _proj_in(xf, odd_w_in[j].astype(BF16))
            wax = jnp.concatenate([odd_w_a[j], odd_w_x[j]], axis=-1).astype(BF16)
            mix = _odd_mix(
                h, odd_conv_w[j], odd_conv_b[j].reshape(1, -1), wax,
                odd_b_a[j].reshape(1, -1), odd_b_x[j].reshape(1, -1), odd_lam[j].reshape(1, -1),
                odd_w_pool[j].astype(BF16), odd_d_scale[j].reshape(1, -1), batch, seq)
            xf = _proj_out_ln(mix, odd_w_out[j].astype(BF16), xf,
                              odd_ln_g[j].reshape(1, -1), odd_ln_b[j].reshape(1, -1))
    return xf.reshape(batch, seq, d)
```

```python
from functools import partial

import jax
import jax.numpy as jnp
from jax import lax
from jax.experimental import pallas as pl
from jax.experimental.pallas import tpu as pltpu

D_MODEL = 2048
DEPTH = 4
A_WIDTH = 1024
A_GROUPS = 8
CHUNK = 128
B_HEAD_DIM = 64
B_Q_HEADS = 16
B_KV_HEADS = 2
B_WIDTH = 1024
B_KV_WIDTH = 128
WINDOW = 128
ROT_DIM = 16
ROPE_THETA = 500000.0
C_WIDTH = 1024
C_HEADS = 8
C_HEAD_DIM = 128
CONV_WIDTH = 4
LRU_C = 8.0
D_WIDTH = 1024
POOL_WINDOWS = (2, 4, 8, 16)
D_GROUP_DIM = 256
MIX_WIDTH = 2048
DN_ALPHA = (2 * DEPTH) ** 0.25
LN_EPS = 1e-5

LANES = 128
SUBLANES = 8
VMEM_LIMIT_BYTES = 56 * 1024 * 1024

PROJ_TM = 256
OUT_TM = 512
MIX_T = 256
CONV_HALO = SUBLANES
POOL_HALO = 4 * SUBLANES

KV_BLOCK = 2 * B_KV_WIDTH
KV_COL = (3 * A_WIDTH + B_WIDTH) // KV_BLOCK
GATE_B_COL = KV_COL + 1
GATE_B_BLOCKS = B_WIDTH // KV_BLOCK

F32 = jnp.float32
BF16 = jnp.bfloat16


def _sigmoid(x):
    return 0.5 * jnp.tanh(0.5 * x) + 0.5


def _silu(x):
    hx = 0.5 * x
    return hx * jnp.tanh(hx) + hx


def _resident(shape):
    return pl.BlockSpec(shape, lambda *_: (0,) * len(shape), pipeline_mode=pl.Buffered(1))


def _layer_block(shape, layer):
    return pl.BlockSpec((None,) + shape, lambda *_: (layer,) + (0,) * len(shape))


def _compiler_params(*semantics):
    return pltpu.CompilerParams(dimension_semantics=semantics, vmem_limit_bytes=VMEM_LIMIT_BYTES)


def _proj_in_kernel(x_ref, w_ref, o_ref):
    o_ref[...] = jnp.dot(x_ref[...].astype(BF16), w_ref[...], preferred_element_type=F32)


def _proj_in(x, w):
    m, k = x.shape
    n = w.shape[1]
    return pl.pallas_call(
        _proj_in_kernel,
        grid=(m // PROJ_TM,),
        in_specs=[pl.BlockSpec((PROJ_TM, k), lambda i: (i, 0)), _resident((k, n))],
        out_specs=pl.BlockSpec((PROJ_TM, n), lambda i: (i, 0)),
        out_shape=jax.ShapeDtypeStruct((m, n), F32),
        compiler_params=_compiler_params("parallel"),
        name="proj_in",
    )(x, w)


def _proj_out_ln_kernel(*refs, cast_next):
    if cast_next:
        mix_ref, w_ref, x_ref, g_ref, b_ref, wn_ref, o_ref, wn_out_ref = refs
        wn_out_ref[...] = wn_ref[...].astype(BF16)
    else:
        mix_ref, w_ref, x_ref, g_ref, b_ref, o_ref = refs
    y = jnp.dot(mix_ref[...], w_ref[...], preferred_element_type=F32)
    z = DN_ALPHA * x_ref[...] + y
    mu = jnp.mean(z, axis=-1, keepdims=True)
    zc = z - mu
    var = jnp.mean(zc * zc, axis=-1, keepdims=True)
    o_ref[...] = zc * lax.rsqrt(var + LN_EPS) * g_ref[...] + b_ref[...]


def _proj_out_ln(mix, w, x, ln_g, ln_b, layer, w_next=None, layer_next=0):
    m, k = mix.shape
    n = w.shape[1]
    steps = m // OUT_TM
    in_specs = [
        pl.BlockSpec((OUT_TM, k), lambda i: (i, 0)),
        _resident((k, n)),
        pl.BlockSpec((OUT_TM, n), lambda i: (i, 0)),
        _layer_block((1, n), layer),
        _layer_block((1, n), layer),
    ]
    out_specs = [pl.BlockSpec((OUT_TM, n), lambda i: (i, 0))]
    out_shape = [jax.ShapeDtypeStruct((m, n), F32)]
    args = [mix, w, x, ln_g, ln_b]
    if w_next is not None:
        kn, nn = w_next.shape[1:]
        slab = kn // steps
        in_specs.append(pl.BlockSpec((None, slab, nn), lambda i: (layer_next, i, 0)))
        out_specs.append(pl.BlockSpec((slab, nn), lambda i: (i, 0)))
        out_shape.append(jax.ShapeDtypeStruct((kn, nn), BF16))
        args.append(w_next)
    outs = pl.pallas_call(
        partial(_proj_out_ln_kernel, cast_next=w_next is not None),
        grid=(steps,),
        in_specs=in_specs,
        out_specs=out_specs,
        out_shape=out_shape,
        compiler_params=_compiler_params("parallel"),
        name="proj_out_ln",
    )(*args)
    return outs if w_next is not None else (outs[0], None)


def _rope_tile(t, cos, sin, take_upper):
    upper = pltpu.roll(t, LANES - ROT_DIM // 2, 1)
    lower = pltpu.roll(t, ROT_DIM // 2, 1)
    return t * cos + jnp.where(take_upper, upper, lower) * sin


def _even_mix_kernel(sinks_ref, u_ref, v_ref, ag_ref, q_ref, kvc_ref, kvp_ref, bg0_ref, bg1_ref,
                     bg2_ref, bg3_ref, cosc_ref, sinc_ref, cosp_ref, sinp_ref, lng_ref, lnb_ref,
                     ws_ref, bst_ref, wout_ref, o_ref, wout_bf_ref, *, layer):
    t_rows = u_ref.shape[0]
    n_chunks = t_rows // CHUNK
    bg_refs = (bg0_ref, bg1_ref, bg2_ref, bg3_ref)

    wout_bf_ref[...] = wout_ref[...].astype(BF16)

    v = v_ref[...]
    mu = jnp.mean(v, axis=-1, keepdims=True)
    vc = v - mu
    var = jnp.mean(vc * vc, axis=-1, keepdims=True)
    vln = (vc * lax.rsqrt(var + LN_EPS) * lng_ref[...] + lnb_ref[...]).astype(BF16)
    causal = (lax.broadcasted_iota(jnp.int32, (CHUNK, CHUNK), 0)
              >= lax.broadcasted_iota(jnp.int32, (CHUNK, CHUNK), 1))
    for g in range(A_GROUPS):
        cols = slice(g * CHUNK, (g + 1) * CHUNK)
        w = jnp.where(causal, ws_ref[g], 0.0).astype(BF16)
        rhs = jnp.concatenate(
            [vln[c * CHUNK:(c + 1) * CHUNK, cols] for c in range(n_chunks)], axis=1)
        mixed = jnp.dot(w, rhs, preferred_element_type=F32) + bst_ref[:, g:g + 1]
        for c in range(n_chunks):
            rows = slice(c * CHUNK, (c + 1) * CHUNK)
            gate = _silu(ag_ref[rows, cols])
            o_ref[rows, cols] = (u_ref[rows, cols] * mixed[:, c * CHUNK:(c + 1) * CHUNK]
                                 * gate).astype(o_ref.dtype)

    lane = lax.broadcasted_iota(jnp.int32, (1, LANES), 1)
    take_upper = (lane % B_HEAD_DIM) < (ROT_DIM // 2)
    low_half = lane < B_HEAD_DIM
    cosc = cosc_ref[...]
    sinc = sinc_ref[...]
    k_all = jnp.concatenate(
        [_rope_tile(kvp_ref[:, :B_KV_WIDTH], cosp_ref[...], sinp_ref[...], take_upper),
         _rope_tile(kvc_ref[:, :B_KV_WIDTH], cosc, sinc, take_upper)], axis=0)
    v_all = jnp.concatenate([kvp_ref[:, B_KV_WIDTH:], kvc_ref[:, B_KV_WIDTH:]], axis=0)
    k_swap = pltpu.roll(k_all, B_HEAD_DIM, 1)
    v_swap = pltpu.roll(v_all, B_HEAD_DIM, 1)
    zero = jnp.zeros_like(k_all)
    k_lo = (jnp.where(low_half, k_all, zero).astype(BF16), jnp.where(low_half, k_swap, zero).astype(BF16))
    k_hi = (jnp.where(low_half, zero, k_swap).astype(BF16), jnp.where(low_half, zero, k_all).astype(BF16))
    v_lo = (jnp.where(low_half, v_all, zero).astype(BF16), jnp.where(low_half, v_swap, zero).astype(BF16))
    v_hi = (jnp.where(low_half, zero, v_swap).astype(BF16), jnp.where(low_half, zero, v_all).astype(BF16))

    scale = B_HEAD_DIM ** -0.5
    q_tiles = [
        (_rope_tile(q_ref[:, j * LANES:(j + 1) * LANES], cosc, sinc, take_upper) * scale).astype(BF16)
        for j in range(B_WIDTH // LANES)]

    tiles_per_kv = B_WIDTH // LANES // B_KV_HEADS
    rows_s = tiles_per_kv * WINDOW
    qi = lax.broadcasted_iota(jnp.int32, (rows_s, 2 * WINDOW), 0) % WINDOW
    kj = lax.broadcasted_iota(jnp.int32, (rows_s, 2 * WINDOW), 1)
    diff = qi + WINDOW - kj
    band = (diff >= 0) & (diff < WINDOW)
    first_key = jnp.where(pl.program_id(1) == 0, WINDOW, 0)

    for n in range(t_rows // WINDOW):
        rows = slice(n * WINDOW, (n + 1) * WINDOW)
        band_rows = slice(n * WINDOW, (n + 2) * WINDOW)
        valid = band
        if n == 0:
            valid = band & (kj >= first_key)
        for kv in range(B_KV_HEADS):
            q4 = jnp.concatenate(
                [q_tiles[kv * tiles_per_kv + j][rows] for j in range(tiles_per_kv)], axis=0)
            k_cat = jnp.concatenate([k_lo[kv][band_rows], k_hi[kv][band_rows]], axis=0)
            s = lax.dot_general(q4, k_cat, (((1,), (1,)), ((), ())),
                                preferred_element_type=F32)
            p_halves = []
            inv_l = []
            for half in range(2):
                sh = jnp.where(valid, s[:, half * 2 * WINDOW:(half + 1) * 2 * WINDOW], -jnp.inf)
                sink = jnp.concatenate(
                    [jnp.full((WINDOW, 1), sinks_ref[layer, 2 * (kv * tiles_per_kv + j) + half], F32)
                     for j in range(tiles_per_kv)], axis=0)
                m = jnp.maximum(jnp.max(sh, axis=-1, keepdims=True), sink)
                p = jnp.exp(sh - m)
                l = jnp.sum(p, axis=-1, keepdims=True) + jnp.exp(sink - m)
                p_halves.append(p.astype(BF16))
                inv_l.append(1.0 / l)
            p_cat = jnp.concatenate(p_halves, axis=1)
            v_cat = jnp.concatenate([v_lo[kv][band_rows], v_hi[kv][band_rows]], axis=0)
            o = jnp.dot(p_cat, v_cat, preferred_element_type=F32)
            o = o * jnp.where(low_half, inv_l[0], inv_l[1])
            for j in range(tiles_per_kv):
                jj = kv * tiles_per_kv + j
                gate_ref = bg_refs[jj // 2]
                gate_cols = slice((jj % 2) * LANES, (jj % 2 + 1) * LANES)
                out_cols = slice(A_WIDTH + jj * LANES, A_WIDTH + (jj + 1) * LANES)
                o_ref[rows, out_cols] = (o[j * WINDOW:(j + 1) * WINDOW]
                                         * _silu(gate_ref[rows, gate_cols])).astype(o_ref.dtype)


def _even_mix(h, sinks, cos_t, sin_t, ln_g, ln_b, ws, bst, w_out, layer, batch, seq):
    m = h.shape[0]
    t = MIX_T
    n_s = seq // t
    blocks_per_tile = t // WINDOW
    k_out, n_out = w_out.shape[1:]
    slab = k_out // (batch * n_s)

    def row(b, s):
        return b * n_s + s

    def prev(b, s):
        return jnp.maximum(row(b, s) * blocks_per_tile - 1, 0)

    wide = lambda c: pl.BlockSpec((t, A_WIDTH), lambda b, s: (row(b, s), c))
    gate = lambda c: pl.BlockSpec((t, KV_BLOCK), lambda b, s: (row(b, s), GATE_B_COL + c))
    return pl.pallas_call(
        partial(_even_mix_kernel, layer=layer),
        grid=(batch, n_s),
        in_specs=[
            pl.BlockSpec(memory_space=pltpu.SMEM),
            wide(0), wide(1), wide(2), wide(3),
            pl.BlockSpec((t, KV_BLOCK), lambda b, s: (row(b, s), KV_COL)),
            pl.BlockSpec((WINDOW, KV_BLOCK), lambda b, s: (prev(b, s), KV_COL)),
            gate(0), gate(1), gate(2), gate(3),
            pl.BlockSpec((t, LANES), lambda b, s: (row(b, s), 0)),
            pl.BlockSpec((t, LANES), lambda b, s: (row(b, s), 0)),
            pl.BlockSpec((WINDOW, LANES), lambda b, s: (prev(b, s), 0)),
            pl.BlockSpec((WINDOW, LANES), lambda b, s: (prev(b, s), 0)),
            _layer_block((1, A_WIDTH), layer),
            _layer_block((1, A_WIDTH), layer),
            _layer_block((A_GROUPS, CHUNK, CHUNK), layer),
            _layer_block((CHUNK, A_GROUPS), layer),
            pl.BlockSpec((None, slab, n_out), lambda b, s: (layer, row(b, s), 0)),
        ],
        out_specs=[
            pl.BlockSpec((t, MIX_WIDTH), lambda b, s: (row(b, s), 0)),
            pl.BlockSpec((slab, n_out), lambda b, s: (row(b, s), 0)),
        ],
        out_shape=[
            jax.ShapeDtypeStruct((m, MIX_WIDTH), BF16),
            jax.ShapeDtypeStruct((k_out, n_out), BF16),
        ],
        compiler_params=_compiler_params("parallel", "parallel"),
        name="even_mix",
    )(sinks, h, h, h, h, h, h, h, h, h, h, cos_t, sin_t, cos_t, sin_t, ln_g, ln_b, ws, bst, w_out)


def _odd_mix_kernel(xc_ref, cg_ref, xd_ref, dg_ref, convw_ref, convb_ref, wa_ref, wx_ref, ba_ref,
                    bx_ref, lam_ref, wpool_ref, dscale_ref, wout_ref, o_ref, wout_bf_ref,
                    ext_c, ext_d, lvl2, lvl4, hstate, sa, sb, sh, wax_bf, wpool_bf):
    t_rows = xc_ref.shape[0]
    n_groups = t_rows // SUBLANES
    s_idx = pl.program_id(1)

    wout_bf_ref[...] = wout_ref[...].astype(BF16)

    @pl.when(s_idx == 0)
    def _():
        ext_c[:, 0:CONV_HALO, :] = jnp.zeros((C_HEADS, CONV_HALO, LANES), F32)
        ext_d[:, 0:POOL_HALO, :] = jnp.zeros((D_WIDTH // LANES, POOL_HALO, LANES), F32)
        hstate[...] = jnp.zeros_like(hstate)
        wax_bf[:, :, :C_HEAD_DIM] = wa_ref[...].astype(BF16)
        wax_bf[:, :, C_HEAD_DIM:] = wx_ref[...].astype(BF16)
        wpool_bf[...] = wpool_ref[...].astype(BF16)

    @pl.when(s_idx > 0)
    def _():
        ext_c[:, 0:CONV_HALO, :] = ext_c[:, t_rows:t_rows + CONV_HALO, :]
        ext_d[:, 0:POOL_HALO, :] = ext_d[:, t_rows:t_rows + POOL_HALO, :]

    z = -lam_ref[...]
    softplus = jnp.maximum(z, 0.0) + jnp.log1p(jnp.exp(-jnp.abs(z)))
    for hd in range(C_HEADS):
        cols = slice(hd * C_HEAD_DIM, (hd + 1) * C_HEAD_DIM)
        ext_c[hd, CONV_HALO:, :] = xc_ref[:, cols]
        xconv = convb_ref[:, cols]
        for j in range(CONV_WIDTH):
            lo = CONV_HALO - (CONV_WIDTH - 1) + j
            xconv = xconv + ext_c[hd, lo:lo + t_rows, :] * convw_ref[j:j + 1, cols]
        ri = jnp.dot(xconv.astype(BF16), wax_bf[hd], preferred_element_type=F32)
        r = _sigmoid(ri[:, :C_HEAD_DIM] + ba_ref[:, cols])
        i = _sigmoid(ri[:, C_HEAD_DIM:] + bx_ref[:, cols])
        log_a = -LRU_C * r * softplus[:, cols]
        a = jnp.exp(log_a)
        m2 = -jnp.tanh(log_a) * (a * a + 1.0)
        mult = jnp.where(m2 == 0.0, 0.0, m2 * lax.rsqrt(m2))
        b = mult * (i * xconv)
        for g in range(n_groups):
            base = (g * C_HEADS + hd) * SUBLANES
            sa[base:base + SUBLANES, :] = a[g * SUBLANES:(g + 1) * SUBLANES]
            sb[base:base + SUBLANES, :] = b[g * SUBLANES:(g + 1) * SUBLANES]

    h = hstate[...]
    for t in range(t_rows):
        g, s = divmod(t, SUBLANES)
        step = pl.ds(g * C_HEADS * SUBLANES + s, C_HEADS, stride=SUBLANES)
        h = sa[step, :] * h + sb[step, :]
        sh[step, :] = h
    hstate[...] = h
    for hd in range(C_HEADS):
        cols = slice(hd * C_HEAD_DIM, (hd + 1) * C_HEAD_DIM)
        hcol = jnp.concatenate(
            [sh[(g * C_HEADS + hd) * SUBLANES:(g * C_HEADS + hd + 1) * SUBLANES, :]
             for g in range(n_groups)], axis=0)
        o_ref[:, cols] = (hcol * _silu(cg_ref[:, cols])).astype(o_ref.dtype)

    n_ext = POOL_HALO + t_rows
    tiles_per_group = D_GROUP_DIM // LANES
    pos1 = s_idx * t_rows + lax.broadcasted_iota(jnp.int32, (t_rows, LANES), 0) + 1
    for g, w in enumerate(POOL_WINDOWS):
        inv = 1.0 / jnp.minimum(pos1, w).astype(F32)
        pooled = []
        for k in range(tiles_per_group):
            tile = g * tiles_per_group + k
            cols = slice(tile * LANES, (tile + 1) * LANES)
            ext_d[tile, POOL_HALO:, :] = xd_ref[:, cols]
            lo = SUBLANES
            win = ext_d[tile, lo:n_ext, :] + ext_d[tile, lo - 1:n_ext - 1, :]
            if w >= 4:
                lvl2[k, lo:n_ext, :] = win
                lo += SUBLANES
                win = lvl2[k, lo:n_ext, :] + lvl2[k, lo - 2:n_ext - 2, :]
            if w >= 8:
                lvl4[k, lo:n_ext, :] = win
                lo += SUBLANES
                win = lvl4[k, lo:n_ext, :] + lvl4[k, lo - 4:n_ext - 4, :]
            if w >= 16:
                lo += SUBLANES
                win = win[SUBLANES:] + win[:-SUBLANES]
            pooled.append((win[POOL_HALO - lo:] * inv - xd_ref[:, cols]).astype(BF16))
        cols = slice(g * D_GROUP_DIM, (g + 1) * D_GROUP_DIM)
        mixed = jnp.dot(jnp.concatenate(pooled, axis=1), wpool_bf[g], preferred_element_type=F32)
        out_cols = slice(C_WIDTH + g * D_GROUP_DIM, C_WIDTH + (g + 1) * D_GROUP_DIM)
        o_ref[:, out_cols] = (mixed * dscale_ref[:, cols] * _silu(dg_ref[:, cols])).astype(o_ref.dtype)


def _odd_mix(h, conv_w, conv_b, w_a, w_x, b_a, b_x, lam, w_pool, d_scale, w_out, layer, batch, seq):
    m = h.shape[0]
    t = MIX_T
    n_s = seq // t
    k_out, n_out = w_out.shape[1:]
    slab = k_out // (batch * n_s)
    wide = lambda c: pl.BlockSpec((t, C_WIDTH), lambda b, s: (b * n_s + s, c))
    vec = _layer_block((1, C_WIDTH), layer)
    n_pool = len(POOL_WINDOWS)
    return pl.pallas_call(
        _odd_mix_kernel,
        grid=(batch, n_s),
        in_specs=[
            wide(0), wide(1), wide(2), wide(3),
            _layer_block((CONV_WIDTH, C_WIDTH), layer),
            vec,
            _layer_block((C_HEADS, C_HEAD_DIM, C_HEAD_DIM), layer),
            _layer_block((C_HEADS, C_HEAD_DIM, C_HEAD_DIM), layer),
            vec, vec, vec,
            _layer_block((n_pool, D_GROUP_DIM, D_GROUP_DIM), layer),
            vec,
            pl.BlockSpec((None, slab, n_out), lambda b, s: (layer, b * n_s + s, 0)),
        ],
        out_specs=[
            pl.BlockSpec((t, MIX_WIDTH), lambda b, s: (b * n_s + s, 0)),
            pl.BlockSpec((slab, n_out), lambda b, s: (b * n_s + s, 0)),
        ],
        out_shape=[
            jax.ShapeDtypeStruct((m, MIX_WIDTH), BF16),
            jax.ShapeDtypeStruct((k_out, n_out), BF16),
        ],
        scratch_shapes=[
            pltpu.VMEM((C_WIDTH // LANES, CONV_HALO + t, LANES), F32),
            pltpu.VMEM((D_WIDTH // LANES, POOL_HALO + t, LANES), F32),
            pltpu.VMEM((D_GROUP_DIM // LANES, POOL_HALO + t, LANES), F32),
            pltpu.VMEM((D_GROUP_DIM // LANES, POOL_HALO + t, LANES), F32),
            pltpu.VMEM((SUBLANES, LANES), F32),
            pltpu.VMEM((t * C_HEADS, LANES), F32),
            pltpu.VMEM((t * C_HEADS, LANES), F32),
            pltpu.VMEM((t * C_HEADS, LANES), F32),
            pltpu.VMEM((C_HEADS, C_HEAD_DIM, 2 * C_HEAD_DIM), BF16),
            pltpu.VMEM((n_pool, D_GROUP_DIM, D_GROUP_DIM), BF16),
        ],
        compiler_params=_compiler_params("parallel", "arbitrary"),
        name="odd_mix",
    )(h, h, h, h, conv_w, conv_b, w_a, w_x, b_a, b_x, lam, w_pool, d_scale, w_out)


def _rope_tables(positions):
    half = ROT_DIM // 2
    inv_freq = ROPE_THETA ** (-jnp.arange(0, ROT_DIM, 2, dtype=F32) / ROT_DIM)
    rest = jnp.zeros((B_HEAD_DIM - ROT_DIM,), F32)
    freq_head = jnp.concatenate([inv_freq, inv_freq, rest])
    sign_head = jnp.concatenate([-jnp.ones((half,), F32), jnp.ones((half,), F32), rest])
    reps = LANES // B_HEAD_DIM
    ang = positions.astype(F32).reshape(-1, 1) * jnp.tile(freq_head, reps)[None, :]
    return jnp.cos(ang), jnp.sin(ang) * jnp.tile(sign_head, reps)[None, :]


def _rows(p):
    return p.reshape(p.shape[0], 1, p.shape[1])


def kernel(x, positions, even_w_in, even_a_ln_g, even_a_ln_b, even_a_ws, even_a_bs, even_b_sinks, even_w_out, even_ln_g, even_ln_b, odd_w_in, odd_conv_w, odd_conv_b, odd_w_a, odd_b_a, odd_w_x, odd_b_x, odd_lam, odd_w_pool, odd_d_scale, odd_w_out, odd_ln_g, odd_ln_b):
    batch, seq, d = x.shape
    m = batch * seq
    xf = x.reshape(m, d)
    cos_t, sin_t = _rope_tables(positions)
    even_bst = jnp.swapaxes(even_a_bs, 1, 2)
    even_vecs = [_rows(p) for p in (even_a_ln_g, even_a_ln_b, even_ln_g, even_ln_b)]
    odd_vecs = [_rows(p) for p in (odd_conv_b, odd_b_a, odd_b_x, odd_lam, odd_d_scale, odd_ln_g, odd_ln_b)]
    w_in = even_w_in[0].astype(BF16)
    for layer in range(DEPTH):
        j = layer // 2
        h = _proj_in(xf, w_in)
        if layer % 2 == 0:
            a_ln_g, a_ln_b, ln_g, ln_b = even_vecs
            mix, w_out = _even_mix(h, even_b_sinks, cos_t, sin_t, a_ln_g, a_ln_b, even_a_ws, even_bst,
                                   even_w_out, j, batch, seq)
            w_next, j_next = odd_w_in, j
        else:
            conv_b, b_a, b_x, lam, d_scale, ln_g, ln_b = odd_vecs
            mix, w_out = _odd_mix(h, odd_conv_w, conv_b, odd_w_a, odd_w_x, b_a, b_x, lam, odd_w_pool,
                                  d_scale, odd_w_out, j, batch, seq)
            w_next, j_next = even_w_in, j + 1
        if layer + 1 == DEPTH:
            w_next = None
        xf, w_in = _proj_out_ln(mix, w_out, xf, ln_g, ln_b, j, w_next, j_next)
    return xf.reshape(batch, seq, d)
```

```python
from functools import partial

import jax
import jax.numpy as jnp
from jax import lax
from jax.experimental import pallas as pl
from jax.experimental.pallas import tpu as pltpu

D_MODEL = 2048
DEPTH = 4
A_WIDTH = 1024
A_GROUPS = 8
CHUNK = 128
B_HEAD_DIM = 64
B_Q_HEADS = 16
B_KV_HEADS = 2
B_WIDTH = 1024
B_KV_WIDTH = 128
WINDOW = 128
ROT_DIM = 16
ROPE_THETA = 500000.0
C_WIDTH = 1024
C_HEADS = 8
C_HEAD_DIM = 128
CONV_WIDTH = 4
LRU_C = 8.0
D_WIDTH = 1024
POOL_WINDOWS = (2, 4, 8, 16)
D_GROUP_DIM = 256
MIX_WIDTH = 2048
DN_ALPHA = (2 * DEPTH) ** 0.25
LN_EPS = 1e-5

LANES = 128
SUBLANES = 8
VMEM_LIMIT_BYTES = 56 * 1024 * 1024

PROJ_TM = 256
OUT_TM = 512
OUT_SUB = 256
MIX_T = 256
CONV_HALO = SUBLANES
POOL_HALO = 4 * SUBLANES

KV_BLOCK = 2 * B_KV_WIDTH
KV_COL = (3 * A_WIDTH + B_WIDTH) // KV_BLOCK
GATE_B_COL = KV_COL + 1
GATE_B_BLOCKS = B_WIDTH // KV_BLOCK

F32 = jnp.float32
BF16 = jnp.bfloat16


def _sigmoid(x):
    return 0.5 * jnp.tanh(0.5 * x) + 0.5


def _silu(x):
    hx = 0.5 * x
    return hx * jnp.tanh(hx) + hx


def _resident(shape):
    return pl.BlockSpec(shape, lambda *_: (0,) * len(shape), pipeline_mode=pl.Buffered(1))


def _whole(arr):
    return pl.BlockSpec(arr.shape, lambda *_: (0,) * arr.ndim)


def _layer_block(shape, layer):
    return pl.BlockSpec((None,) + shape, lambda *_: (layer,) + (0,) * len(shape))


def _compiler_params(*semantics):
    return pltpu.CompilerParams(dimension_semantics=semantics, vmem_limit_bytes=VMEM_LIMIT_BYTES)


def _proj_in_kernel(x_ref, w_ref, o_ref):
    o_ref[...] = jnp.dot(x_ref[...].astype(BF16), w_ref[...], preferred_element_type=F32)


def _proj_in(x, w):
    m, k = x.shape
    n = w.shape[1]
    return pl.pallas_call(
        _proj_in_kernel,
        grid=(m // PROJ_TM,),
        in_specs=[pl.BlockSpec((PROJ_TM, k), lambda i: (i, 0)), _resident((k, n))],
        out_specs=pl.BlockSpec((PROJ_TM, n), lambda i: (i, 0)),
        out_shape=jax.ShapeDtypeStruct((m, n), F32),
        compiler_params=_compiler_params("parallel"),
        name="proj_in",
    )(x, w)


def _proj_out_ln_kernel(*refs, layer, cast_next):
    if cast_next:
        mix_ref, w_ref, x_ref, g_ref, b_ref, wn_ref, o_ref, wn_out_ref = refs
        wn_out_ref[...] = wn_ref[...].astype(BF16)
    else:
        mix_ref, w_ref, x_ref, g_ref, b_ref, o_ref = refs
    g = g_ref[layer:layer + 1, :]
    b = b_ref[layer:layer + 1, :]
    for r in range(OUT_TM // OUT_SUB):
        rows = slice(r * OUT_SUB, (r + 1) * OUT_SUB)
        y = jnp.dot(mix_ref[rows, :], w_ref[...], preferred_element_type=F32)
        z = DN_ALPHA * x_ref[rows, :] + y
        mu = jnp.mean(z, axis=-1, keepdims=True)
        zc = z - mu
        var = jnp.mean(zc * zc, axis=-1, keepdims=True)
        o_ref[rows, :] = zc * lax.rsqrt(var + LN_EPS) * g + b


def _proj_out_ln(mix, w, x, ln_g, ln_b, layer, w_next=None, layer_next=0):
    m, k = mix.shape
    n = w.shape[1]
    steps = m // OUT_TM
    in_specs = [
        pl.BlockSpec((OUT_TM, k), lambda i: (i, 0)),
        _resident((k, n)),
        pl.BlockSpec((OUT_TM, n), lambda i: (i, 0)),
        _whole(ln_g),
        _whole(ln_b),
    ]
    out_specs = [pl.BlockSpec((OUT_TM, n), lambda i: (i, 0))]
    out_shape = [jax.ShapeDtypeStruct((m, n), F32)]
    args = [mix, w, x, ln_g, ln_b]
    if w_next is not None:
        kn, nn = w_next.shape[1:]
        slab = kn // steps
        in_specs.append(pl.BlockSpec((None, slab, nn), lambda i: (layer_next, i, 0)))
        out_specs.append(pl.BlockSpec((slab, nn), lambda i: (i, 0)))
        out_shape.append(jax.ShapeDtypeStruct((kn, nn), BF16))
        args.append(w_next)
    outs = pl.pallas_call(
        partial(_proj_out_ln_kernel, layer=layer, cast_next=w_next is not None),
        grid=(steps,),
        in_specs=in_specs,
        out_specs=out_specs,
        out_shape=out_shape,
        compiler_params=_compiler_params("parallel"),
        name="proj_out_ln",
    )(*args)
    return outs if w_next is not None else (outs[0], None)


def _rope_tile(t, cos, sin, take_upper):
    upper = pltpu.roll(t, LANES - ROT_DIM // 2, 1)
    lower = pltpu.roll(t, ROT_DIM // 2, 1)
    return t * cos + jnp.where(take_upper, upper, lower) * sin


def _even_mix_kernel(sinks_ref, u_ref, v_ref, ag_ref, q_ref, kvc_ref, kvp_ref, bg0_ref, bg1_ref,
                     bg2_ref, bg3_ref, cosc_ref, sinc_ref, cosp_ref, sinp_ref, lng_ref, lnb_ref,
                     ws_ref, bst_ref, wout_ref, o_ref, wout_bf_ref, *, layer):
    t_rows = u_ref.shape[0]
    n_chunks = t_rows // CHUNK
    bg_refs = (bg0_ref, bg1_ref, bg2_ref, bg3_ref)

    wout_bf_ref[...] = wout_ref[...].astype(BF16)

    v = v_ref[...]
    mu = jnp.mean(v, axis=-1, keepdims=True)
    vc = v - mu
    var = jnp.mean(vc * vc, axis=-1, keepdims=True)
    vln = (vc * lax.rsqrt(var + LN_EPS) * lng_ref[layer:layer + 1, :]
           + lnb_ref[layer:layer + 1, :]).astype(BF16)
    causal = (lax.broadcasted_iota(jnp.int32, (CHUNK, CHUNK), 0)
              >= lax.broadcasted_iota(jnp.int32, (CHUNK, CHUNK), 1))
    for g in range(A_GROUPS):
        cols = slice(g * CHUNK, (g + 1) * CHUNK)
        w = jnp.where(causal, ws_ref[g], 0.0).astype(BF16)
        rhs = jnp.concatenate(
            [vln[c * CHUNK:(c + 1) * CHUNK, cols] for c in range(n_chunks)], axis=1)
        mixed = jnp.dot(w, rhs, preferred_element_type=F32) + bst_ref[:, g:g + 1]
        for c in range(n_chunks):
            rows = slice(c * CHUNK, (c + 1) * CHUNK)
            gate = _silu(ag_ref[rows, cols])
            o_ref[rows, cols] = (u_ref[rows, cols] * mixed[:, c * CHUNK:(c + 1) * CHUNK]
                                 * gate).astype(o_ref.dtype)

    lane = lax.broadcasted_iota(jnp.int32, (1, LANES), 1)
    take_upper = (lane % B_HEAD_DIM) < (ROT_DIM // 2)
    low_half = lane < B_HEAD_DIM
    cosc = cosc_ref[...]
    sinc = sinc_ref[...]
    k_all = jnp.concatenate(
        [_rope_tile(kvp_ref[:, :B_KV_WIDTH], cosp_ref[...], sinp_ref[...], take_upper),
         _rope_tile(kvc_ref[:, :B_KV_WIDTH], cosc, sinc, take_upper)], axis=0)
    k_swap = pltpu.roll(k_all, B_HEAD_DIM, 1)
    zero = jnp.zeros_like(k_all)
    k_lo = (jnp.where(low_half, k_all, zero).astype(BF16), jnp.where(low_half, k_swap, zero).astype(BF16))
    k_hi = (jnp.where(low_half, zero, k_swap).astype(BF16), jnp.where(low_half, zero, k_all).astype(BF16))
    v_all = jnp.concatenate([kvp_ref[:, B_KV_WIDTH:], kvc_ref[:, B_KV_WIDTH:]], axis=0)
    vt_all = v_all.T.astype(BF16)

    scale = B_HEAD_DIM ** -0.5
    q_tiles = [
        (_rope_tile(q_ref[:, j * LANES:(j + 1) * LANES], cosc, sinc, take_upper) * scale).astype(BF16)
        for j in range(B_WIDTH // LANES)]

    tiles_per_kv = B_WIDTH // LANES // B_KV_HEADS
    band_keys = 2 * WINDOW
    kj = lax.broadcasted_iota(jnp.int32, (band_keys, WINDOW), 0)
    qi = lax.broadcasted_iota(jnp.int32, (band_keys, WINDOW), 1)
    diff = qi + WINDOW - kj
    band = (diff >= 0) & (diff < WINDOW)
    first_key = jnp.where(pl.program_id(1) == 0, WINDOW, 0)
    neg_inf = jnp.full((band_keys, WINDOW), -jnp.inf, F32)
    mask_any = jnp.where(band, 0.0, neg_inf)
    mask_first = jnp.where(band & (kj >= first_key), 0.0, neg_inf)

    for n in range(t_rows // WINDOW):
        rows = slice(n * WINDOW, (n + 1) * WINDOW)
        band_rows = slice(n * WINDOW, (n + 2) * WINDOW)
        mask = jnp.concatenate([mask_first if n == 0 else mask_any] * tiles_per_kv, axis=1)
        for kv in range(B_KV_HEADS):
            q4 = jnp.concatenate(
                [q_tiles[kv * tiles_per_kv + j][rows] for j in range(tiles_per_kv)], axis=0)
            k_cat = jnp.concatenate([k_lo[kv][band_rows], k_hi[kv][band_rows]], axis=0)
            st = lax.dot_general(k_cat, q4, (((1,), (1,)), ((), ())), preferred_element_type=F32)
            vt = vt_all[kv * B_HEAD_DIM:(kv + 1) * B_HEAD_DIM, band_rows]
            o_halves = []
            for half in range(2):
                sh = st[half * band_keys:(half + 1) * band_keys] + mask
                sink = jnp.concatenate(
                    [jnp.full((1, WINDOW), sinks_ref[layer, 2 * (kv * tiles_per_kv + j) + half], F32)
                     for j in range(tiles_per_kv)], axis=1)
                m = jnp.maximum(jnp.max(sh, axis=0, keepdims=True), sink)
                p = jnp.exp(sh - m)
                l = jnp.sum(p, axis=0, keepdims=True) + jnp.exp(sink - m)
                o_half = jnp.dot(vt, p.astype(BF16), preferred_element_type=F32)
                o_halves.append(o_half * (1.0 / l))
            ot = jnp.concatenate(o_halves, axis=0)
            for j in range(tiles_per_kv):
                jj = kv * tiles_per_kv + j
                gate_ref = bg_refs[jj // 2]
                gate_cols = slice((jj % 2) * LANES, (jj % 2 + 1) * LANES)
                out_cols = slice(A_WIDTH + jj * LANES, A_WIDTH + (jj + 1) * LANES)
                o_tile = ot[:, j * WINDOW:(j + 1) * WINDOW].T
                o_ref[rows, out_cols] = (o_tile * _silu(gate_ref[rows, gate_cols])).astype(o_ref.dtype)


def _even_mix(h, sinks, cos_t, sin_t, ln_g, ln_b, ws, bst, w_out, layer, batch, seq):
    m = h.shape[0]
    t = MIX_T
    n_s = seq // t
    blocks_per_tile = t // WINDOW
    k_out, n_out = w_out.shape[1:]
    slab = k_out // (batch * n_s)

    def row(b, s):
        return b * n_s + s

    def prev(b, s):
        return jnp.maximum(row(b, s) * blocks_per_tile - 1, 0)

    wide = lambda c: pl.BlockSpec((t, A_WIDTH), lambda b, s: (row(b, s), c))
    gate = lambda c: pl.BlockSpec((t, KV_BLOCK), lambda b, s: (row(b, s), GATE_B_COL + c))
    return pl.pallas_call(
        partial(_even_mix_kernel, layer=layer),
        grid=(batch, n_s),
        in_specs=[
            pl.BlockSpec(memory_space=pltpu.SMEM),
            wide(0), wide(1), wide(2), wide(3),
            pl.BlockSpec((t, KV_BLOCK), lambda b, s: (row(b, s), KV_COL)),
            pl.BlockSpec((WINDOW, KV_BLOCK), lambda b, s: (prev(b, s), KV_COL)),
            gate(0), gate(1), gate(2), gate(3),
            pl.BlockSpec((t, LANES), lambda b, s: (row(b, s), 0)),
            pl.BlockSpec((t, LANES), lambda b, s: (row(b, s), 0)),
            pl.BlockSpec((WINDOW, LANES), lambda b, s: (prev(b, s), 0)),
            pl.BlockSpec((WINDOW, LANES), lambda b, s: (prev(b, s), 0)),
            _whole(ln_g),
            _whole(ln_b),
            _layer_block((A_GROUPS, CHUNK, CHUNK), layer),
            _layer_block((CHUNK, A_GROUPS), layer),
            pl.BlockSpec((None, slab, n_out), lambda b, s: (layer, row(b, s), 0)),
        ],
        out_specs=[
            pl.BlockSpec((t, MIX_WIDTH), lambda b, s: (row(b, s), 0)),
            pl.BlockSpec((slab, n_out), lambda b, s: (row(b, s), 0)),
        ],
        out_shape=[
            jax.ShapeDtypeStruct((m, MIX_WIDTH), BF16),
            jax.ShapeDtypeStruct((k_out, n_out), BF16),
        ],
        compiler_params=_compiler_params("parallel", "parallel"),
        name="even_mix",
    )(sinks, h, h, h, h, h, h, h, h, h, h, cos_t, sin_t, cos_t, sin_t, ln_g, ln_b, ws, bst, w_out)


def _odd_mix_kernel(xc_ref, cg_ref, xd_ref, dg_ref, convw_ref, convb_ref, wa_ref, wx_ref, ba_ref,
                    bx_ref, lam_ref, wpool_ref, dscale_ref, wout_ref, o_ref, wout_bf_ref,
                    ext_c, ext_d, lvl2, lvl4, hstate, sa, sb, sh, wax_bf, wpool_bf, *, layer):
    t_rows = xc_ref.shape[0]
    row = slice(layer, layer + 1)
    n_groups = t_rows // SUBLANES
    s_idx = pl.program_id(1)

    wout_bf_ref[...] = wout_ref[...].astype(BF16)

    @pl.when(s_idx == 0)
    def _():
        ext_c[:, 0:CONV_HALO, :] = jnp.zeros((C_HEADS, CONV_HALO, LANES), F32)
        ext_d[:, 0:POOL_HALO, :] = jnp.zeros((D_WIDTH // LANES, POOL_HALO, LANES), F32)
        hstate[...] = jnp.zeros_like(hstate)
        wax_bf[:, :, :C_HEAD_DIM] = wa_ref[...].astype(BF16)
        wax_bf[:, :, C_HEAD_DIM:] = wx_ref[...].astype(BF16)
        wpool_bf[...] = wpool_ref[...].astype(BF16)

    @pl.when(s_idx > 0)
    def _():
        ext_c[:, 0:CONV_HALO, :] = ext_c[:, t_rows:t_rows + CONV_HALO, :]
        ext_d[:, 0:POOL_HALO, :] = ext_d[:, t_rows:t_rows + POOL_HALO, :]

    z = -lam_ref[row, :]
    softplus = jnp.maximum(z, 0.0) + jnp.log1p(jnp.exp(-jnp.abs(z)))
    for hd in range(C_HEADS):
        cols = slice(hd * C_HEAD_DIM, (hd + 1) * C_HEAD_DIM)
        ext_c[hd, CONV_HALO:, :] = xc_ref[:, cols]
        xconv = convb_ref[row, cols]
        for j in range(CONV_WIDTH):
            lo = CONV_HALO - (CONV_WIDTH - 1) + j
            xconv = xconv + ext_c[hd, lo:lo + t_rows, :] * convw_ref[j:j + 1, cols]
        ri = jnp.dot(xconv.astype(BF16), wax_bf[hd], preferred_element_type=F32)
        r = _sigmoid(ri[:, :C_HEAD_DIM] + ba_ref[row, cols])
        i = _sigmoid(ri[:, C_HEAD_DIM:] + bx_ref[row, cols])
        log_a = -LRU_C * r * softplus[:, cols]
        a = jnp.exp(log_a)
        m2 = -jnp.tanh(log_a) * (a * a + 1.0)
        mult = jnp.where(m2 == 0.0, 0.0, m2 * lax.rsqrt(m2))
        b = mult * (i * xconv)
        for g in range(n_groups):
            base = (g * C_HEADS + hd) * SUBLANES
            sa[base:base + SUBLANES, :] = a[g * SUBLANES:(g + 1) * SUBLANES]
            sb[base:base + SUBLANES, :] = b[g * SUBLANES:(g + 1) * SUBLANES]

    h = hstate[...]
    for t in range(t_rows):
        g, s = divmod(t, SUBLANES)
        step = pl.ds(g * C_HEADS * SUBLANES + s, C_HEADS, stride=SUBLANES)
        h = sa[step, :] * h + sb[step, :]
        sh[step, :] = h
    hstate[...] = h
    for hd in range(C_HEADS):
        cols = slice(hd * C_HEAD_DIM, (hd + 1) * C_HEAD_DIM)
        hcol = jnp.concatenate(
            [sh[(g * C_HEADS + hd) * SUBLANES:(g * C_HEADS + hd + 1) * SUBLANES, :]
             for g in range(n_groups)], axis=0)
        o_ref[:, cols] = (hcol * _silu(cg_ref[:, cols])).astype(o_ref.dtype)

    n_ext = POOL_HALO + t_rows
    tiles_per_group = D_GROUP_DIM // LANES
    pos1 = s_idx * t_rows + lax.broadcasted_iota(jnp.int32, (t_rows, LANES), 0) + 1
    for g, w in enumerate(POOL_WINDOWS):
        inv = 1.0 / jnp.minimum(pos1, w).astype(F32)
        pooled = []
        for k in range(tiles_per_group):
            tile = g * tiles_per_group + k
            cols = slice(tile * LANES, (tile + 1) * LANES)
            ext_d[tile, POOL_HALO:, :] = xd_ref[:, cols]
            lo = SUBLANES
            win = ext_d[tile, lo:n_ext, :] + ext_d[tile, lo - 1:n_ext - 1, :]
            if w >= 4:
                lvl2[k, lo:n_ext, :] = win
                lo += SUBLANES
                win = lvl2[k, lo:n_ext, :] + lvl2[k, lo - 2:n_ext - 2, :]
            if w >= 8:
                lvl4[k, lo:n_ext, :] = win
                lo += SUBLANES
                win = lvl4[k, lo:n_ext, :] + lvl4[k, lo - 4:n_ext - 4, :]
            if w >= 16:
                lo += SUBLANES
                win = win[SUBLANES:] + win[:-SUBLANES]
            pooled.append((win[POOL_HALO - lo:] * inv - xd_ref[:, cols]).astype(BF16))
        cols = slice(g * D_GROUP_DIM, (g + 1) * D_GROUP_DIM)
        mixed = jnp.dot(jnp.concatenate(pooled, axis=1), wpool_bf[g], preferred_element_type=F32)
        out_cols = slice(C_WIDTH + g * D_GROUP_DIM, C_WIDTH + (g + 1) * D_GROUP_DIM)
        o_ref[:, out_cols] = (mixed * dscale_ref[row, cols] * _silu(dg_ref[:, cols])).astype(o_ref.dtype)


def _odd_mix(h, conv_w, conv_b, w_a, w_x, b_a, b_x, lam, w_pool, d_scale, w_out, layer, batch, seq):
    m = h.shape[0]
    t = MIX_T
    n_s = seq // t
    k_out, n_out = w_out.shape[1:]
    slab = k_out // (batch * n_s)
    wide = lambda c: pl.BlockSpec((t, C_WIDTH), lambda b, s: (b * n_s + s, c))
    n_pool = len(POOL_WINDOWS)
    return pl.pallas_call(
        partial(_odd_mix_kernel, layer=layer),
        grid=(batch, n_s),
        in_specs=[
            wide(0), wide(1), wide(2), wide(3),
            _layer_block((CONV_WIDTH, C_WIDTH), layer),
            _whole(conv_b),
            _layer_block((C_HEADS, C_HEAD_DIM, C_HEAD_DIM), layer),
            _layer_block((C_HEADS, C_HEAD_DIM, C_HEAD_DIM), layer),
            _whole(b_a), _whole(b_x), _whole(lam),
            _layer_block((n_pool, D_GROUP_DIM, D_GROUP_DIM), layer),
            _whole(d_scale),
            pl.BlockSpec((None, slab, n_out), lambda b, s: (layer, b * n_s + s, 0)),
        ],
        out_specs=[
            pl.BlockSpec((t, MIX_WIDTH), lambda b, s: (b * n_s + s, 0)),
            pl.BlockSpec((slab, n_out), lambda b, s: (b * n_s + s, 0)),
        ],
        out_shape=[
            jax.ShapeDtypeStruct((m, MIX_WIDTH), BF16),
            jax.ShapeDtypeStruct((k_out, n_out), BF16),
        ],
        scratch_shapes=[
            pltpu.VMEM((C_WIDTH // LANES, CONV_HALO + t, LANES), F32),
            pltpu.VMEM((D_WIDTH // LANES, POOL_HALO + t, LANES), F32),
            pltpu.VMEM((D_GROUP_DIM // LANES, POOL_HALO + t, LANES), F32),
            pltpu.VMEM((D_GROUP_DIM // LANES, POOL_HALO + t, LANES), F32),
            pltpu.VMEM((SUBLANES, LANES), F32),
            pltpu.VMEM((t * C_HEADS, LANES), F32),
            pltpu.VMEM((t * C_HEADS, LANES), F32),
            pltpu.VMEM((t * C_HEADS, LANES), F32),
            pltpu.VMEM((C_HEADS, C_HEAD_DIM, 2 * C_HEAD_DIM), BF16),
            pltpu.VMEM((n_pool, D_GROUP_DIM, D_GROUP_DIM), BF16),
        ],
        compiler_params=_compiler_params("parallel", "arbitrary"),
        name="odd_mix",
    )(h, h, h, h, conv_w, conv_b, w_a, w_x, b_a, b_x, lam, w_pool, d_scale, w_out)


def _rope_tables(positions):
    half = ROT_DIM // 2
    inv_freq = ROPE_THETA ** (-jnp.arange(0, ROT_DIM, 2, dtype=F32) / ROT_DIM)
    rest = jnp.zeros((B_HEAD_DIM - ROT_DIM,), F32)
    freq_head = jnp.concatenate([inv_freq, inv_freq, rest])
    sign_head = jnp.concatenate([-jnp.ones((half,), F32), jnp.ones((half,), F32), rest])
    reps = LANES // B_HEAD_DIM
    ang = positions.astype(F32).reshape(-1, 1) * jnp.tile(freq_head, reps)[None, :]
    return jnp.cos(ang), jnp.sin(ang) * jnp.tile(sign_head, reps)[None, :]


def kernel(x, positions, even_w_in, even_a_ln_g, even_a_ln_b, even_a_ws, even_a_bs, even_b_sinks, even_w_out, even_ln_g, even_ln_b, odd_w_in, odd_conv_w, odd_conv_b, odd_w_a, odd_b_a, odd_w_x, odd_b_x, odd_lam, odd_w_pool, odd_d_scale, odd_w_out, odd_ln_g, odd_ln_b):
    batch, seq, d = x.shape
    m = batch * seq
    xf = x.reshape(m, d)
    cos_t, sin_t = _rope_tables(positions)
    even_bst = jnp.swapaxes(even_a_bs, 1, 2)
    w_in = even_w_in[0].astype(BF16)
    for layer in range(DEPTH):
        j = layer // 2
        h = _proj_in(xf, w_in)
        if layer % 2 == 0:
            ln_g, ln_b = even_ln_g, even_ln_b
            mix, w_out = _even_mix(h, even_b_sinks, cos_t, sin_t, even_a_ln_g, even_a_ln_b, even_a_ws,
                                   even_bst, even_w_out, j, batch, seq)
            w_next, j_next = odd_w_in, j
        else:
            ln_g, ln_b = odd_ln_g, odd_ln_b
            mix, w_out = _odd_mix(h, odd_conv_w, odd_conv_b, odd_w_a, odd_w_x, odd_b_a, odd_b_x, odd_lam,
                                  odd_w_pool, odd_d_scale, odd_w_out, j, batch, seq)
            w_next, j_next = even_w_in, j + 1
        if layer + 1 == DEPTH:
            w_next = None
        xf, w_in = _proj_out_ln(mix, w_out, xf, ln_g, ln_b, j, w_next, j_next)
    return xf.reshape(batch, seq, d)
```

```python
from functools import partial

import jax
import jax.numpy as jnp
from jax import lax
from jax.experimental import pallas as pl
from jax.experimental.pallas import tpu as pltpu

D_MODEL = 2048
DEPTH = 4
A_WIDTH = 1024
A_GROUPS = 8
CHUNK = 128
B_HEAD_DIM = 64
B_Q_HEADS = 16
B_KV_HEADS = 2
B_WIDTH = 1024
B_KV_WIDTH = 128
WINDOW = 128
ROT_DIM = 16
ROPE_THETA = 500000.0
C_WIDTH = 1024
C_HEADS = 8
C_HEAD_DIM = 128
CONV_WIDTH = 4
LRU_C = 8.0
D_WIDTH = 1024
POOL_WINDOWS = (2, 4, 8, 16)
D_GROUP_DIM = 256
MIX_WIDTH = 2048
DN_ALPHA = (2 * DEPTH) ** 0.25
LN_EPS = 1e-5

LANES = 128
SUBLANES = 8
VMEM_LIMIT_BYTES = 56 * 1024 * 1024

PROJ_TM = 256
MIX_T = 256
OUT_BLOCK = 256
OUT_BLOCKS = D_MODEL // OUT_BLOCK
CONV_HALO = SUBLANES
POOL_HALO = 4 * SUBLANES

KV_BLOCK = 2 * B_KV_WIDTH
KV_COL = (3 * A_WIDTH + B_WIDTH) // KV_BLOCK
GATE_B_COL = KV_COL + 1

F32 = jnp.float32
BF16 = jnp.bfloat16


def _sigmoid(x):
    return 0.5 * jnp.tanh(0.5 * x) + 0.5


def _silu(x):
    hx = 0.5 * x
    return hx * jnp.tanh(hx) + hx


def _resident(shape):
    return pl.BlockSpec(shape, lambda *_: (0,) * len(shape), pipeline_mode=pl.Buffered(1))


def _whole(arr):
    return pl.BlockSpec(arr.shape, lambda *_: (0,) * arr.ndim)


def _layer_block(shape, layer):
    return pl.BlockSpec((None,) + shape, lambda *_: (layer,) + (0,) * len(shape))


def _compiler_params(*semantics):
    return pltpu.CompilerParams(dimension_semantics=semantics, vmem_limit_bytes=VMEM_LIMIT_BYTES)


def _proj_in_kernel(x_ref, w_ref, wout_ref, o_ref, wout_bf_ref):
    wout_bf_ref[...] = wout_ref[...].astype(BF16)
    o_ref[...] = jnp.dot(x_ref[...].astype(BF16), w_ref[...], preferred_element_type=F32)


def _proj_in(x, w, w_out, layer):
    m, k = x.shape
    n = w.shape[1]
    steps = m // PROJ_TM
    k_out, n_out = w_out.shape[1:]
    slab = k_out // steps
    return pl.pallas_call(
        _proj_in_kernel,
        grid=(steps,),
        in_specs=[
            pl.BlockSpec((PROJ_TM, k), lambda i: (i, 0)),
            _resident((k, n)),
            pl.BlockSpec((None, slab, n_out), lambda i: (layer, i, 0)),
        ],
        out_specs=[
            pl.BlockSpec((PROJ_TM, n), lambda i: (i, 0)),
            pl.BlockSpec((slab, n_out), lambda i: (i, 0)),
        ],
        out_shape=[
            jax.ShapeDtypeStruct((m, n), F32),
            jax.ShapeDtypeStruct((k_out, n_out), BF16),
        ],
        compiler_params=_compiler_params("parallel"),
        name="proj_in",
    )(x, w, w_out)


class _OutProj:
    def __init__(self, mix_ref, wout_ref, x_ref, z_ref):
        self.refs = (mix_ref, wout_ref, x_ref, z_ref)
        self.done = 0

    def emit(self, count=1):
        mix_ref, wout_ref, x_ref, z_ref = self.refs
        for _ in range(min(count, OUT_BLOCKS - self.done)):
            cols = slice(self.done * OUT_BLOCK, (self.done + 1) * OUT_BLOCK)
            z_ref[:, cols] = DN_ALPHA * x_ref[:, cols] + jnp.dot(
                mix_ref[...], wout_ref[:, cols], preferred_element_type=F32)
            self.done += 1

    def finish_ln(self, g_ref, b_ref, layer, o_ref):
        self.emit(OUT_BLOCKS)
        z = self.refs[3][...]
        mu = jnp.mean(z, axis=-1, keepdims=True)
        zc = z - mu
        var = jnp.mean(zc * zc, axis=-1, keepdims=True)
        o_ref[...] = (zc * lax.rsqrt(var + LN_EPS) * g_ref[layer:layer + 1, :]
                      + b_ref[layer:layer + 1, :])


def _lagged_tiles(n_tiles):
    def cur(i):
        return jnp.minimum(i, n_tiles - 1)

    def out(i):
        return jnp.maximum(i - 1, 0)

    return cur, out


def _tail_specs(x, w_out_bf, ln_g, ln_b, w_next, layer_next, cur, out, n_tiles):
    m, d = x.shape
    in_specs = [
        pl.BlockSpec((MIX_T, d), lambda i: (out(i), 0)),
        _resident(w_out_bf.shape),
        _whole(ln_g),
        _whole(ln_b),
    ]
    out_specs = [pl.BlockSpec((MIX_T, d), lambda i: (out(i), 0))]
    out_shape = [jax.ShapeDtypeStruct((m, d), F32)]
    args = [x, w_out_bf, ln_g, ln_b]
    if w_next is not None:
        kn, nn = w_next.shape[1:]
        slab = kn // n_tiles
        in_specs.append(pl.BlockSpec((None, slab, nn), lambda i: (layer_next, cur(i), 0)))
        out_specs.append(pl.BlockSpec((slab, nn), lambda i: (cur(i), 0)))
        out_shape.append(jax.ShapeDtypeStruct((kn, nn), BF16))
        args.append(w_next)
    return in_specs, out_specs, out_shape, args


def _rope_tile(t, cos, sin, take_upper):
    upper = pltpu.roll(t, LANES - ROT_DIM // 2, 1)
    lower = pltpu.roll(t, ROT_DIM // 2, 1)
    return t * cos + jnp.where(take_upper, upper, lower) * sin


def _even_layer_kernel(*refs, layer, cast_next, n_tiles, n_s):
    (sinks_ref, u_ref, v_ref, ag_ref, q_ref, kvc_ref, kvp_ref, bg0_ref, bg1_ref, bg2_ref, bg3_ref,
     cosc_ref, sinc_ref, cosp_ref, sinp_ref, lng_ref, lnb_ref, ws_ref, bst_ref,
     x_ref, wout_ref, g_ref, b_ref) = refs[:23]
    if cast_next:
        wn_ref, o_ref, wn_bf_ref, mix_ref, mix_prev, z_scr = refs[23:]
        wn_bf_ref[...] = wn_ref[...].astype(BF16)
    else:
        o_ref, mix_ref, mix_prev, z_scr = refs[23:]
    t_rows = u_ref.shape[0]
    n_chunks = t_rows // CHUNK
    bg_refs = (bg0_ref, bg1_ref, bg2_ref, bg3_ref)
    step = pl.program_id(0)
    seq_tile = jnp.minimum(step, n_tiles - 1) % n_s

    @pl.when(step == 0)
    def _():
        mix_prev[...] = jnp.zeros_like(mix_prev)

    out_proj = _OutProj(mix_prev, wout_ref, x_ref, z_scr)

    v = v_ref[...]
    mu = jnp.mean(v, axis=-1, keepdims=True)
    vc = v - mu
    var = jnp.mean(vc * vc, axis=-1, keepdims=True)
    vln = (vc * lax.rsqrt(var + LN_EPS) * lng_ref[layer:layer + 1, :]
           + lnb_ref[layer:layer + 1, :]).astype(BF16)
    causal = (lax.broadcasted_iota(jnp.int32, (CHUNK, CHUNK), 0)
              >= lax.broadcasted_iota(jnp.int32, (CHUNK, CHUNK), 1))
    lane = lax.broadcasted_iota(jnp.int32, (1, LANES), 1)
    take_upper = (lane % B_HEAD_DIM) < (ROT_DIM // 2)
    low_half = lane < B_HEAD_DIM
    cosc = cosc_ref[...]
    sinc = sinc_ref[...]
    scale = B_HEAD_DIM ** -0.5
    q_tiles = []
    for g in range(A_GROUPS):
        cols = slice(g * CHUNK, (g + 1) * CHUNK)
        w = jnp.where(causal, ws_ref[g], 0.0).astype(BF16)
        rhs = jnp.concatenate(
            [vln[c * CHUNK:(c + 1) * CHUNK, cols] for c in range(n_chunks)], axis=1)
        mixed = jnp.dot(w, rhs, preferred_element_type=F32) + bst_ref[:, g:g + 1]
        for c in range(n_chunks):
            rows = slice(c * CHUNK, (c + 1) * CHUNK)
            gate = _silu(ag_ref[rows, cols])
            mix_ref[rows, cols] = (u_ref[rows, cols] * mixed[:, c * CHUNK:(c + 1) * CHUNK]
                                   * gate).astype(BF16)
        q_tiles.append(
            (_rope_tile(q_ref[:, cols], cosc, sinc, take_upper) * scale).astype(BF16))
        if g % 2 == 1:
            out_proj.emit()

    k_all = jnp.concatenate(
        [_rope_tile(kvp_ref[:, :B_KV_WIDTH], cosp_ref[...], sinp_ref[...], take_upper),
         _rope_tile(kvc_ref[:, :B_KV_WIDTH], cosc, sinc, take_upper)], axis=0)
    k_swap = pltpu.roll(k_all, B_HEAD_DIM, 1)
    zero = jnp.zeros_like(k_all)
    k_lo = (jnp.where(low_half, k_all, zero).astype(BF16), jnp.where(low_half, k_swap, zero).astype(BF16))
    k_hi = (jnp.where(low_half, zero, k_swap).astype(BF16), jnp.where(low_half, zero, k_all).astype(BF16))
    v_all = jnp.concatenate([kvp_ref[:, B_KV_WIDTH:], kvc_ref[:, B_KV_WIDTH:]], axis=0)
    vt_all = v_all.T.astype(BF16)

    tiles_per_kv = B_WIDTH // LANES // B_KV_HEADS
    band_keys = 2 * WINDOW
    kj = lax.broadcasted_iota(jnp.int32, (band_keys, WINDOW), 0)
    qi = lax.broadcasted_iota(jnp.int32, (band_keys, WINDOW), 1)
    diff = qi + WINDOW - kj
    band = (diff >= 0) & (diff < WINDOW)
    first_key = jnp.where(seq_tile == 0, WINDOW, 0)
    neg_inf = jnp.full((band_keys, WINDOW), -jnp.inf, F32)
    mask_any = jnp.where(band, 0.0, neg_inf)
    mask_first = jnp.where(band & (kj >= first_key), 0.0, neg_inf)

    for n in range(t_rows // WINDOW):
        rows = slice(n * WINDOW, (n + 1) * WINDOW)
        band_rows = slice(n * WINDOW, (n + 2) * WINDOW)
        mask = jnp.concatenate([mask_first if n == 0 else mask_any] * tiles_per_kv, axis=1)
        for kv in range(B_KV_HEADS):
            out_proj.emit()
            q4 = jnp.concatenate(
                [q_tiles[kv * tiles_per_kv + j][rows] for j in range(tiles_per_kv)], axis=0)
            k_cat = jnp.concatenate([k_lo[kv][band_rows], k_hi[kv][band_rows]], axis=0)
            st = lax.dot_general(k_cat, q4, (((1,), (1,)), ((), ())), preferred_element_type=F32)
            vt = vt_all[kv * B_HEAD_DIM:(kv + 1) * B_HEAD_DIM, band_rows]
            o_halves = []
            for half in range(2):
                sh = st[half * band_keys:(half + 1) * band_keys] + mask
                sink = jnp.concatenate(
                    [jnp.full((1, WINDOW), sinks_ref[layer, 2 * (kv * tiles_per_kv + j) + half], F32)
                     for j in range(tiles_per_kv)], axis=1)
                m = jnp.maximum(jnp.max(sh, axis=0, keepdims=True), sink)
                p = jnp.exp(sh - m)
                l = jnp.sum(p, axis=0, keepdims=True) + jnp.exp(sink - m)
                o_half = jnp.dot(vt, p.astype(BF16), preferred_element_type=F32)
                o_halves.append(o_half * (1.0 / l))
            ot = jnp.concatenate(o_halves, axis=0)
            for j in range(tiles_per_kv):
                jj = kv * tiles_per_kv + j
                gate_ref = bg_refs[jj // 2]
                gate_cols = slice((jj % 2) * LANES, (jj % 2 + 1) * LANES)
                out_cols = slice(A_WIDTH + jj * LANES, A_WIDTH + (jj + 1) * LANES)
                o_tile = ot[:, j * WINDOW:(j + 1) * WINDOW].T
                mix_ref[rows, out_cols] = (o_tile * _silu(gate_ref[rows, gate_cols])).astype(BF16)

    out_proj.finish_ln(g_ref, b_ref, layer, o_ref)
    mix_prev[...] = mix_ref[...]


def _even_layer(h, x, sinks, cos_t, sin_t, a_ln_g, a_ln_b, ws, bst, w_out_bf, ln_g, ln_b, layer,
                w_next, layer_next, seq):
    m = h.shape[0]
    t = MIX_T
    n_tiles = m // t
    n_s = seq // t
    blocks_per_tile = t // WINDOW
    cur, out = _lagged_tiles(n_tiles)

    def prev(i):
        return jnp.maximum(cur(i) * blocks_per_tile - 1, 0)

    wide = lambda c: pl.BlockSpec((t, A_WIDTH), lambda i: (cur(i), c))
    gate = lambda c: pl.BlockSpec((t, KV_BLOCK), lambda i: (cur(i), GATE_B_COL + c))
    tail_in, out_specs, out_shape, tail_args = _tail_specs(
        x, w_out_bf, ln_g, ln_b, w_next, layer_next, cur, out, n_tiles)
    outs = pl.pallas_call(
        partial(_even_layer_kernel, layer=layer, cast_next=w_next is not None, n_tiles=n_tiles, n_s=n_s),
        grid=(n_tiles + 1,),
        in_specs=[
            pl.BlockSpec(memory_space=pltpu.SMEM),
            wide(0), wide(1), wide(2), wide(3),
            pl.BlockSpec((t, KV_BLOCK), lambda i: (cur(i), KV_COL)),
            pl.BlockSpec((WINDOW, KV_BLOCK), lambda i: (prev(i), KV_COL)),
            gate(0), gate(1), gate(2), gate(3),
            pl.BlockSpec((t, LANES), lambda i: (cur(i), 0)),
            pl.BlockSpec((t, LANES), lambda i: (cur(i), 0)),
            pl.BlockSpec((WINDOW, LANES), lambda i: (prev(i), 0)),
            pl.BlockSpec((WINDOW, LANES), lambda i: (prev(i), 0)),
            _whole(a_ln_g),
            _whole(a_ln_b),
            _layer_block((A_GROUPS, CHUNK, CHUNK), layer),
            _layer_block((CHUNK, A_GROUPS), layer),
        ] + tail_in,
        out_specs=out_specs,
        out_shape=out_shape,
        scratch_shapes=[pltpu.VMEM((t, MIX_WIDTH), BF16), pltpu.VMEM((t, MIX_WIDTH), BF16),
                        pltpu.VMEM((t, D_MODEL), F32)],
        compiler_params=_compiler_params("arbitrary"),
        name="even_layer",
    )(sinks, h, h, h, h, h, h, h, h, h, h, cos_t, sin_t, cos_t, sin_t, a_ln_g, a_ln_b, ws, bst,
      *tail_args)
    return outs if w_next is not None else (outs[0], None)


def _odd_layer_kernel(*refs, layer, cast_next, n_tiles, n_s):
    (xc_ref, cg_ref, xd_ref, dg_ref, convw_ref, convb_ref, wa_ref, wx_ref, ba_ref, bx_ref, lam_ref,
     wpool_ref, dscale_ref, x_ref, wout_ref, g_ref, b_ref) = refs[:17]
    if cast_next:
        wn_ref, o_ref, wn_bf_ref = refs[17:20]
        wn_bf_ref[...] = wn_ref[...].astype(BF16)
        scratch = refs[20:]
    else:
        o_ref = refs[17]
        scratch = refs[18:]
    mix_ref, mix_prev, z_scr, ext_c, ext_d, lvl2, lvl4, hstate, sa, sb, sh, wax_bf, wpool_bf = scratch
    t_rows = xc_ref.shape[0]
    row = slice(layer, layer + 1)
    n_groups = t_rows // SUBLANES
    step = pl.program_id(0)
    s_idx = jnp.minimum(step, n_tiles - 1) % n_s

    @pl.when(step == 0)
    def _():
        mix_prev[...] = jnp.zeros_like(mix_prev)

    @pl.when(s_idx == 0)
    def _():
        ext_c[:, 0:CONV_HALO, :] = jnp.zeros((C_HEADS, CONV_HALO, LANES), F32)
        ext_d[:, 0:POOL_HALO, :] = jnp.zeros((D_WIDTH // LANES, POOL_HALO, LANES), F32)
        hstate[...] = jnp.zeros_like(hstate)
        wax_bf[:, :, :C_HEAD_DIM] = wa_ref[...].astype(BF16)
        wax_bf[:, :, C_HEAD_DIM:] = wx_ref[...].astype(BF16)
        wpool_bf[...] = wpool_ref[...].astype(BF16)

    @pl.when(s_idx > 0)
    def _():
        ext_c[:, 0:CONV_HALO, :] = ext_c[:, t_rows:t_rows + CONV_HALO, :]
        ext_d[:, 0:POOL_HALO, :] = ext_d[:, t_rows:t_rows + POOL_HALO, :]

    out_proj = _OutProj(mix_prev, wout_ref, x_ref, z_scr)

    z = -lam_ref[row, :]
    softplus = jnp.maximum(z, 0.0) + jnp.log1p(jnp.exp(-jnp.abs(z)))
    for hd in range(C_HEADS):
        cols = slice(hd * C_HEAD_DIM, (hd + 1) * C_HEAD_DIM)
        ext_c[hd, CONV_HALO:, :] = xc_ref[:, cols]
        xconv = convb_ref[row, cols]
        for j in range(CONV_WIDTH):
            lo = CONV_HALO - (CONV_WIDTH - 1) + j
            xconv = xconv + ext_c[hd, lo:lo + t_rows, :] * convw_ref[j:j + 1, cols]
        ri = jnp.dot(xconv.astype(BF16), wax_bf[hd], preferred_element_type=F32)
        r = _sigmoid(ri[:, :C_HEAD_DIM] + ba_ref[row, cols])
        i = _sigmoid(ri[:, C_HEAD_DIM:] + bx_ref[row, cols])
        log_a = -LRU_C * r * softplus[:, cols]
        a = jnp.exp(log_a)
        m2 = -jnp.tanh(log_a) * (a * a + 1.0)
        mult = jnp.where(m2 == 0.0, 0.0, m2 * lax.rsqrt(m2))
        b = mult * (i * xconv)
        for g in range(n_groups):
            base = (g * C_HEADS + hd) * SUBLANES
            sa[base:base + SUBLANES, :] = a[g * SUBLANES:(g + 1) * SUBLANES]
            sb[base:base + SUBLANES, :] = b[g * SUBLANES:(g + 1) * SUBLANES]
        if hd in (1, 3, 5, 6, 7):
            out_proj.emit()

    out_proj.emit()
    h = hstate[...]
    for t in range(t_rows):
        g, s = divmod(t, SUBLANES)
        at_t = pl.ds(g * C_HEADS * SUBLANES + s, C_HEADS, stride=SUBLANES)
        h = sa[at_t, :] * h + sb[at_t, :]
        sh[at_t, :] = h
    hstate[...] = h
    out_proj.emit()
    for hd in range(C_HEADS):
        cols = slice(hd * C_HEAD_DIM, (hd + 1) * C_HEAD_DIM)
        hcol = jnp.concatenate(
            [sh[(g * C_HEADS + hd) * SUBLANES:(g * C_HEADS + hd + 1) * SUBLANES, :]
             for g in range(n_groups)], axis=0)
        mix_ref[:, cols] = (hcol * _silu(cg_ref[:, cols])).astype(BF16)

    n_ext = POOL_HALO + t_rows
    tiles_per_group = D_GROUP_DIM // LANES
    pos1 = s_idx * t_rows + lax.broadcasted_iota(jnp.int32, (t_rows, LANES), 0) + 1
    for g, w in enumerate(POOL_WINDOWS):
        if g == 2:
            out_proj.emit()
        inv = 1.0 / jnp.minimum(pos1, w).astype(F32)
        pooled = []
        for k in range(tiles_per_group):
            tile = g * tiles_per_group + k
            cols = slice(tile * LANES, (tile + 1) * LANES)
            ext_d[tile, POOL_HALO:, :] = xd_ref[:, cols]
            lo = SUBLANES
            win = ext_d[tile, lo:n_ext, :] + ext_d[tile, lo - 1:n_ext - 1, :]
            if w >= 4:
                lvl2[k, lo:n_ext, :] = win
                lo += SUBLANES
                win = lvl2[k, lo:n_ext, :] + lvl2[k, lo - 2:n_ext - 2, :]
            if w >= 8:
                lvl4[k, lo:n_ext, :] = win
                lo += SUBLANES
                win = lvl4[k, lo:n_ext, :] + lvl4[k, lo - 4:n_ext - 4, :]
            if w >= 16:
                lo += SUBLANES
                win = win[SUBLANES:] + win[:-SUBLANES]
            pooled.append((win[POOL_HALO - lo:] * inv - xd_ref[:, cols]).astype(BF16))
        cols = slice(g * D_GROUP_DIM, (g + 1) * D_GROUP_DIM)
        mixed = jnp.dot(jnp.concatenate(pooled, axis=1), wpool_bf[g], preferred_element_type=F32)
        out_cols = slice(C_WIDTH + g * D_GROUP_DIM, C_WIDTH + (g + 1) * D_GROUP_DIM)
        mix_ref[:, out_cols] = (mixed * dscale_ref[row, cols] * _silu(dg_ref[:, cols])).astype(BF16)

    out_proj.finish_ln(g_ref, b_ref, layer, o_ref)
    mix_prev[...] = mix_ref[...]


def _odd_layer(h, x, conv_w, conv_b, w_a, w_x, b_a, b_x, lam, w_pool, d_scale, w_out_bf, ln_g, ln_b,
               layer, w_next, layer_next, seq):
    m = h.shape[0]
    t = MIX_T
    n_tiles = m // t
    n_s = seq // t
    cur, out = _lagged_tiles(n_tiles)
    wide = lambda c: pl.BlockSpec((t, C_WIDTH), lambda i: (cur(i), c))
    n_pool = len(POOL_WINDOWS)
    tail_in, out_specs, out_shape, tail_args = _tail_specs(
        x, w_out_bf, ln_g, ln_b, w_next, layer_next, cur, out, n_tiles)
    outs = pl.pallas_call(
        partial(_odd_layer_kernel, layer=layer, cast_next=w_next is not None, n_tiles=n_tiles, n_s=n_s),
        grid=(n_tiles + 1,),
        in_specs=[
            wide(0), wide(1), wide(2), wide(3),
            _layer_block((CONV_WIDTH, C_WIDTH), layer),
            _whole(conv_b),
            _layer_block((C_HEADS, C_HEAD_DIM, C_HEAD_DIM), layer),
            _layer_block((C_HEADS, C_HEAD_DIM, C_HEAD_DIM), layer),
            _whole(b_a), _whole(b_x), _whole(lam),
            _layer_block((n_pool, D_GROUP_DIM, D_GROUP_DIM), layer),
            _whole(d_scale),
        ] + tail_in,
        out_specs=out_specs,
        out_shape=out_shape,
        scratch_shapes=[
            pltpu.VMEM((t, MIX_WIDTH), BF16),
            pltpu.VMEM((t, MIX_WIDTH), BF16),
            pltpu.VMEM((t, D_MODEL), F32),
            pltpu.VMEM((C_WIDTH // LANES, CONV_HALO + t, LANES), F32),
            pltpu.VMEM((D_WIDTH // LANES, POOL_HALO + t, LANES), F32),
            pltpu.VMEM((D_GROUP_DIM // LANES, POOL_HALO + t, LANES), F32),
            pltpu.VMEM((D_GROUP_DIM // LANES, POOL_HALO + t, LANES), F32),
            pltpu.VMEM((SUBLANES, LANES), F32),
            pltpu.VMEM((t * C_HEADS, LANES), F32),
            pltpu.VMEM((t * C_HEADS, LANES), F32),
            pltpu.VMEM((t * C_HEADS, LANES), F32),
            pltpu.VMEM((C_HEADS, C_HEAD_DIM, 2 * C_HEAD_DIM), BF16),
            pltpu.VMEM((n_pool, D_GROUP_DIM, D_GROUP_DIM), BF16),
        ],
        compiler_params=_compiler_params("arbitrary"),
        name="odd_layer",
    )(h, h, h, h, conv_w, conv_b, w_a, w_x, b_a, b_x, lam, w_pool, d_scale, *tail_args)
    return outs if w_next is not None else (outs[0], None)


def _rope_tables(positions):
    half = ROT_DIM // 2
    inv_freq = ROPE_THETA ** (-jnp.arange(0, ROT_DIM, 2, dtype=F32) / ROT_DIM)
    rest = jnp.zeros((B_HEAD_DIM - ROT_DIM,), F32)
    freq_head = jnp.concatenate([inv_freq, inv_freq, rest])
    sign_head = jnp.concatenate([-jnp.ones((half,), F32), jnp.ones((half,), F32), rest])
    reps = LANES // B_HEAD_DIM
    ang = positions.astype(F32).reshape(-1, 1) * jnp.tile(freq_head, reps)[None, :]
    return jnp.cos(ang), jnp.sin(ang) * jnp.tile(sign_head, reps)[None, :]


def kernel(x, positions, even_w_in, even_a_ln_g, even_a_ln_b, even_a_ws, even_a_bs, even_b_sinks, even_w_out, even_ln_g, even_ln_b, odd_w_in, odd_conv_w, odd_conv_b, odd_w_a, odd_b_a, odd_w_x, odd_b_x, odd_lam, odd_w_pool, odd_d_scale, odd_w_out, odd_ln_g, odd_ln_b):
    batch, seq, d = x.shape
    m = batch * seq
    xf = x.reshape(m, d)
    cos_t, sin_t = _rope_tables(positions)
    even_bst = jnp.swapaxes(even_a_bs, 1, 2)
    w_in = even_w_in[0].astype(BF16)
    for layer in range(DEPTH):
        j = layer // 2
        if layer % 2 == 0:
            h, w_out = _proj_in(xf, w_in, even_w_out, j)
            xf, w_in = _even_layer(h, xf, even_b_sinks, cos_t, sin_t, even_a_ln_g, even_a_ln_b, even_a_ws,
                                   even_bst, w_out, even_ln_g, even_ln_b, j, odd_w_in, j, seq)
        else:
            h, w_out = _proj_in(xf, w_in, odd_w_out, j)
            w_next = even_w_in if layer + 1 < DEPTH else None
            xf, w_in = _odd_layer(h, xf, odd_conv_w, odd_conv_b, odd_w_a, odd_w_x, odd_b_a, odd_b_x, odd_lam,
                                  odd_w_pool, odd_d_scale, w_out, odd_ln_g, odd_ln_b, j, w_next, j + 1, seq)
    return xf.reshape(batch, seq, d)
```

```python
from functools import partial

import jax
import jax.numpy as jnp
from jax import lax
from jax.experimental import pallas as pl
from jax.experimental.pallas import tpu as pltpu

D_MODEL = 2048
DEPTH = 4
A_WIDTH = 1024
A_GROUPS = 8
CHUNK = 128
B_HEAD_DIM = 64
B_Q_HEADS = 16
B_KV_HEADS = 2
B_WIDTH = 1024
B_KV_WIDTH = 128
WINDOW = 128
ROT_DIM = 16
ROPE_THETA = 500000.0
C_WIDTH = 1024
C_HEADS = 8
C_HEAD_DIM = 128
CONV_WIDTH = 4
LRU_C = 8.0
D_WIDTH = 1024
POOL_WINDOWS = (2, 4, 8, 16)
D_GROUP_DIM = 256
MIX_WIDTH = 2048
DN_ALPHA = (2 * DEPTH) ** 0.25
LN_EPS = 1e-5

LANES = 128
SUBLANES = 8
VMEM_LIMIT_BYTES = 56 * 1024 * 1024

PROJ_TM = 256
MIX_T = 256
OUT_BLOCK = 256
OUT_BLOCKS = D_MODEL // OUT_BLOCK
CONV_HALO = SUBLANES
POOL_HALO = 4 * SUBLANES

KV_BLOCK = 2 * B_KV_WIDTH
KV_COL = (3 * A_WIDTH + B_WIDTH) // KV_BLOCK
GATE_B_COL = KV_COL + 1

F32 = jnp.float32
BF16 = jnp.bfloat16


def _sigmoid(x):
    return 0.5 * jnp.tanh(0.5 * x) + 0.5


def _silu(x):
    hx = 0.5 * x
    return hx * jnp.tanh(hx) + hx


def _resident(shape):
    return pl.BlockSpec(shape, lambda *_: (0,) * len(shape), pipeline_mode=pl.Buffered(1))


def _whole(arr):
    return pl.BlockSpec(arr.shape, lambda *_: (0,) * arr.ndim)


def _layer_block(shape, layer):
    return pl.BlockSpec((None,) + shape, lambda *_: (layer,) + (0,) * len(shape))


def _compiler_params(*semantics):
    return pltpu.CompilerParams(dimension_semantics=semantics, vmem_limit_bytes=VMEM_LIMIT_BYTES)


def _proj_in_kernel(*refs, cast_next):
    if cast_next:
        x_ref, w_ref, wout_ref, wn_ref, o_ref, wout_bf_ref, wn_bf_ref = refs
        wn_bf_ref[...] = wn_ref[...].astype(BF16)
    else:
        x_ref, w_ref, wout_ref, o_ref, wout_bf_ref = refs
    wout_bf_ref[...] = wout_ref[...].astype(BF16)
    o_ref[...] = jnp.dot(x_ref[...].astype(BF16), w_ref[...], preferred_element_type=F32)


def _proj_in(x, w, w_out, layer, w_next, layer_next):
    m, k = x.shape
    n = w.shape[1]
    steps = m // PROJ_TM

    def slab_specs(stacked, which):
        rows, cols = stacked.shape[1:]
        slab = rows // steps
        return (pl.BlockSpec((None, slab, cols), lambda i: (which, i, 0)),
                pl.BlockSpec((slab, cols), lambda i: (i, 0)),
                jax.ShapeDtypeStruct((rows, cols), BF16))

    casts = [slab_specs(w_out, layer)]
    args = [x, w, w_out]
    if w_next is not None:
        casts.append(slab_specs(w_next, layer_next))
        args.append(w_next)
    outs = pl.pallas_call(
        partial(_proj_in_kernel, cast_next=w_next is not None),
        grid=(steps,),
        in_specs=[pl.BlockSpec((PROJ_TM, k), lambda i: (i, 0)), _resident((k, n))]
        + [c[0] for c in casts],
        out_specs=[pl.BlockSpec((PROJ_TM, n), lambda i: (i, 0))] + [c[1] for c in casts],
        out_shape=[jax.ShapeDtypeStruct((m, n), F32)] + [c[2] for c in casts],
        compiler_params=_compiler_params("parallel"),
        name="proj_in",
    )(*args)
    return outs if w_next is not None else (*outs, None)


class _OutProj:
    def __init__(self, mix_ref, wout_ref, x_ref, z_ref):
        self.refs = (mix_ref, wout_ref, x_ref, z_ref)
        self.done = 0

    def emit(self, count=1):
        mix_ref, wout_ref, x_ref, z_ref = self.refs
        for _ in range(min(count, OUT_BLOCKS - self.done)):
            cols = slice(self.done * OUT_BLOCK, (self.done + 1) * OUT_BLOCK)
            z_ref[:, cols] = DN_ALPHA * x_ref[:, cols] + jnp.dot(
                mix_ref[...], wout_ref[:, cols], preferred_element_type=F32)
            self.done += 1

    def finish_ln(self, g_ref, b_ref, layer, o_ref):
        self.emit(OUT_BLOCKS)
        z = self.refs[3][...]
        mu = jnp.mean(z, axis=-1, keepdims=True)
        zc = z - mu
        var = jnp.mean(zc * zc, axis=-1, keepdims=True)
        o_ref[...] = (zc * lax.rsqrt(var + LN_EPS) * g_ref[layer:layer + 1, :]
                      + b_ref[layer:layer + 1, :])


def _lagged_tiles(n_tiles):
    def cur(i):
        return jnp.minimum(i, n_tiles - 1)

    def out(i):
        return jnp.maximum(i - 1, 0)

    return cur, out


def _tail_specs(x, w_out_bf, ln_g, ln_b, out):
    m, d = x.shape
    in_specs = [
        pl.BlockSpec((MIX_T, d), lambda i: (out(i), 0)),
        _resident(w_out_bf.shape),
        _whole(ln_g),
        _whole(ln_b),
    ]
    out_spec = pl.BlockSpec((MIX_T, d), lambda i: (out(i), 0))
    out_shape = jax.ShapeDtypeStruct((m, d), F32)
    return in_specs, out_spec, out_shape, [x, w_out_bf, ln_g, ln_b]


def _rope_tile(t, cos, sin, take_upper):
    upper = pltpu.roll(t, LANES - ROT_DIM // 2, 1)
    lower = pltpu.roll(t, ROT_DIM // 2, 1)
    return t * cos + jnp.where(take_upper, upper, lower) * sin


def _even_layer_kernel(sinks_ref, u_ref, v_ref, ag_ref, q_ref, kvc_ref, kvp_ref, bg0_ref, bg1_ref,
                       bg2_ref, bg3_ref, cosc_ref, sinc_ref, cosp_ref, sinp_ref, lng_ref, lnb_ref,
                       ws_ref, bst_ref, x_ref, wout_ref, g_ref, b_ref, o_ref,
                       mix_ref, mix_prev, z_scr, *, layer, n_tiles, n_s):
    t_rows = u_ref.shape[0]
    n_chunks = t_rows // CHUNK
    bg_refs = (bg0_ref, bg1_ref, bg2_ref, bg3_ref)
    step = pl.program_id(0)
    seq_tile = jnp.minimum(step, n_tiles - 1) % n_s

    @pl.when(step == 0)
    def _():
        mix_prev[...] = jnp.zeros_like(mix_prev)

    out_proj = _OutProj(mix_prev, wout_ref, x_ref, z_scr)

    v = v_ref[...]
    mu = jnp.mean(v, axis=-1, keepdims=True)
    vc = v - mu
    var = jnp.mean(vc * vc, axis=-1, keepdims=True)
    vln = (vc * lax.rsqrt(var + LN_EPS) * lng_ref[layer:layer + 1, :]
           + lnb_ref[layer:layer + 1, :]).astype(BF16)
    causal = (lax.broadcasted_iota(jnp.int32, (CHUNK, CHUNK), 0)
              >= lax.broadcasted_iota(jnp.int32, (CHUNK, CHUNK), 1))
    lane = lax.broadcasted_iota(jnp.int32, (1, LANES), 1)
    take_upper = (lane % B_HEAD_DIM) < (ROT_DIM // 2)
    low_half = lane < B_HEAD_DIM
    cosc = cosc_ref[...]
    sinc = sinc_ref[...]
    scale = B_HEAD_DIM ** -0.5
    q_tiles = []
    for g in range(A_GROUPS):
        cols = slice(g * CHUNK, (g + 1) * CHUNK)
        w = jnp.where(causal, ws_ref[g], 0.0).astype(BF16)
        rhs = jnp.concatenate(
            [vln[c * CHUNK:(c + 1) * CHUNK, cols] for c in range(n_chunks)], axis=1)
        mixed = jnp.dot(w, rhs, preferred_element_type=F32) + bst_ref[:, g:g + 1]
        for c in range(n_chunks):
            rows = slice(c * CHUNK, (c + 1) * CHUNK)
            gate = _silu(ag_ref[rows, cols])
            mix_ref[rows, cols] = (u_ref[rows, cols] * mixed[:, c * CHUNK:(c + 1) * CHUNK]
                                   * gate).astype(BF16)
        q_tiles.append(
            (_rope_tile(q_ref[:, cols], cosc, sinc, take_upper) * scale).astype(BF16))
        if g % 2 == 1:
            out_proj.emit()

    k_all = jnp.concatenate(
        [_rope_tile(kvp_ref[:, :B_KV_WIDTH], cosp_ref[...], sinp_ref[...], take_upper),
         _rope_tile(kvc_ref[:, :B_KV_WIDTH], cosc, sinc, take_upper)], axis=0)
    k_swap = pltpu.roll(k_all, B_HEAD_DIM, 1)
    zero = jnp.zeros_like(k_all)
    k_lo = (jnp.where(low_half, k_all, zero).astype(BF16), jnp.where(low_half, k_swap, zero).astype(BF16))
    k_hi = (jnp.where(low_half, zero, k_swap).astype(BF16), jnp.where(low_half, zero, k_all).astype(BF16))
    v_all = jnp.concatenate([kvp_ref[:, B_KV_WIDTH:], kvc_ref[:, B_KV_WIDTH:]], axis=0)
    vt_all = v_all.T.astype(BF16)

    tiles_per_kv = B_WIDTH // LANES // B_KV_HEADS
    band_keys = 2 * WINDOW
    kj = lax.broadcasted_iota(jnp.int32, (band_keys, WINDOW), 0)
    qi = lax.broadcasted_iota(jnp.int32, (band_keys, WINDOW), 1)
    diff = qi + WINDOW - kj
    band = (diff >= 0) & (diff < WINDOW)
    first_key = jnp.where(seq_tile == 0, WINDOW, 0)
    neg_inf = jnp.full((band_keys, WINDOW), -jnp.inf, F32)
    mask_any = jnp.where(band, 0.0, neg_inf)
    mask_first = jnp.where(band & (kj >= first_key), 0.0, neg_inf)

    for n in range(t_rows // WINDOW):
        rows = slice(n * WINDOW, (n + 1) * WINDOW)
        band_rows = slice(n * WINDOW, (n + 2) * WINDOW)
        mask = jnp.concatenate([mask_first if n == 0 else mask_any] * tiles_per_kv, axis=1)
        for kv in range(B_KV_HEADS):
            out_proj.emit()
            q4 = jnp.concatenate(
                [q_tiles[kv * tiles_per_kv + j][rows] for j in range(tiles_per_kv)], axis=0)
            k_cat = jnp.concatenate([k_lo[kv][band_rows], k_hi[kv][band_rows]], axis=0)
            st = lax.dot_general(k_cat, q4, (((1,), (1,)), ((), ())), preferred_element_type=F32)
            vt = vt_all[kv * B_HEAD_DIM:(kv + 1) * B_HEAD_DIM, band_rows]
            o_halves = []
            for half in range(2):
                sh = st[half * band_keys:(half + 1) * band_keys] + mask
                sink = jnp.concatenate(
                    [jnp.full((1, WINDOW), sinks_ref[layer, 2 * (kv * tiles_per_kv + j) + half], F32)
                     for j in range(tiles_per_kv)], axis=1)
                m = jnp.maximum(jnp.max(sh, axis=0, keepdims=True), sink)
                p = jnp.exp(sh - m)
                l = jnp.sum(p, axis=0, keepdims=True) + jnp.exp(sink - m)
                o_half = jnp.dot(vt, p.astype(BF16), preferred_element_type=F32)
                o_halves.append(o_half * (1.0 / l))
            ot = jnp.concatenate(o_halves, axis=0)
            for j in range(tiles_per_kv):
                jj = kv * tiles_per_kv + j
                gate_ref = bg_refs[jj // 2]
                gate_cols = slice((jj % 2) * LANES, (jj % 2 + 1) * LANES)
                out_cols = slice(A_WIDTH + jj * LANES, A_WIDTH + (jj + 1) * LANES)
                o_tile = ot[:, j * WINDOW:(j + 1) * WINDOW].T
                mix_ref[rows, out_cols] = (o_tile * _silu(gate_ref[rows, gate_cols])).astype(BF16)

    out_proj.finish_ln(g_ref, b_ref, layer, o_ref)
    mix_prev[...] = mix_ref[...]


def _even_layer(h, x, sinks, cos_t, sin_t, a_ln_g, a_ln_b, ws, bst, w_out_bf, ln_g, ln_b, layer, seq):
    m = h.shape[0]
    t = MIX_T
    n_tiles = m // t
    n_s = seq // t
    blocks_per_tile = t // WINDOW
    cur, out = _lagged_tiles(n_tiles)

    def prev(i):
        return jnp.maximum(cur(i) * blocks_per_tile - 1, 0)

    wide = lambda c: pl.BlockSpec((t, A_WIDTH), lambda i: (cur(i), c))
    gate = lambda c: pl.BlockSpec((t, KV_BLOCK), lambda i: (cur(i), GATE_B_COL + c))
    tail_in, out_specs, out_shape, tail_args = _tail_specs(x, w_out_bf, ln_g, ln_b, out)
    return pl.pallas_call(
        partial(_even_layer_kernel, layer=layer, n_tiles=n_tiles, n_s=n_s),
        grid=(n_tiles + 1,),
        in_specs=[
            pl.BlockSpec(memory_space=pltpu.SMEM),
            wide(0), wide(1), wide(2), wide(3),
            pl.BlockSpec((t, KV_BLOCK), lambda i: (cur(i), KV_COL)),
            pl.BlockSpec((WINDOW, KV_BLOCK), lambda i: (prev(i), KV_COL)),
            gate(0), gate(1), gate(2), gate(3),
            pl.BlockSpec((t, LANES), lambda i: (cur(i), 0)),
            pl.BlockSpec((t, LANES), lambda i: (cur(i), 0)),
            pl.BlockSpec((WINDOW, LANES), lambda i: (prev(i), 0)),
            pl.BlockSpec((WINDOW, LANES), lambda i: (prev(i), 0)),
            _whole(a_ln_g),
            _whole(a_ln_b),
            _layer_block((A_GROUPS, CHUNK, CHUNK), layer),
            _layer_block((CHUNK, A_GROUPS), layer),
        ] + tail_in,
        out_specs=out_specs,
        out_shape=out_shape,
        scratch_shapes=[pltpu.VMEM((t, MIX_WIDTH), BF16), pltpu.VMEM((t, MIX_WIDTH), BF16),
                        pltpu.VMEM((t, D_MODEL), F32)],
        compiler_params=_compiler_params("arbitrary"),
        name="even_layer",
    )(sinks, h, h, h, h, h, h, h, h, h, h, cos_t, sin_t, cos_t, sin_t, a_ln_g, a_ln_b, ws, bst,
      *tail_args)


def _odd_layer_kernel(xc_ref, cg_ref, xd_ref, dg_ref, convw_ref, convb_ref, wa_ref, wx_ref, ba_ref,
                      bx_ref, lam_ref, wpool_ref, dscale_ref, x_ref, wout_ref, g_ref, b_ref, o_ref,
                      mix_ref, mix_prev, z_scr, ext_c, ext_d, lvl2, lvl4, hstate, sa, sb, sh,
                      wax_bf, wpool_bf, *, layer, n_tiles, n_s):
    t_rows = xc_ref.shape[0]
    row = slice(layer, layer + 1)
    n_groups = t_rows // SUBLANES
    step = pl.program_id(0)
    s_idx = jnp.minimum(step, n_tiles - 1) % n_s

    @pl.when(step == 0)
    def _():
        mix_prev[...] = jnp.zeros_like(mix_prev)

    @pl.when(s_idx == 0)
    def _():
        ext_c[:, 0:CONV_HALO, :] = jnp.zeros((C_HEADS, CONV_HALO, LANES), F32)
        ext_d[:, 0:POOL_HALO, :] = jnp.zeros((D_WIDTH // LANES, POOL_HALO, LANES), F32)
        hstate[...] = jnp.zeros_like(hstate)
        wax_bf[:, :, :C_HEAD_DIM] = wa_ref[...].astype(BF16)
        wax_bf[:, :, C_HEAD_DIM:] = wx_ref[...].astype(BF16)
        wpool_bf[...] = wpool_ref[...].astype(BF16)

    @pl.when(s_idx > 0)
    def _():
        ext_c[:, 0:CONV_HALO, :] = ext_c[:, t_rows:t_rows + CONV_HALO, :]
        ext_d[:, 0:POOL_HALO, :] = ext_d[:, t_rows:t_rows + POOL_HALO, :]

    out_proj = _OutProj(mix_prev, wout_ref, x_ref, z_scr)

    z = -lam_ref[row, :]
    softplus = jnp.maximum(z, 0.0) + jnp.log1p(jnp.exp(-jnp.abs(z)))
    for hd in range(C_HEADS):
        cols = slice(hd * C_HEAD_DIM, (hd + 1) * C_HEAD_DIM)
        ext_c[hd, CONV_HALO:, :] = xc_ref[:, cols]
        xconv = convb_ref[row, cols]
        for j in range(CONV_WIDTH):
            lo = CONV_HALO - (CONV_WIDTH - 1) + j
            xconv = xconv + ext_c[hd, lo:lo + t_rows, :] * convw_ref[j:j + 1, cols]
        ri = jnp.dot(xconv.astype(BF16), wax_bf[hd], preferred_element_type=F32)
        r = _sigmoid(ri[:, :C_HEAD_DIM] + ba_ref[row, cols])
        i = _sigmoid(ri[:, C_HEAD_DIM:] + bx_ref[row, cols])
        log_a = -LRU_C * r * softplus[:, cols]
        a = jnp.exp(log_a)
        m2 = -jnp.tanh(log_a) * (a * a + 1.0)
        mult = jnp.where(m2 == 0.0, 0.0, m2 * lax.rsqrt(m2))
        b = mult * (i * xconv)
        for g in range(n_groups):
            base = (g * C_HEADS + hd) * SUBLANES
            sa[base:base + SUBLANES, :] = a[g * SUBLANES:(g + 1) * SUBLANES]
            sb[base:base + SUBLANES, :] = b[g * SUBLANES:(g + 1) * SUBLANES]
        if hd in (1, 3, 5, 6, 7):
            out_proj.emit()

    out_proj.emit()
    h = hstate[...]
    for t in range(t_rows):
        g, s = divmod(t, SUBLANES)
        at_t = pl.ds(g * C_HEADS * SUBLANES + s, C_HEADS, stride=SUBLANES)
        h = sa[at_t, :] * h + sb[at_t, :]
        sh[at_t, :] = h
    hstate[...] = h
    out_proj.emit()
    for hd in range(C_HEADS):
        cols = slice(hd * C_HEAD_DIM, (hd + 1) * C_HEAD_DIM)
        hcol = jnp.concatenate(
            [sh[(g * C_HEADS + hd) * SUBLANES:(g * C_HEADS + hd + 1) * SUBLANES, :]
             for g in range(n_groups)], axis=0)
        mix_ref[:, cols] = (hcol * _silu(cg_ref[:, cols])).astype(BF16)

    n_ext = POOL_HALO + t_rows
    tiles_per_group = D_GROUP_DIM // LANES
    pos1 = s_idx * t_rows + lax.broadcasted_iota(jnp.int32, (t_rows, LANES), 0) + 1
    for g, w in enumerate(POOL_WINDOWS):
        if g == 2:
            out_proj.emit()
        inv = 1.0 / jnp.minimum(pos1, w).astype(F32)
        pooled = []
        for k in range(tiles_per_group):
            tile = g * tiles_per_group + k
            cols = slice(tile * LANES, (tile + 1) * LANES)
            ext_d[tile, POOL_HALO:, :] = xd_ref[:, cols]
            lo = SUBLANES
            win = ext_d[tile, lo:n_ext, :] + ext_d[tile, lo - 1:n_ext - 1, :]
            if w >= 4:
                lvl2[k, lo:n_ext, :] = win
                lo += SUBLANES
                win = lvl2[k, lo:n_ext, :] + lvl2[k, lo - 2:n_ext - 2, :]
            if w >= 8:
                lvl4[k, lo:n_ext, :] = win
                lo += SUBLANES
                win = lvl4[k, lo:n_ext, :] + lvl4[k, lo - 4:n_ext - 4, :]
            if w >= 16:
                lo += SUBLANES
                win = win[SUBLANES:] + win[:-SUBLANES]
            pooled.append((win[POOL_HALO - lo:] * inv - xd_ref[:, cols]).astype(BF16))
        cols = slice(g * D_GROUP_DIM, (g + 1) * D_GROUP_DIM)
        mixed = jnp.dot(jnp.concatenate(pooled, axis=1), wpool_bf[g], preferred_element_type=F32)
        out_cols = slice(C_WIDTH + g * D_GROUP_DIM, C_WIDTH + (g + 1) * D_GROUP_DIM)
        mix_ref[:, out_cols] = (mixed * dscale_ref[row, cols] * _silu(dg_ref[:, cols])).astype(BF16)

    out_proj.finish_ln(g_ref, b_ref, layer, o_ref)
    mix_prev[...] = mix_ref[...]


def _odd_layer(h, x, conv_w, conv_b, w_a, w_x, b_a, b_x, lam, w_pool, d_scale, w_out_bf, ln_g, ln_b,
               layer, seq):
    m = h.shape[0]
    t = MIX_T
    n_tiles = m // t
    n_s = seq // t
    cur, out = _lagged_tiles(n_tiles)
    wide = lambda c: pl.BlockSpec((t, C_WIDTH), lambda i: (cur(i), c))
    n_pool = len(POOL_WINDOWS)
    tail_in, out_specs, out_shape, tail_args = _tail_specs(x, w_out_bf, ln_g, ln_b, out)
    return pl.pallas_call(
        partial(_odd_layer_kernel, layer=layer, n_tiles=n_tiles, n_s=n_s),
        grid=(n_tiles + 1,),
        in_specs=[
            wide(0), wide(1), wide(2), wide(3),
            _layer_block((CONV_WIDTH, C_WIDTH), layer),
            _whole(conv_b),
            _layer_block((C_HEADS, C_HEAD_DIM, C_HEAD_DIM), layer),
            _layer_block((C_HEADS, C_HEAD_DIM, C_HEAD_DIM), layer),
            _whole(b_a), _whole(b_x), _whole(lam),
            _layer_block((n_pool, D_GROUP_DIM, D_GROUP_DIM), layer),
            _whole(d_scale),
        ] + tail_in,
        out_specs=out_specs,
        out_shape=out_shape,
        scratch_shapes=[
            pltpu.VMEM((t, MIX_WIDTH), BF16),
            pltpu.VMEM((t, MIX_WIDTH), BF16),
            pltpu.VMEM((t, D_MODEL), F32),
            pltpu.VMEM((C_WIDTH // LANES, CONV_HALO + t, LANES), F32),
            pltpu.VMEM((D_WIDTH // LANES, POOL_HALO + t, LANES), F32),
            pltpu.VMEM((D_GROUP_DIM // LANES, POOL_HALO + t, LANES), F32),
            pltpu.VMEM((D_GROUP_DIM // LANES, POOL_HALO + t, LANES), F32),
            pltpu.VMEM((SUBLANES, LANES), F32),
            pltpu.VMEM((t * C_HEADS, LANES), F32),
            pltpu.VMEM((t * C_HEADS, LANES), F32),
            pltpu.VMEM((t * C_HEADS, LANES), F32),
            pltpu.VMEM((C_HEADS, C_HEAD_DIM, 2 * C_HEAD_DIM), BF16),
            pltpu.VMEM((n_pool, D_GROUP_DIM, D_GROUP_DIM), BF16),
        ],
        compiler_params=_compiler_params("arbitrary"),
        name="odd_layer",
    )(h, h, h, h, conv_w, conv_b, w_a, w_x, b_a, b_x, lam, w_pool, d_scale, *tail_args)


def _rope_tables(positions):
    half = ROT_DIM // 2
    inv_freq = ROPE_THETA ** (-jnp.arange(0, ROT_DIM, 2, dtype=F32) / ROT_DIM)
    rest = jnp.zeros((B_HEAD_DIM - ROT_DIM,), F32)
    freq_head = jnp.concatenate([inv_freq, inv_freq, rest])
    sign_head = jnp.concatenate([-jnp.ones((half,), F32), jnp.ones((half,), F32), rest])
    reps = LANES // B_HEAD_DIM
    ang = positions.astype(F32).reshape(-1, 1) * jnp.tile(freq_head, reps)[None, :]
    return jnp.cos(ang), jnp.sin(ang) * jnp.tile(sign_head, reps)[None, :]


def kernel(x, positions, even_w_in, even_a_ln_g, even_a_ln_b, even_a_ws, even_a_bs, even_b_sinks, even_w_out, even_ln_g, even_ln_b, odd_w_in, odd_conv_w, odd_conv_b, odd_w_a, odd_b_a, odd_w_x, odd_b_x, odd_lam, odd_w_pool, odd_d_scale, odd_w_out, odd_ln_g, odd_ln_b):
    batch, seq, d = x.shape
    m = batch * seq
    xf = x.reshape(m, d)
    cos_t, sin_t = _rope_tables(positions)
    even_bst = jnp.swapaxes(even_a_bs, 1, 2)
    w_in = even_w_in[0].astype(BF16)
    for layer in range(DEPTH):
        j = layer // 2
        if layer % 2 == 0:
            h, w_out, w_in = _proj_in(xf, w_in, even_w_out, j, odd_w_in, j)
            xf = _even_layer(h, xf, even_b_sinks, cos_t, sin_t, even_a_ln_g, even_a_ln_b, even_a_ws,
                             even_bst, w_out, even_ln_g, even_ln_b, j, seq)
        else:
            w_next = even_w_in if layer + 1 < DEPTH else None
            h, w_out, w_in = _proj_in(xf, w_in, odd_w_out, j, w_next, j + 1)
            xf = _odd_layer(h, xf, odd_conv_w, odd_conv_b, odd_w_a, odd_w_x, odd_b_a, odd_b_x, odd_lam,
                            odd_w_pool, odd_d_scale, w_out, odd_ln_g, odd_ln_b, j, seq)
    return xf.reshape(batch, seq, d)
```

```python
from functools import partial

import jax
import jax.numpy as jnp
from jax import lax
from jax.experimental import pallas as pl
from jax.experimental.pallas import tpu as pltpu

D_MODEL = 2048
DEPTH = 4
A_WIDTH = 1024
A_GROUPS = 8
CHUNK = 128
B_HEAD_DIM = 64
B_Q_HEADS = 16
B_KV_HEADS = 2
B_WIDTH = 1024
B_KV_WIDTH = 128
WINDOW = 128
ROT_DIM = 16
ROPE_THETA = 500000.0
C_WIDTH = 1024
C_HEADS = 8
C_HEAD_DIM = 128
CONV_WIDTH = 4
LRU_C = 8.0
D_WIDTH = 1024
POOL_WINDOWS = (2, 4, 8, 16)
D_GROUP_DIM = 256
MIX_WIDTH = 2048
DN_ALPHA = (2 * DEPTH) ** 0.25
LN_EPS = 1e-5

LANES = 128
SUBLANES = 8
VMEM_LIMIT_BYTES = 56 * 1024 * 1024

PROJ_TM = 256
MIX_T = 256
OUT_BLOCK = 256
OUT_BLOCKS = D_MODEL // OUT_BLOCK
CONV_HALO = SUBLANES
POOL_HALO = 4 * SUBLANES

KV_BLOCK = 2 * B_KV_WIDTH
KV_COL = (3 * A_WIDTH + B_WIDTH) // KV_BLOCK
GATE_B_COL = KV_COL + 1

F32 = jnp.float32
BF16 = jnp.bfloat16
H_DTYPE = BF16


def _sigmoid(x):
    return 0.5 * jnp.tanh(0.5 * x) + 0.5


def _silu(x):
    hx = 0.5 * x
    return hx * jnp.tanh(hx) + hx


def _resident(shape):
    return pl.BlockSpec(shape, lambda *_: (0,) * len(shape), pipeline_mode=pl.Buffered(1))


def _whole(arr):
    return pl.BlockSpec(arr.shape, lambda *_: (0,) * arr.ndim)


def _layer_block(shape, layer):
    return pl.BlockSpec((None,) + shape, lambda *_: (layer,) + (0,) * len(shape))


def _compiler_params(*semantics):
    return pltpu.CompilerParams(dimension_semantics=semantics, vmem_limit_bytes=VMEM_LIMIT_BYTES)


def _proj_in_kernel(*refs, cast_next):
    if cast_next:
        x_ref, w_ref, wout_ref, wn_ref, o_ref, wout_bf_ref, wn_bf_ref = refs
        wn_bf_ref[...] = wn_ref[...].astype(BF16)
    else:
        x_ref, w_ref, wout_ref, o_ref, wout_bf_ref = refs
    wout_bf_ref[...] = wout_ref[...].astype(BF16)
    o_ref[...] = jnp.dot(x_ref[...].astype(BF16), w_ref[...],
                         preferred_element_type=F32).astype(o_ref.dtype)


def _proj_in(x, w, w_out, layer, w_next, layer_next):
    m, k = x.shape
    n = w.shape[1]
    steps = m // PROJ_TM

    def slab_specs(stacked, which):
        rows, cols = stacked.shape[1:]
        slab = rows // steps
        return (pl.BlockSpec((None, slab, cols), lambda i: (which, i, 0)),
                pl.BlockSpec((slab, cols), lambda i: (i, 0)),
                jax.ShapeDtypeStruct((rows, cols), BF16))

    casts = [slab_specs(w_out, layer)]
    args = [x, w, w_out]
    if w_next is not None:
        casts.append(slab_specs(w_next, layer_next))
        args.append(w_next)
    outs = pl.pallas_call(
        partial(_proj_in_kernel, cast_next=w_next is not None),
        grid=(steps,),
        in_specs=[pl.BlockSpec((PROJ_TM, k), lambda i: (i, 0)), _resident((k, n))]
        + [c[0] for c in casts],
        out_specs=[pl.BlockSpec((PROJ_TM, n), lambda i: (i, 0))] + [c[1] for c in casts],
        out_shape=[jax.ShapeDtypeStruct((m, n), H_DTYPE)] + [c[2] for c in casts],
        compiler_params=_compiler_params("parallel"),
        name="proj_in",
    )(*args)
    return outs if w_next is not None else (*outs, None)


class _OutProj:
    def __init__(self, mix_ref, wout_ref, x_ref, z_ref):
        self.refs = (mix_ref, wout_ref, x_ref, z_ref)
        self.done = 0

    def emit(self, count=1):
        mix_ref, wout_ref, x_ref, z_ref = self.refs
        for _ in range(min(count, OUT_BLOCKS - self.done)):
            cols = slice(self.done * OUT_BLOCK, (self.done + 1) * OUT_BLOCK)
            z_ref[:, cols] = DN_ALPHA * x_ref[:, cols] + jnp.dot(
                mix_ref[...], wout_ref[:, cols], preferred_element_type=F32)
            self.done += 1

    def finish_ln(self, g_ref, b_ref, layer, o_ref):
        self.emit(OUT_BLOCKS)
        z = self.refs[3][...]
        mu = jnp.mean(z, axis=-1, keepdims=True)
        zc = z - mu
        var = jnp.mean(zc * zc, axis=-1, keepdims=True)
        o_ref[...] = (zc * lax.rsqrt(var + LN_EPS) * g_ref[layer:layer + 1, :]
                      + b_ref[layer:layer + 1, :])


def _lagged_tiles(n_tiles):
    def cur(i):
        return jnp.minimum(i, n_tiles - 1)

    def out(i):
        return jnp.maximum(i - 1, 0)

    return cur, out


def _tail_specs(x, w_out_bf, ln_g, ln_b, out):
    m, d = x.shape
    in_specs = [
        pl.BlockSpec((MIX_T, d), lambda i: (out(i), 0)),
        _resident(w_out_bf.shape),
        _whole(ln_g),
        _whole(ln_b),
    ]
    out_spec = pl.BlockSpec((MIX_T, d), lambda i: (out(i), 0))
    out_shape = jax.ShapeDtypeStruct((m, d), F32)
    return in_specs, out_spec, out_shape, [x, w_out_bf, ln_g, ln_b]


def _rope_tile(t, cos, sin, take_upper):
    upper = pltpu.roll(t, LANES - ROT_DIM // 2, 1)
    lower = pltpu.roll(t, ROT_DIM // 2, 1)
    return t * cos + jnp.where(take_upper, upper, lower) * sin


def _even_layer_kernel(sinks_ref, u_ref, v_ref, ag_ref, q_ref, kvc_ref, kvp_ref, bg0_ref, bg1_ref,
                       bg2_ref, bg3_ref, cosc_ref, sinc_ref, cosp_ref, sinp_ref, lng_ref, lnb_ref,
                       ws_ref, bst_ref, x_ref, wout_ref, g_ref, b_ref, o_ref,
                       mix_ref, mix_prev, z_scr, *, layer, n_tiles, n_s):
    t_rows = u_ref.shape[0]
    n_chunks = t_rows // CHUNK
    bg_refs = (bg0_ref, bg1_ref, bg2_ref, bg3_ref)
    step = pl.program_id(0)
    seq_tile = jnp.minimum(step, n_tiles - 1) % n_s

    @pl.when(step == 0)
    def _():
        mix_prev[...] = jnp.zeros_like(mix_prev)

    out_proj = _OutProj(mix_prev, wout_ref, x_ref, z_scr)

    v = v_ref[...].astype(F32)
    mu = jnp.mean(v, axis=-1, keepdims=True)
    vc = v - mu
    var = jnp.mean(vc * vc, axis=-1, keepdims=True)
    vln = (vc * lax.rsqrt(var + LN_EPS) * lng_ref[layer:layer + 1, :]
           + lnb_ref[layer:layer + 1, :]).astype(BF16)
    causal = (lax.broadcasted_iota(jnp.int32, (CHUNK, CHUNK), 0)
              >= lax.broadcasted_iota(jnp.int32, (CHUNK, CHUNK), 1))
    lane = lax.broadcasted_iota(jnp.int32, (1, LANES), 1)
    take_upper = (lane % B_HEAD_DIM) < (ROT_DIM // 2)
    low_half = lane < B_HEAD_DIM
    cosc = cosc_ref[...]
    sinc = sinc_ref[...]
    scale = B_HEAD_DIM ** -0.5
    q_tiles = []
    for g in range(A_GROUPS):
        cols = slice(g * CHUNK, (g + 1) * CHUNK)
        w = jnp.where(causal, ws_ref[g], 0.0).astype(BF16)
        rhs = jnp.concatenate(
            [vln[c * CHUNK:(c + 1) * CHUNK, cols] for c in range(n_chunks)], axis=1)
        mixed = jnp.dot(w, rhs, preferred_element_type=F32) + bst_ref[:, g:g + 1]
        for c in range(n_chunks):
            rows = slice(c * CHUNK, (c + 1) * CHUNK)
            gate = _silu(ag_ref[rows, cols].astype(F32))
            mix_ref[rows, cols] = (u_ref[rows, cols].astype(F32) * mixed[:, c * CHUNK:(c + 1) * CHUNK]
                                   * gate).astype(BF16)
        q_tiles.append(
            (_rope_tile(q_ref[:, cols].astype(F32), cosc, sinc, take_upper) * scale).astype(BF16))
        if g % 2 == 1:
            out_proj.emit()

    k_all = jnp.concatenate(
        [_rope_tile(kvp_ref[:, :B_KV_WIDTH].astype(F32), cosp_ref[...], sinp_ref[...], take_upper),
         _rope_tile(kvc_ref[:, :B_KV_WIDTH].astype(F32), cosc, sinc, take_upper)], axis=0)
    k_swap = pltpu.roll(k_all, B_HEAD_DIM, 1)
    zero = jnp.zeros_like(k_all)
    k_lo = (jnp.where(low_half, k_all, zero).astype(BF16), jnp.where(low_half, k_swap, zero).astype(BF16))
    k_hi = (jnp.where(low_half, zero, k_swap).astype(BF16), jnp.where(low_half, zero, k_all).astype(BF16))
    v_all = jnp.concatenate([kvp_ref[:, B_KV_WIDTH:], kvc_ref[:, B_KV_WIDTH:]], axis=0)
    vt_all = v_all.astype(F32).T.astype(BF16)

    tiles_per_kv = B_WIDTH // LANES // B_KV_HEADS
    band_keys = 2 * WINDOW
    kj = lax.broadcasted_iota(jnp.int32, (band_keys, WINDOW), 0)
    qi = lax.broadcasted_iota(jnp.int32, (band_keys, WINDOW), 1)
    diff = qi + WINDOW - kj
    band = (diff >= 0) & (diff < WINDOW)
    first_key = jnp.where(seq_tile == 0, WINDOW, 0)
    neg_inf = jnp.full((band_keys, WINDOW), -jnp.inf, F32)
    mask_any = jnp.where(band, 0.0, neg_inf)
    mask_first = jnp.where(band & (kj >= first_key), 0.0, neg_inf)

    for n in range(t_rows // WINDOW):
        rows = slice(n * WINDOW, (n + 1) * WINDOW)
        band_rows = slice(n * WINDOW, (n + 2) * WINDOW)
        mask = jnp.concatenate([mask_first if n == 0 else mask_any] * tiles_per_kv, axis=1)
        for kv in range(B_KV_HEADS):
            out_proj.emit()
            q4 = jnp.concatenate(
                [q_tiles[kv * tiles_per_kv + j][rows] for j in range(tiles_per_kv)], axis=0)
            k_cat = jnp.concatenate([k_lo[kv][band_rows], k_hi[kv][band_rows]], axis=0)
            st = lax.dot_general(k_cat, q4, (((1,), (1,)), ((), ())), preferred_element_type=F32)
            vt = vt_all[kv * B_HEAD_DIM:(kv + 1) * B_HEAD_DIM, band_rows]
            o_halves = []
            for half in range(2):
                sh = st[half * band_keys:(half + 1) * band_keys] + mask
                sink = jnp.concatenate(
                    [jnp.full((1, WINDOW), sinks_ref[layer, 2 * (kv * tiles_per_kv + j) + half], F32)
                     for j in range(tiles_per_kv)], axis=1)
                m = jnp.maximum(jnp.max(sh, axis=0, keepdims=True), sink)
                p = jnp.exp(sh - m)
                l = jnp.sum(p, axis=0, keepdims=True) + jnp.exp(sink - m)
                o_half = jnp.dot(vt, p.astype(BF16), preferred_element_type=F32)
                o_halves.append(o_half * (1.0 / l))
            ot = jnp.concatenate(o_halves, axis=0)
            for j in range(tiles_per_kv):
                jj = kv * tiles_per_kv + j
                gate_ref = bg_refs[jj // 2]
                gate_cols = slice((jj % 2) * LANES, (jj % 2 + 1) * LANES)
                out_cols = slice(A_WIDTH + jj * LANES, A_WIDTH + (jj + 1) * LANES)
                o_tile = ot[:, j * WINDOW:(j + 1) * WINDOW].T
                mix_ref[rows, out_cols] = (o_tile * _silu(gate_ref[rows, gate_cols].astype(F32))).astype(BF16)

    out_proj.finish_ln(g_ref, b_ref, layer, o_ref)
    mix_prev[...] = mix_ref[...]


def _even_layer(h, x, sinks, cos_t, sin_t, a_ln_g, a_ln_b, ws, bst, w_out_bf, ln_g, ln_b, layer, seq):
    m = h.shape[0]
    t = MIX_T
    n_tiles = m // t
    n_s = seq // t
    blocks_per_tile = t // WINDOW
    cur, out = _lagged_tiles(n_tiles)

    def prev(i):
        return jnp.maximum(cur(i) * blocks_per_tile - 1, 0)

    wide = lambda c: pl.BlockSpec((t, A_WIDTH), lambda i: (cur(i), c))
    gate = lambda c: pl.BlockSpec((t, KV_BLOCK), lambda i: (cur(i), GATE_B_COL + c))
    tail_in, out_specs, out_shape, tail_args = _tail_specs(x, w_out_bf, ln_g, ln_b, out)
    return pl.pallas_call(
        partial(_even_layer_kernel, layer=layer, n_tiles=n_tiles, n_s=n_s),
        grid=(n_tiles + 1,),
        in_specs=[
            pl.BlockSpec(memory_space=pltpu.SMEM),
            wide(0), wide(1), wide(2), wide(3),
            pl.BlockSpec((t, KV_BLOCK), lambda i: (cur(i), KV_COL)),
            pl.BlockSpec((WINDOW, KV_BLOCK), lambda i: (prev(i), KV_COL)),
            gate(0), gate(1), gate(2), gate(3),
            pl.BlockSpec((t, LANES), lambda i: (cur(i), 0)),
            pl.BlockSpec((t, LANES), lambda i: (cur(i), 0)),
            pl.BlockSpec((WINDOW, LANES), lambda i: (prev(i), 0)),
            pl.BlockSpec((WINDOW, LANES), lambda i: (prev(i), 0)),
            _whole(a_ln_g),
            _whole(a_ln_b),
            _layer_block((A_GROUPS, CHUNK, CHUNK), layer),
            _layer_block((CHUNK, A_GROUPS), layer),
        ] + tail_in,
        out_specs=out_specs,
        out_shape=out_shape,
        scratch_shapes=[pltpu.VMEM((t, MIX_WIDTH), BF16), pltpu.VMEM((t, MIX_WIDTH), BF16),
                        pltpu.VMEM((t, D_MODEL), F32)],
        compiler_params=_compiler_params("arbitrary"),
        name="even_layer",
    )(sinks, h, h, h, h, h, h, h, h, h, h, cos_t, sin_t, cos_t, sin_t, a_ln_g, a_ln_b, ws, bst,
      *tail_args)


def _odd_layer_kernel(xc_ref, cg_ref, xd_ref, dg_ref, convw_ref, convb_ref, wa_ref, wx_ref, ba_ref,
                      bx_ref, lam_ref, wpool_ref, dscale_ref, x_ref, wout_ref, g_ref, b_ref, o_ref,
                      mix_ref, mix_prev, z_scr, ext_c, ext_d, lvl2, lvl4, hstate, sa, sb, sh,
                      wax_bf, wpool_bf, *, layer, n_tiles, n_s):
    t_rows = xc_ref.shape[0]
    row = slice(layer, layer + 1)
    n_groups = t_rows // SUBLANES
    step = pl.program_id(0)
    s_idx = jnp.minimum(step, n_tiles - 1) % n_s

    @pl.when(step == 0)
    def _():
        mix_prev[...] = jnp.zeros_like(mix_prev)

    @pl.when(s_idx == 0)
    def _():
        ext_c[:, 0:CONV_HALO, :] = jnp.zeros((C_HEADS, CONV_HALO, LANES), F32)
        ext_d[:, 0:POOL_HALO, :] = jnp.zeros((D_WIDTH // LANES, POOL_HALO, LANES), F32)
        hstate[...] = jnp.zeros_like(hstate)
        wax_bf[:, :, :C_HEAD_DIM] = wa_ref[...].astype(BF16)
        wax_bf[:, :, C_HEAD_DIM:] = wx_ref[...].astype(BF16)
        wpool_bf[...] = wpool_ref[...].astype(BF16)

    @pl.when(s_idx > 0)
    def _():
        ext_c[:, 0:CONV_HALO, :] = ext_c[:, t_rows:t_rows + CONV_HALO, :]
        ext_d[:, 0:POOL_HALO, :] = ext_d[:, t_rows:t_rows + POOL_HALO, :]

    out_proj = _OutProj(mix_prev, wout_ref, x_ref, z_scr)

    z = -lam_ref[row, :]
    softplus = jnp.maximum(z, 0.0) + jnp.log1p(jnp.exp(-jnp.abs(z)))
    for hd in range(C_HEADS):
        cols = slice(hd * C_HEAD_DIM, (hd + 1) * C_HEAD_DIM)
        ext_c[hd, CONV_HALO:, :] = xc_ref[:, cols].astype(F32)
        xconv = convb_ref[row, cols]
        for j in range(CONV_WIDTH):
            lo = CONV_HALO - (CONV_WIDTH - 1) + j
            xconv = xconv + ext_c[hd, lo:lo + t_rows, :] * convw_ref[j:j + 1, cols]
        ri = jnp.dot(xconv.astype(BF16), wax_bf[hd], preferred_element_type=F32)
        r = _sigmoid(ri[:, :C_HEAD_DIM] + ba_ref[row, cols])
        i = _sigmoid(ri[:, C_HEAD_DIM:] + bx_ref[row, cols])
        log_a = -LRU_C * r * softplus[:, cols]
        a = jnp.exp(log_a)
        m2 = -jnp.tanh(log_a) * (a * a + 1.0)
        mult = jnp.where(m2 == 0.0, 0.0, m2 * lax.rsqrt(m2))
        b = mult * (i * xconv)
        for g in range(n_groups):
            base = (g * C_HEADS + hd) * SUBLANES
            sa[base:base + SUBLANES, :] = a[g * SUBLANES:(g + 1) * SUBLANES]
            sb[base:base + SUBLANES, :] = b[g * SUBLANES:(g + 1) * SUBLANES]
        if hd in (1, 3, 5, 6, 7):
            out_proj.emit()

    out_proj.emit()
    h = hstate[...]
    for t in range(t_rows):
        g, s = divmod(t, SUBLANES)
        at_t = pl.ds(g * C_HEADS * SUBLANES + s, C_HEADS, stride=SUBLANES)
        h = sa[at_t, :] * h + sb[at_t, :]
        sh[at_t, :] = h
    hstate[...] = h
    out_proj.emit()
    for hd in range(C_HEADS):
        cols = slice(hd * C_HEAD_DIM, (hd + 1) * C_HEAD_DIM)
        hcol = jnp.concatenate(
            [sh[(g * C_HEADS + hd) * SUBLANES:(g * C_HEADS + hd + 1) * SUBLANES, :]
             for g in range(n_groups)], axis=0)
        mix_ref[:, cols] = (hcol * _silu(cg_ref[:, cols].astype(F32))).astype(BF16)

    n_ext = POOL_HALO + t_rows
    tiles_per_group = D_GROUP_DIM // LANES
    pos1 = s_idx * t_rows + lax.broadcasted_iota(jnp.int32, (t_rows, LANES), 0) + 1
    for g, w in enumerate(POOL_WINDOWS):
        if g == 2:
            out_proj.emit()
        inv = 1.0 / jnp.minimum(pos1, w).astype(F32)
        pooled = []
        for k in range(tiles_per_group):
            tile = g * tiles_per_group + k
            cols = slice(tile * LANES, (tile + 1) * LANES)
            xd = xd_ref[:, cols].astype(F32)
            ext_d[tile, POOL_HALO:, :] = xd
            lo = SUBLANES
            win = ext_d[tile, lo:n_ext, :] + ext_d[tile, lo - 1:n_ext - 1, :]
            if w >= 4:
                lvl2[k, lo:n_ext, :] = win
                lo += SUBLANES
                win = lvl2[k, lo:n_ext, :] + lvl2[k, lo - 2:n_ext - 2, :]
            if w >= 8:
                lvl4[k, lo:n_ext, :] = win
                lo += SUBLANES
                win = lvl4[k, lo:n_ext, :] + lvl4[k, lo - 4:n_ext - 4, :]
            if w >= 16:
                lo += SUBLANES
                win = win[SUBLANES:] + win[:-SUBLANES]
            pooled.append((win[POOL_HALO - lo:] * inv - xd).astype(BF16))
        cols = slice(g * D_GROUP_DIM, (g + 1) * D_GROUP_DIM)
        mixed = jnp.dot(jnp.concatenate(pooled, axis=1), wpool_bf[g], preferred_element_type=F32)
        out_cols = slice(C_WIDTH + g * D_GROUP_DIM, C_WIDTH + (g + 1) * D_GROUP_DIM)
        mix_ref[:, out_cols] = (mixed * dscale_ref[row, cols] * _silu(dg_ref[:, cols].astype(F32))).astype(BF16)

    out_proj.finish_ln(g_ref, b_ref, layer, o_ref)
    mix_prev[...] = mix_ref[...]


def _odd_layer(h, x, conv_w, conv_b, w_a, w_x, b_a, b_x, lam, w_pool, d_scale, w_out_bf, ln_g, ln_b,
               layer, seq):
    m = h.shape[0]
    t = MIX_T
    n_tiles = m // t
    n_s = seq // t
    cur, out = _lagged_tiles(n_tiles)
    wide = lambda c: pl.BlockSpec((t, C_WIDTH), lambda i: (cur(i), c))
    n_pool = len(POOL_WINDOWS)
    tail_in, out_specs, out_shape, tail_args = _tail_specs(x, w_out_bf, ln_g, ln_b, out)
    return pl.pallas_call(
        partial(_odd_layer_kernel, layer=layer, n_tiles=n_tiles, n_s=n_s),
        grid=(n_tiles + 1,),
        in_specs=[
            wide(0), wide(1), wide(2), wide(3),
            _layer_block((CONV_WIDTH, C_WIDTH), layer),
            _whole(conv_b),
            _layer_block((C_HEADS, C_HEAD_DIM, C_HEAD_DIM), layer),
            _layer_block((C_HEADS, C_HEAD_DIM, C_HEAD_DIM), layer),
            _whole(b_a), _whole(b_x), _whole(lam),
            _layer_block((n_pool, D_GROUP_DIM, D_GROUP_DIM), layer),
            _whole(d_scale),
        ] + tail_in,
        out_specs=out_specs,
        out_shape=out_shape,
        scratch_shapes=[
            pltpu.VMEM((t, MIX_WIDTH), BF16),
            pltpu.VMEM((t, MIX_WIDTH), BF16),
            pltpu.VMEM((t, D_MODEL), F32),
            pltpu.VMEM((C_WIDTH // LANES, CONV_HALO + t, LANES), F32),
            pltpu.VMEM((D_WIDTH // LANES, POOL_HALO + t, LANES), F32),
            pltpu.VMEM((D_GROUP_DIM // LANES, POOL_HALO + t, LANES), F32),
            pltpu.VMEM((D_GROUP_DIM // LANES, POOL_HALO + t, LANES), F32),
            pltpu.VMEM((SUBLANES, LANES), F32),
            pltpu.VMEM((t * C_HEADS, LANES), F32),
            pltpu.VMEM((t * C_HEADS, LANES), F32),
            pltpu.VMEM((t * C_HEADS, LANES), F32),
            pltpu.VMEM((C_HEADS, C_HEAD_DIM, 2 * C_HEAD_DIM), BF16),
            pltpu.VMEM((n_pool, D_GROUP_DIM, D_GROUP_DIM), BF16),
        ],
        compiler_params=_compiler_params("arbitrary"),
        name="odd_layer",
    )(h, h, h, h, conv_w, conv_b, w_a, w_x, b_a, b_x, lam, w_pool, d_scale, *tail_args)


def _rope_tables(positions):
    half = ROT_DIM // 2
    inv_freq = ROPE_THETA ** (-jnp.arange(0, ROT_DIM, 2, dtype=F32) / ROT_DIM)
    rest = jnp.zeros((B_HEAD_DIM - ROT_DIM,), F32)
    freq_head = jnp.concatenate([inv_freq, inv_freq, rest])
    sign_head = jnp.concatenate([-jnp.ones((half,), F32), jnp.ones((half,), F32), rest])
    reps = LANES // B_HEAD_DIM
    ang = positions.astype(F32).reshape(-1, 1) * jnp.tile(freq_head, reps)[None, :]
    return jnp.cos(ang), jnp.sin(ang) * jnp.tile(sign_head, reps)[None, :]


def kernel(x, positions, even_w_in, even_a_ln_g, even_a_ln_b, even_a_ws, even_a_bs, even_b_sinks, even_w_out, even_ln_g, even_ln_b, odd_w_in, odd_conv_w, odd_conv_b, odd_w_a, odd_b_a, odd_w_x, odd_b_x, odd_lam, odd_w_pool, odd_d_scale, odd_w_out, odd_ln_g, odd_ln_b):
    batch, seq, d = x.shape
    m = batch * seq
    xf = x.reshape(m, d)
    cos_t, sin_t = _rope_tables(positions)
    even_bst = jnp.swapaxes(even_a_bs, 1, 2)
    w_in = even_w_in[0].astype(BF16)
    for layer in range(DEPTH):
        j = layer // 2
        if layer % 2 == 0:
            h, w_out, w_in = _proj_in(xf, w_in, even_w_out, j, odd_w_in, j)
            xf = _even_layer(h, xf, even_b_sinks, cos_t, sin_t, even_a_ln_g, even_a_ln_b, even_a_ws,
                             even_bst, w_out, even_ln_g, even_ln_b, j, seq)
        else:
            w_next = even_w_in if layer + 1 < DEPTH else None
            h, w_out, w_in = _proj_in(xf, w_in, odd_w_out, j, w_next, j + 1)
            xf = _odd_layer(h, xf, odd_conv_w, odd_conv_b, odd_w_a, odd_w_x, odd_b_a, odd_b_x, odd_lam,
                            odd_w_pool, odd_d_scale, w_out, odd_ln_g, odd_ln_b, j, seq)
    return xf.reshape(batch, seq, d)
```

```python
from functools import partial

import jax
import jax.numpy as jnp
from jax import lax
from jax.experimental import pallas as pl
from jax.experimental.pallas import tpu as pltpu

D_MODEL = 2048
DEPTH = 4
A_WIDTH = 1024
A_GROUPS = 8
CHUNK = 128
B_HEAD_DIM = 64
B_Q_HEADS = 16
B_KV_HEADS = 2
B_WIDTH = 1024
B_KV_WIDTH = 128
WINDOW = 128
ROT_DIM = 16
ROPE_THETA = 500000.0
C_WIDTH = 1024
C_HEADS = 8
C_HEAD_DIM = 128
CONV_WIDTH = 4
LRU_C = 8.0
D_WIDTH = 1024
POOL_WINDOWS = (2, 4, 8, 16)
D_GROUP_DIM = 256
MIX_WIDTH = 2048
DN_ALPHA = (2 * DEPTH) ** 0.25
LN_EPS = 1e-5

LANES = 128
SUBLANES = 8
VMEM_LIMIT_BYTES = 56 * 1024 * 1024

PROJ_TM = 256
MIX_T = 256
OUT_BLOCK = 256
OUT_BLOCKS = D_MODEL // OUT_BLOCK
CONV_HALO = SUBLANES
POOL_HALO = 4 * SUBLANES

KV_BLOCK = 2 * B_KV_WIDTH
KV_COL = (3 * A_WIDTH + B_WIDTH) // KV_BLOCK
GATE_B_COL = KV_COL + 1

F32 = jnp.float32
BF16 = jnp.bfloat16


def _sigmoid(x):
    return 0.5 * jnp.tanh(0.5 * x) + 0.5


def _silu(x):
    hx = 0.5 * x
    return hx * jnp.tanh(hx) + hx


def _resident(shape):
    return pl.BlockSpec(shape, lambda *_: (0,) * len(shape), pipeline_mode=pl.Buffered(1))


def _whole(arr):
    return pl.BlockSpec(arr.shape, lambda *_: (0,) * arr.ndim)


def _layer_block(shape, layer):
    return pl.BlockSpec((None,) + shape, lambda *_: (layer,) + (0,) * len(shape))


def _compiler_params(*semantics):
    return pltpu.CompilerParams(dimension_semantics=semantics, vmem_limit_bytes=VMEM_LIMIT_BYTES)


def _proj_in_kernel(*refs, cast_next):
    if cast_next:
        x_ref, w_ref, wout_ref, wn_ref, o_ref, wout_bf_ref, wn_bf_ref = refs
        wn_bf_ref[...] = wn_ref[...].astype(BF16)
    else:
        x_ref, w_ref, wout_ref, o_ref, wout_bf_ref = refs
    wout_bf_ref[...] = wout_ref[...].astype(BF16)
    o_ref[...] = jnp.dot(x_ref[...].astype(BF16), w_ref[...], preferred_element_type=F32)


def _proj_in(x, w, w_out, layer, w_next, layer_next):
    m, k = x.shape
    n = w.shape[1]
    steps = m // PROJ_TM

    def slab_specs(stacked, which):
        rows, cols = stacked.shape[1:]
        slab = rows // steps
        return (pl.BlockSpec((None, slab, cols), lambda i: (which, i, 0)),
                pl.BlockSpec((slab, cols), lambda i: (i, 0)),
                jax.ShapeDtypeStruct((rows, cols), BF16))

    casts = [slab_specs(w_out, layer)]
    args = [x, w, w_out]
    if w_next is not None:
        casts.append(slab_specs(w_next, layer_next))
        args.append(w_next)
    outs = pl.pallas_call(
        partial(_proj_in_kernel, cast_next=w_next is not None),
        grid=(steps,),
        in_specs=[pl.BlockSpec((PROJ_TM, k), lambda i: (i, 0)), _resident((k, n))]
        + [c[0] for c in casts],
        out_specs=[pl.BlockSpec((PROJ_TM, n), lambda i: (i, 0))] + [c[1] for c in casts],
        out_shape=[jax.ShapeDtypeStruct((m, n), F32)] + [c[2] for c in casts],
        compiler_params=_compiler_params("parallel"),
        name="proj_in",
    )(*args)
    return outs if w_next is not None else (*outs, None)


class _OutProj:
    def __init__(self, mix_ref, wout_ref, x_ref, z_ref):
        self.refs = (mix_ref, wout_ref, x_ref, z_ref)
        self.done = 0

    def emit(self, count=1):
        mix_ref, wout_ref, x_ref, z_ref = self.refs
        for _ in range(min(count, OUT_BLOCKS - self.done)):
            cols = slice(self.done * OUT_BLOCK, (self.done + 1) * OUT_BLOCK)
            z_ref[:, cols] = DN_ALPHA * x_ref[:, cols] + jnp.dot(
                mix_ref[...], wout_ref[:, cols], preferred_element_type=F32)
            self.done += 1

    def finish(self):
        self.emit(OUT_BLOCKS)


def _ln_rows(z_ref, g_ref, b_ref, layer, o_ref, rows):
    z = z_ref[rows, :]
    mu = jnp.mean(z, axis=-1, keepdims=True)
    zc = z - mu
    var = jnp.mean(zc * zc, axis=-1, keepdims=True)
    o_ref[rows, :] = (zc * lax.rsqrt(var + LN_EPS) * g_ref[layer:layer + 1, :]
                      + b_ref[layer:layer + 1, :])


def _lagged_tiles(n_tiles):
    def clamp(i):
        return jnp.clip(i, 0, n_tiles - 1)

    return clamp, (lambda i: clamp(i - 1)), (lambda i: clamp(i - 2))


def _two_phase(step, body, mix_a, mix_b, z_a, z_b):
    @pl.when(step == 0)
    def _():
        mix_b[...] = jnp.zeros_like(mix_b)
        z_b[...] = jnp.zeros_like(z_b)

    @pl.when(step % 2 == 0)
    def _():
        body(mix_a, mix_b, z_a, z_b)

    @pl.when(step % 2 == 1)
    def _():
        body(mix_b, mix_a, z_b, z_a)


def _tail_specs(x, w_out_bf, ln_g, ln_b, res, out):
    m, d = x.shape
    in_specs = [
        pl.BlockSpec((MIX_T, d), lambda i: (res(i), 0)),
        _resident(w_out_bf.shape),
        _whole(ln_g),
        _whole(ln_b),
    ]
    out_spec = pl.BlockSpec((MIX_T, d), lambda i: (out(i), 0))
    out_shape = jax.ShapeDtypeStruct((m, d), F32)
    scratch = [pltpu.VMEM((MIX_T, MIX_WIDTH), BF16), pltpu.VMEM((MIX_T, MIX_WIDTH), BF16),
               pltpu.VMEM((MIX_T, d), F32), pltpu.VMEM((MIX_T, d), F32)]
    return in_specs, out_spec, out_shape, scratch, [x, w_out_bf, ln_g, ln_b]


def _rope_tile(t, cos, sin, take_upper):
    upper = pltpu.roll(t, LANES - ROT_DIM // 2, 1)
    lower = pltpu.roll(t, ROT_DIM // 2, 1)
    return t * cos + jnp.where(take_upper, upper, lower) * sin


def _even_layer_kernel(*refs, layer, n_tiles, n_s):
    *io_refs, mix_a, mix_b, z_a, z_b = refs
    step = pl.program_id(0)
    seq_tile = jnp.minimum(step, n_tiles - 1) % n_s
    _two_phase(step, partial(_even_step, *io_refs, layer=layer, seq_tile=seq_tile),
               mix_a, mix_b, z_a, z_b)


def _even_step(sinks_ref, u_ref, v_ref, ag_ref, q_ref, kvc_ref, kvp_ref, bg0_ref, bg1_ref, bg2_ref,
               bg3_ref, cosc_ref, sinc_ref, cosp_ref, sinp_ref, lng_ref, lnb_ref, ws_ref, bst_ref,
               x_ref, wout_ref, g_ref, b_ref, o_ref, mix_ref, mix_prev, z_ref, z_prev,
               *, layer, seq_tile):
    t_rows = u_ref.shape[0]
    n_chunks = t_rows // CHUNK
    bg_refs = (bg0_ref, bg1_ref, bg2_ref, bg3_ref)
    ln_rows = [slice(r * t_rows // 2, (r + 1) * t_rows // 2) for r in range(2)]
    out_proj = _OutProj(mix_prev, wout_ref, x_ref, z_ref)
    out_proj.emit()

    v = v_ref[...]
    mu = jnp.mean(v, axis=-1, keepdims=True)
    vc = v - mu
    var = jnp.mean(vc * vc, axis=-1, keepdims=True)
    vln = (vc * lax.rsqrt(var + LN_EPS) * lng_ref[layer:layer + 1, :]
           + lnb_ref[layer:layer + 1, :]).astype(BF16)
    causal = (lax.broadcasted_iota(jnp.int32, (CHUNK, CHUNK), 0)
              >= lax.broadcasted_iota(jnp.int32, (CHUNK, CHUNK), 1))
    lane = lax.broadcasted_iota(jnp.int32, (1, LANES), 1)
    take_upper = (lane % B_HEAD_DIM) < (ROT_DIM // 2)
    low_half = lane < B_HEAD_DIM
    cosc = cosc_ref[...]
    sinc = sinc_ref[...]
    scale = B_HEAD_DIM ** -0.5
    q_tiles = []
    for g in range(A_GROUPS):
        cols = slice(g * CHUNK, (g + 1) * CHUNK)
        w = jnp.where(causal, ws_ref[g], 0.0).astype(BF16)
        rhs = jnp.concatenate(
            [vln[c * CHUNK:(c + 1) * CHUNK, cols] for c in range(n_chunks)], axis=1)
        mixed = jnp.dot(w, rhs, preferred_element_type=F32) + bst_ref[:, g:g + 1]
        for c in range(n_chunks):
            rows = slice(c * CHUNK, (c + 1) * CHUNK)
            gate = _silu(ag_ref[rows, cols])
            mix_ref[rows, cols] = (u_ref[rows, cols] * mixed[:, c * CHUNK:(c + 1) * CHUNK]
                                   * gate).astype(BF16)
        q_tiles.append(
            (_rope_tile(q_ref[:, cols], cosc, sinc, take_upper) * scale).astype(BF16))
        if g in (1, 5):
            _ln_rows(z_prev, g_ref, b_ref, layer, o_ref, ln_rows[g // 4])
        if g in (2, 4, 6):
            out_proj.emit()

    k_all = jnp.concatenate(
        [_rope_tile(kvp_ref[:, :B_KV_WIDTH], cosp_ref[...], sinp_ref[...], take_upper),
         _rope_tile(kvc_ref[:, :B_KV_WIDTH], cosc, sinc, take_upper)], axis=0)
    k_swap = pltpu.roll(k_all, B_HEAD_DIM, 1)
    zero = jnp.zeros_like(k_all)
    k_lo = (jnp.where(low_half, k_all, zero).astype(BF16), jnp.where(low_half, k_swap, zero).astype(BF16))
    k_hi = (jnp.where(low_half, zero, k_swap).astype(BF16), jnp.where(low_half, zero, k_all).astype(BF16))
    v_all = jnp.concatenate([kvp_ref[:, B_KV_WIDTH:], kvc_ref[:, B_KV_WIDTH:]], axis=0)
    vt_all = v_all.T.astype(BF16)

    tiles_per_kv = B_WIDTH // LANES // B_KV_HEADS
    band_keys = 2 * WINDOW
    kj = lax.broadcasted_iota(jnp.int32, (band_keys, WINDOW), 0)
    qi = lax.broadcasted_iota(jnp.int32, (band_keys, WINDOW), 1)
    diff = qi + WINDOW - kj
    band = (diff >= 0) & (diff < WINDOW)
    first_key = jnp.where(seq_tile == 0, WINDOW, 0)
    neg_inf = jnp.full((band_keys, WINDOW), -jnp.inf, F32)
    mask_any = jnp.where(band, 0.0, neg_inf)
    mask_first = jnp.where(band & (kj >= first_key), 0.0, neg_inf)

    for n in range(t_rows // WINDOW):
        rows = slice(n * WINDOW, (n + 1) * WINDOW)
        band_rows = slice(n * WINDOW, (n + 2) * WINDOW)
        mask = jnp.concatenate([mask_first if n == 0 else mask_any] * tiles_per_kv, axis=1)
        for kv in range(B_KV_HEADS):
            out_proj.emit()
            q4 = jnp.concatenate(
                [q_tiles[kv * tiles_per_kv + j][rows] for j in range(tiles_per_kv)], axis=0)
            k_cat = jnp.concatenate([k_lo[kv][band_rows], k_hi[kv][band_rows]], axis=0)
            st = lax.dot_general(k_cat, q4, (((1,), (1,)), ((), ())), preferred_element_type=F32)
            vt = vt_all[kv * B_HEAD_DIM:(kv + 1) * B_HEAD_DIM, band_rows]
            o_halves = []
            for half in range(2):
                sh = st[half * band_keys:(half + 1) * band_keys] + mask
                sink = jnp.concatenate(
                    [jnp.full((1, WINDOW), sinks_ref[layer, 2 * (kv * tiles_per_kv + j) + half], F32)
                     for j in range(tiles_per_kv)], axis=1)
                m = jnp.maximum(jnp.max(sh, axis=0, keepdims=True), sink)
                p = jnp.exp(sh - m)
                l = jnp.sum(p, axis=0, keepdims=True) + jnp.exp(sink - m)
                o_half = jnp.dot(vt, p.astype(BF16), preferred_element_type=F32)
                o_halves.append(o_half * (1.0 / l))
            ot = jnp.concatenate(o_halves, axis=0)
            for j in range(tiles_per_kv):
                jj = kv * tiles_per_kv + j
                gate_ref = bg_refs[jj // 2]
                gate_cols = slice((jj % 2) * LANES, (jj % 2 + 1) * LANES)
                out_cols = slice(A_WIDTH + jj * LANES, A_WIDTH + (jj + 1) * LANES)
                o_tile = ot[:, j * WINDOW:(j + 1) * WINDOW].T
                mix_ref[rows, out_cols] = (o_tile * _silu(gate_ref[rows, gate_cols])).astype(BF16)

    out_proj.finish()


def _even_layer(h, x, sinks, cos_t, sin_t, a_ln_g, a_ln_b, ws, bst, w_out_bf, ln_g, ln_b, layer, seq):
    m = h.shape[0]
    t = MIX_T
    n_tiles = m // t
    n_s = seq // t
    blocks_per_tile = t // WINDOW
    cur, res, out = _lagged_tiles(n_tiles)

    def prev(i):
        return jnp.maximum(cur(i) * blocks_per_tile - 1, 0)

    wide = lambda c: pl.BlockSpec((t, A_WIDTH), lambda i: (cur(i), c))
    gate = lambda c: pl.BlockSpec((t, KV_BLOCK), lambda i: (cur(i), GATE_B_COL + c))
    tail_in, out_specs, out_shape, tail_scratch, tail_args = _tail_specs(x, w_out_bf, ln_g, ln_b, res, out)
    return pl.pallas_call(
        partial(_even_layer_kernel, layer=layer, n_tiles=n_tiles, n_s=n_s),
        grid=(n_tiles + 2,),
        in_specs=[
            pl.BlockSpec(memory_space=pltpu.SMEM),
            wide(0), wide(1), wide(2), wide(3),
            pl.BlockSpec((t, KV_BLOCK), lambda i: (cur(i), KV_COL)),
            pl.BlockSpec((WINDOW, KV_BLOCK), lambda i: (prev(i), KV_COL)),
            gate(0), gate(1), gate(2), gate(3),
            pl.BlockSpec((t, LANES), lambda i: (cur(i), 0)),
            pl.BlockSpec((t, LANES), lambda i: (cur(i), 0)),
            pl.BlockSpec((WINDOW, LANES), lambda i: (prev(i), 0)),
            pl.BlockSpec((WINDOW, LANES), lambda i: (prev(i), 0)),
            _whole(a_ln_g),
            _whole(a_ln_b),
            _layer_block((A_GROUPS, CHUNK, CHUNK), layer),
            _layer_block((CHUNK, A_GROUPS), layer),
        ] + tail_in,
        out_specs=out_specs,
        out_shape=out_shape,
        scratch_shapes=tail_scratch,
        compiler_params=_compiler_params("arbitrary"),
        name="even_layer",
    )(sinks, h, h, h, h, h, h, h, h, h, h, cos_t, sin_t, cos_t, sin_t, a_ln_g, a_ln_b, ws, bst,
      *tail_args)


def _odd_layer_kernel(*refs, layer, n_tiles, n_s):
    *io_refs, mix_a, mix_b, z_a, z_b = refs[:22]
    (ext_c, ext_d, lvl2, lvl4, hstate, sa, sb, sh, wax_bf, wpool_bf) = scratch = refs[22:]
    xc_ref = io_refs[0]
    wa_ref, wx_ref, wpool_ref = io_refs[6], io_refs[7], io_refs[11]
    t_rows = xc_ref.shape[0]
    step = pl.program_id(0)
    s_idx = jnp.minimum(step, n_tiles - 1) % n_s

    @pl.when(s_idx == 0)
    def _():
        ext_c[:, 0:CONV_HALO, :] = jnp.zeros((C_HEADS, CONV_HALO, LANES), F32)
        ext_d[:, 0:POOL_HALO, :] = jnp.zeros((D_WIDTH // LANES, POOL_HALO, LANES), F32)
        hstate[...] = jnp.zeros_like(hstate)
        wax_bf[:, :, :C_HEAD_DIM] = wa_ref[...].astype(BF16)
        wax_bf[:, :, C_HEAD_DIM:] = wx_ref[...].astype(BF16)
        wpool_bf[...] = wpool_ref[...].astype(BF16)

    @pl.when(s_idx > 0)
    def _():
        ext_c[:, 0:CONV_HALO, :] = ext_c[:, t_rows:t_rows + CONV_HALO, :]
        ext_d[:, 0:POOL_HALO, :] = ext_d[:, t_rows:t_rows + POOL_HALO, :]

    _two_phase(step, partial(_odd_step, *io_refs, *scratch, layer=layer, s_idx=s_idx),
               mix_a, mix_b, z_a, z_b)


def _odd_step(xc_ref, cg_ref, xd_ref, dg_ref, convw_ref, convb_ref, wa_ref, wx_ref, ba_ref, bx_ref,
              lam_ref, wpool_ref, dscale_ref, x_ref, wout_ref, g_ref, b_ref, o_ref,
              ext_c, ext_d, lvl2, lvl4, hstate, sa, sb, sh, wax_bf, wpool_bf,
              mix_ref, mix_prev, z_ref, z_prev, *, layer, s_idx):
    t_rows = xc_ref.shape[0]
    row = slice(layer, layer + 1)
    n_groups = t_rows // SUBLANES
    ln_rows = [slice(r * t_rows // 2, (r + 1) * t_rows // 2) for r in range(2)]
    out_proj = _OutProj(mix_prev, wout_ref, x_ref, z_ref)
    out_proj.emit()

    z = -lam_ref[row, :]
    softplus = jnp.maximum(z, 0.0) + jnp.log1p(jnp.exp(-jnp.abs(z)))
    for hd in range(C_HEADS):
        cols = slice(hd * C_HEAD_DIM, (hd + 1) * C_HEAD_DIM)
        ext_c[hd, CONV_HALO:, :] = xc_ref[:, cols]
        xconv = convb_ref[row, cols]
        for j in range(CONV_WIDTH):
            lo = CONV_HALO - (CONV_WIDTH - 1) + j
            xconv = xconv + ext_c[hd, lo:lo + t_rows, :] * convw_ref[j:j + 1, cols]
        ri = jnp.dot(xconv.astype(BF16), wax_bf[hd], preferred_element_type=F32)
        r = _sigmoid(ri[:, :C_HEAD_DIM] + ba_ref[row, cols])
        i = _sigmoid(ri[:, C_HEAD_DIM:] + bx_ref[row, cols])
        log_a = -LRU_C * r * softplus[:, cols]
        a = jnp.exp(log_a)
        m2 = -jnp.tanh(log_a) * (a * a + 1.0)
        mult = jnp.where(m2 == 0.0, 0.0, m2 * lax.rsqrt(m2))
        b = mult * (i * xconv)
        for g in range(n_groups):
            base = (g * C_HEADS + hd) * SUBLANES
            sa[base:base + SUBLANES, :] = a[g * SUBLANES:(g + 1) * SUBLANES]
            sb[base:base + SUBLANES, :] = b[g * SUBLANES:(g + 1) * SUBLANES]
        if hd in (2, 5):
            _ln_rows(z_prev, g_ref, b_ref, layer, o_ref, ln_rows[hd // 4])
        if hd in (1, 3, 4, 6, 7):
            out_proj.emit()

    h = hstate[...]
    for t in range(t_rows):
        g, s = divmod(t, SUBLANES)
        at_t = pl.ds(g * C_HEADS * SUBLANES + s, C_HEADS, stride=SUBLANES)
        h = sa[at_t, :] * h + sb[at_t, :]
        sh[at_t, :] = h
    hstate[...] = h
    out_proj.emit()
    for hd in range(C_HEADS):
        cols = slice(hd * C_HEAD_DIM, (hd + 1) * C_HEAD_DIM)
        hcol = jnp.concatenate(
            [sh[(g * C_HEADS + hd) * SUBLANES:(g * C_HEADS + hd + 1) * SUBLANES, :]
             for g in range(n_groups)], axis=0)
        mix_ref[:, cols] = (hcol * _silu(cg_ref[:, cols])).astype(BF16)

    n_ext = POOL_HALO + t_rows
    tiles_per_group = D_GROUP_DIM // LANES
    pos1 = s_idx * t_rows + lax.broadcasted_iota(jnp.int32, (t_rows, LANES), 0) + 1
    for g, w in enumerate(POOL_WINDOWS):
        if g == 2:
            out_proj.emit()
        inv = 1.0 / jnp.minimum(pos1, w).astype(F32)
        pooled = []
        for k in range(tiles_per_group):
            tile = g * tiles_per_group + k
            cols = slice(tile * LANES, (tile + 1) * LANES)
            ext_d[tile, POOL_HALO:, :] = xd_ref[:, cols]
            lo = SUBLANES
            win = ext_d[tile, lo:n_ext, :] + ext_d[tile, lo - 1:n_ext - 1, :]
            if w >= 4:
                lvl2[k, lo:n_ext, :] = win
                lo += SUBLANES
                win = lvl2[k, lo:n_ext, :] + lvl2[k, lo - 2:n_ext - 2, :]
            if w >= 8:
                lvl4[k, lo:n_ext, :] = win
                lo += SUBLANES
                win = lvl4[k, lo:n_ext, :] + lvl4[k, lo - 4:n_ext - 4, :]
            if w >= 16:
                lo += SUBLANES
                win = win[SUBLANES:] + win[:-SUBLANES]
            pooled.append((win[POOL_HALO - lo:] * inv - xd_ref[:, cols]).astype(BF16))
        cols = slice(g * D_GROUP_DIM, (g + 1) * D_GROUP_DIM)
        mixed = jnp.dot(jnp.concatenate(pooled, axis=1), wpool_bf[g], preferred_element_type=F32)
        out_cols = slice(C_WIDTH + g * D_GROUP_DIM, C_WIDTH + (g + 1) * D_GROUP_DIM)
        mix_ref[:, out_cols] = (mixed * dscale_ref[row, cols] * _silu(dg_ref[:, cols])).astype(BF16)

    out_proj.finish()


def _odd_layer(h, x, conv_w, conv_b, w_a, w_x, b_a, b_x, lam, w_pool, d_scale, w_out_bf, ln_g, ln_b,
               layer, seq):
    m = h.shape[0]
    t = MIX_T
    n_tiles = m // t
    n_s = seq // t
    cur, res, out = _lagged_tiles(n_tiles)
    wide = lambda c: pl.BlockSpec((t, C_WIDTH), lambda i: (cur(i), c))
    n_pool = len(POOL_WINDOWS)
    tail_in, out_specs, out_shape, tail_scratch, tail_args = _tail_specs(x, w_out_bf, ln_g, ln_b, res, out)
    return pl.pallas_call(
        partial(_odd_layer_kernel, layer=layer, n_tiles=n_tiles, n_s=n_s),
        grid=(n_tiles + 2,),
        in_specs=[
            wide(0), wide(1), wide(2), wide(3),
            _layer_block((CONV_WIDTH, C_WIDTH), layer),
            _whole(conv_b),
            _layer_block((C_HEADS, C_HEAD_DIM, C_HEAD_DIM), layer),
            _layer_block((C_HEADS, C_HEAD_DIM, C_HEAD_DIM), layer),
            _whole(b_a), _whole(b_x), _whole(lam),
            _layer_block((n_pool, D_GROUP_DIM, D_GROUP_DIM), layer),
            _whole(d_scale),
        ] + tail_in,
        out_specs=out_specs,
        out_shape=out_shape,
        scratch_shapes=tail_scratch + [
            pltpu.VMEM((C_WIDTH // LANES, CONV_HALO + t, LANES), F32),
            pltpu.VMEM((D_WIDTH // LANES, POOL_HALO + t, LANES), F32),
            pltpu.VMEM((D_GROUP_DIM // LANES, POOL_HALO + t, LANES), F32),
            pltpu.VMEM((D_GROUP_DIM // LANES, POOL_HALO + t, LANES), F32),
            pltpu.VMEM((SUBLANES, LANES), F32),
            pltpu.VMEM((t * C_HEADS, LANES), F32),
            pltpu.VMEM((t * C_HEADS, LANES), F32),
            pltpu.VMEM((t * C_HEADS, LANES), F32),
            pltpu.VMEM((C_HEADS, C_HEAD_DIM, 2 * C_HEAD_DIM), BF16),
            pltpu.VMEM((n_pool, D_GROUP_DIM, D_GROUP_DIM), BF16),
        ],
        compiler_params=_compiler_params("arbitrary"),
        name="odd_layer",
    )(h, h, h, h, conv_w, conv_b, w_a, w_x, b_a, b_x, lam, w_pool, d_scale, *tail_args)


def _rope_tables(positions):
    half = ROT_DIM // 2
    inv_freq = ROPE_THETA ** (-jnp.arange(0, ROT_DIM, 2, dtype=F32) / ROT_DIM)
    rest = jnp.zeros((B_HEAD_DIM - ROT_DIM,), F32)
    freq_head = jnp.concatenate([inv_freq, inv_freq, rest])
    sign_head = jnp.concatenate([-jnp.ones((half,), F32), jnp.ones((half,), F32), rest])
    reps = LANES // B_HEAD_DIM
    ang = positions.astype(F32).reshape(-1, 1) * jnp.tile(freq_head, reps)[None, :]
    return jnp.cos(ang), jnp.sin(ang) * jnp.tile(sign_head, reps)[None, :]


def kernel(x, positions, even_w_in, even_a_ln_g, even_a_ln_b, even_a_ws, even_a_bs, even_b_sinks, even_w_out, even_ln_g, even_ln_b, odd_w_in, odd_conv_w, odd_conv_b, odd_w_a, odd_b_a, odd_w_x, odd_b_x, odd_lam, odd_w_pool, odd_d_scale, odd_w_out, odd_ln_g, odd_ln_b):
    batch, seq, d = x.shape
    m = batch * seq
    xf = x.reshape(m, d)
    cos_t, sin_t = _rope_tables(positions)
    even_bst = jnp.swapaxes(even_a_bs, 1, 2)
    w_in = even_w_in[0].astype(BF16)
    for layer in range(DEPTH):
        j = layer // 2
        if layer % 2 == 0:
            h, w_out, w_in = _proj_in(xf, w_in, even_w_out, j, odd_w_in, j)
            xf = _even_layer(h, xf, even_b_sinks, cos_t, sin_t, even_a_ln_g, even_a_ln_b, even_a_ws,
                             even_bst, w_out, even_ln_g, even_ln_b, j, seq)
        else:
            w_next = even_w_in if layer + 1 < DEPTH else None
            h, w_out, w_in = _proj_in(xf, w_in, odd_w_out, j, w_next, j + 1)
            xf = _odd_layer(h, xf, odd_conv_w, odd_conv_b, odd_w_a, odd_w_x, odd_b_a, odd_b_x, odd_lam,
                            odd_w_pool, odd_d_scale, w_out, odd_ln_g, odd_ln_b, j, seq)
    return xf.reshape(batch, seq, d)
```

```python
from functools import partial

import jax
import jax.numpy as jnp
from jax import lax
from jax.experimental import pallas as pl
from jax.experimental.pallas import tpu as pltpu

D_MODEL = 2048
DEPTH = 4
A_WIDTH = 1024
A_GROUPS = 8
CHUNK = 128
B_HEAD_DIM = 64
B_Q_HEADS = 16
B_KV_HEADS = 2
B_WIDTH = 1024
B_KV_WIDTH = 128
WINDOW = 128
ROT_DIM = 16
ROPE_THETA = 500000.0
C_WIDTH = 1024
C_HEADS = 8
C_HEAD_DIM = 128
CONV_WIDTH = 4
LRU_C = 8.0
D_WIDTH = 1024
POOL_WINDOWS = (2, 4, 8, 16)
D_GROUP_DIM = 256
MIX_WIDTH = 2048
DN_ALPHA = (2 * DEPTH) ** 0.25
LN_EPS = 1e-5

LANES = 128
SUBLANES = 8
VMEM_LIMIT_BYTES = 56 * 1024 * 1024

PROJ_TM = 256
MIX_T = 256
OUT_BLOCK = 256
OUT_BLOCKS = D_MODEL // OUT_BLOCK
CONV_HALO = SUBLANES
POOL_HALO = 4 * SUBLANES

KV_BLOCK = 2 * B_KV_WIDTH
KV_COL = (3 * A_WIDTH + B_WIDTH) // KV_BLOCK
GATE_B_COL = KV_COL + 1

F32 = jnp.float32
BF16 = jnp.bfloat16


def _sigmoid(x):
    return 0.5 * jnp.tanh(0.5 * x) + 0.5


def _silu(x):
    hx = 0.5 * x
    return hx * jnp.tanh(hx) + hx


def _resident(shape):
    return pl.BlockSpec(shape, lambda *_: (0,) * len(shape), pipeline_mode=pl.Buffered(1))


def _whole(arr):
    return pl.BlockSpec(arr.shape, lambda *_: (0,) * arr.ndim)


def _layer_block(shape, layer):
    return pl.BlockSpec((None,) + shape, lambda *_: (layer,) + (0,) * len(shape))


def _compiler_params(*semantics):
    return pltpu.CompilerParams(dimension_semantics=semantics, vmem_limit_bytes=VMEM_LIMIT_BYTES)


def _proj_in_kernel(*refs, cast_next):
    if cast_next:
        x_ref, w_ref, wout_ref, wn_ref, o_ref, wout_bf_ref, wn_bf_ref = refs
        wn_bf_ref[...] = wn_ref[...].astype(BF16)
    else:
        x_ref, w_ref, wout_ref, o_ref, wout_bf_ref = refs
    for p in range(OUT_BLOCKS):
        wout_bf_ref[p] = wout_ref[:, p * OUT_BLOCK:(p + 1) * OUT_BLOCK].astype(BF16)
    o_ref[...] = jnp.dot(x_ref[...].astype(BF16), w_ref[...], preferred_element_type=F32)


def _proj_in(x, w, w_out, layer, w_next, layer_next):
    m, k = x.shape
    n = w.shape[1]
    steps = m // PROJ_TM

    def slab_specs(stacked, which):
        rows, cols = stacked.shape[1:]
        slab = rows // steps
        return (pl.BlockSpec((None, slab, cols), lambda i: (which, i, 0)),
                pl.BlockSpec((slab, cols), lambda i: (i, 0)),
                jax.ShapeDtypeStruct((rows, cols), BF16))

    k_out = w_out.shape[1]
    slab_out = k_out // steps
    casts = [(pl.BlockSpec((None, slab_out, D_MODEL), lambda i: (layer, i, 0)),
              pl.BlockSpec((OUT_BLOCKS, slab_out, OUT_BLOCK), lambda i: (0, i, 0)),
              jax.ShapeDtypeStruct((OUT_BLOCKS, k_out, OUT_BLOCK), BF16))]
    args = [x, w, w_out]
    if w_next is not None:
        casts.append(slab_specs(w_next, layer_next))
        args.append(w_next)
    outs = pl.pallas_call(
        partial(_proj_in_kernel, cast_next=w_next is not None),
        grid=(steps,),
        in_specs=[pl.BlockSpec((PROJ_TM, k), lambda i: (i, 0)), _resident((k, n))]
        + [c[0] for c in casts],
        out_specs=[pl.BlockSpec((PROJ_TM, n), lambda i: (i, 0))] + [c[1] for c in casts],
        out_shape=[jax.ShapeDtypeStruct((m, n), F32)] + [c[2] for c in casts],
        compiler_params=_compiler_params("parallel"),
        name="proj_in",
    )(*args)
    return outs if w_next is not None else (*outs, None)


class _OutProj:
    def __init__(self, mix_ref, wout_ref, x_ref, z_ref):
        self.mix = mix_ref[...]
        self.refs = (wout_ref, x_ref, z_ref)
        self.done = 0

    def emit(self, count=1):
        wout_ref, x_ref, z_ref = self.refs
        for _ in range(min(count, OUT_BLOCKS - self.done)):
            cols = slice(self.done * OUT_BLOCK, (self.done + 1) * OUT_BLOCK)
            z_ref[:, cols] = DN_ALPHA * x_ref[:, cols] + jnp.dot(
                self.mix, wout_ref[self.done], preferred_element_type=F32)
            self.done += 1

    def finish(self):
        self.emit(OUT_BLOCKS)


def _ln_rows(z_ref, g_ref, b_ref, layer, o_ref, rows):
    z = z_ref[rows, :]
    mu = jnp.mean(z, axis=-1, keepdims=True)
    zc = z - mu
    var = jnp.mean(zc * zc, axis=-1, keepdims=True)
    o_ref[rows, :] = (zc * lax.rsqrt(var + LN_EPS) * g_ref[layer:layer + 1, :]
                      + b_ref[layer:layer + 1, :])


def _lagged_tiles(n_tiles):
    def clamp(i):
        return jnp.clip(i, 0, n_tiles - 1)

    return clamp, (lambda i: clamp(i - 1)), (lambda i: clamp(i - 2))


def _two_phase(step, body, mix_a, mix_b, z_a, z_b):
    @pl.when(step == 0)
    def _():
        mix_b[...] = jnp.zeros_like(mix_b)
        z_b[...] = jnp.zeros_like(z_b)

    @pl.when(step % 2 == 0)
    def _():
        body(mix_a, mix_b, z_a, z_b)

    @pl.when(step % 2 == 1)
    def _():
        body(mix_b, mix_a, z_b, z_a)


def _tail_specs(x, w_out_bf, ln_g, ln_b, res, out):
    m, d = x.shape
    in_specs = [
        pl.BlockSpec((MIX_T, d), lambda i: (res(i), 0)),
        _resident(w_out_bf.shape),
        _whole(ln_g),
        _whole(ln_b),
    ]
    out_spec = pl.BlockSpec((MIX_T, d), lambda i: (out(i), 0))
    out_shape = jax.ShapeDtypeStruct((m, d), F32)
    scratch = [pltpu.VMEM((MIX_T, MIX_WIDTH), BF16), pltpu.VMEM((MIX_T, MIX_WIDTH), BF16),
               pltpu.VMEM((MIX_T, d), F32), pltpu.VMEM((MIX_T, d), F32)]
    return in_specs, out_spec, out_shape, scratch, [x, w_out_bf, ln_g, ln_b]


def _rope_tile(t, cos, sin, take_upper):
    upper = pltpu.roll(t, LANES - ROT_DIM // 2, 1)
    lower = pltpu.roll(t, ROT_DIM // 2, 1)
    return t * cos + jnp.where(take_upper, upper, lower) * sin


def _even_layer_kernel(*refs, layer, n_tiles, n_s):
    *io_refs, mix_a, mix_b, z_a, z_b = refs
    step = pl.program_id(0)
    seq_tile = jnp.minimum(step, n_tiles - 1) % n_s
    _two_phase(step, partial(_even_step, *io_refs, layer=layer, seq_tile=seq_tile),
               mix_a, mix_b, z_a, z_b)


def _even_step(sinks_ref, u_ref, v_ref, ag_ref, q_ref, kvc_ref, kvp_ref, bg0_ref, bg1_ref, bg2_ref,
               bg3_ref, cosc_ref, sinc_ref, cosp_ref, sinp_ref, lng_ref, lnb_ref, ws_ref, bst_ref,
               x_ref, wout_ref, g_ref, b_ref, o_ref, mix_ref, mix_prev, z_ref, z_prev,
               *, layer, seq_tile):
    t_rows = u_ref.shape[0]
    n_chunks = t_rows // CHUNK
    bg_refs = (bg0_ref, bg1_ref, bg2_ref, bg3_ref)
    ln_rows = [slice(r * t_rows // 2, (r + 1) * t_rows // 2) for r in range(2)]
    out_proj = _OutProj(mix_prev, wout_ref, x_ref, z_ref)
    out_proj.emit()

    v = v_ref[...]
    mu = jnp.mean(v, axis=-1, keepdims=True)
    vc = v - mu
    var = jnp.mean(vc * vc, axis=-1, keepdims=True)
    vln = (vc * lax.rsqrt(var + LN_EPS) * lng_ref[layer:layer + 1, :]
           + lnb_ref[layer:layer + 1, :]).astype(BF16)
    causal = (lax.broadcasted_iota(jnp.int32, (CHUNK, CHUNK), 0)
              >= lax.broadcasted_iota(jnp.int32, (CHUNK, CHUNK), 1))
    lane = lax.broadcasted_iota(jnp.int32, (1, LANES), 1)
    take_upper = (lane % B_HEAD_DIM) < (ROT_DIM // 2)
    low_half = lane < B_HEAD_DIM
    cosc = cosc_ref[...]
    sinc = sinc_ref[...]
    scale = B_HEAD_DIM ** -0.5
    q_tiles = []
    for g in range(A_GROUPS):
        cols = slice(g * CHUNK, (g + 1) * CHUNK)
        w = jnp.where(causal, ws_ref[g], 0.0).astype(BF16)
        rhs = jnp.concatenate(
            [vln[c * CHUNK:(c + 1) * CHUNK, cols] for c in range(n_chunks)], axis=1)
        mixed = jnp.dot(w, rhs, preferred_element_type=F32) + bst_ref[:, g:g + 1]
        for c in range(n_chunks):
            rows = slice(c * CHUNK, (c + 1) * CHUNK)
            gate = _silu(ag_ref[rows, cols])
            mix_ref[rows, cols] = (u_ref[rows, cols] * mixed[:, c * CHUNK:(c + 1) * CHUNK]
                                   * gate).astype(BF16)
        q_tiles.append(
            (_rope_tile(q_ref[:, cols], cosc, sinc, take_upper) * scale).astype(BF16))
        if g in (1, 5):
            _ln_rows(z_prev, g_ref, b_ref, layer, o_ref, ln_rows[g // 4])
        if g in (2, 4, 6):
            out_proj.emit()

    k_all = jnp.concatenate(
        [_rope_tile(kvp_ref[:, :B_KV_WIDTH], cosp_ref[...], sinp_ref[...], take_upper),
         _rope_tile(kvc_ref[:, :B_KV_WIDTH], cosc, sinc, take_upper)], axis=0)
    k_swap = pltpu.roll(k_all, B_HEAD_DIM, 1)
    zero = jnp.zeros_like(k_all)
    k_lo = (jnp.where(low_half, k_all, zero).astype(BF16), jnp.where(low_half, k_swap, zero).astype(BF16))
    k_hi = (jnp.where(low_half, zero, k_swap).astype(BF16), jnp.where(low_half, zero, k_all).astype(BF16))
    v_all = jnp.concatenate([kvp_ref[:, B_KV_WIDTH:], kvc_ref[:, B_KV_WIDTH:]], axis=0)
    vt_all = v_all.T.astype(BF16)

    tiles_per_kv = B_WIDTH // LANES // B_KV_HEADS
    band_keys = 2 * WINDOW
    kj = lax.broadcasted_iota(jnp.int32, (band_keys, WINDOW), 0)
    qi = lax.broadcasted_iota(jnp.int32, (band_keys, WINDOW), 1)
    diff = qi + WINDOW - kj
    band = (diff >= 0) & (diff < WINDOW)
    first_key = jnp.where(seq_tile == 0, WINDOW, 0)
    neg_inf = jnp.full((band_keys, WINDOW), -jnp.inf, F32)
    mask_any = jnp.where(band, 0.0, neg_inf)
    mask_first = jnp.where(band & (kj >= first_key), 0.0, neg_inf)

    for n in range(t_rows // WINDOW):
        rows = slice(n * WINDOW, (n + 1) * WINDOW)
        band_rows = slice(n * WINDOW, (n + 2) * WINDOW)
        mask = jnp.concatenate([mask_first if n == 0 else mask_any] * tiles_per_kv, axis=1)
        for kv in range(B_KV_HEADS):
            out_proj.emit()
            q4 = jnp.concatenate(
                [q_tiles[kv * tiles_per_kv + j][rows] for j in range(tiles_per_kv)], axis=0)
            k_cat = jnp.concatenate([k_lo[kv][band_rows], k_hi[kv][band_rows]], axis=0)
            st = lax.dot_general(k_cat, q4, (((1,), (1,)), ((), ())), preferred_element_type=F32)
            vt = vt_all[kv * B_HEAD_DIM:(kv + 1) * B_HEAD_DIM, band_rows]
            o_halves = []
            for half in range(2):
                sh = st[half * band_keys:(half + 1) * band_keys] + mask
                sink = jnp.concatenate(
                    [jnp.full((1, WINDOW), sinks_ref[layer, 2 * (kv * tiles_per_kv + j) + half], F32)
                     for j in range(tiles_per_kv)], axis=1)
                m = jnp.maximum(jnp.max(sh, axis=0, keepdims=True), sink)
                p = jnp.exp(sh - m)
                l = jnp.sum(p, axis=0, keepdims=True) + jnp.exp(sink - m)
                o_half = jnp.dot(vt, p.astype(BF16), preferred_element_type=F32)
                o_halves.append(o_half * (1.0 / l))
            ot = jnp.concatenate(o_halves, axis=0)
            for j in range(tiles_per_kv):
                jj = kv * tiles_per_kv + j
                gate_ref = bg_refs[jj // 2]
                gate_cols = slice((jj % 2) * LANES, (jj % 2 + 1) * LANES)
                out_cols = slice(A_WIDTH + jj * LANES, A_WIDTH + (jj + 1) * LANES)
                o_tile = ot[:, j * WINDOW:(j + 1) * WINDOW].T
                mix_ref[rows, out_cols] = (o_tile * _silu(gate_ref[rows, gate_cols])).astype(BF16)

    out_proj.finish()


def _even_layer(h, x, sinks, cos_t, sin_t, a_ln_g, a_ln_b, ws, bst, w_out_bf, ln_g, ln_b, layer, seq):
    m = h.shape[0]
    t = MIX_T
    n_tiles = m // t
    n_s = seq // t
    blocks_per_tile = t // WINDOW
    cur, res, out = _lagged_tiles(n_tiles)

    def prev(i):
        return jnp.maximum(cur(i) * blocks_per_tile - 1, 0)

    wide = lambda c: pl.BlockSpec((t, A_WIDTH), lambda i: (cur(i), c))
    gate = lambda c: pl.BlockSpec((t, KV_BLOCK), lambda i: (cur(i), GATE_B_COL + c))
    tail_in, out_specs, out_shape, tail_scratch, tail_args = _tail_specs(x, w_out_bf, ln_g, ln_b, res, out)
    return pl.pallas_call(
        partial(_even_layer_kernel, layer=layer, n_tiles=n_tiles, n_s=n_s),
        grid=(n_tiles + 2,),
        in_specs=[
            pl.BlockSpec(memory_space=pltpu.SMEM),
            wide(0), wide(1), wide(2), wide(3),
            pl.BlockSpec((t, KV_BLOCK), lambda i: (cur(i), KV_COL)),
            pl.BlockSpec((WINDOW, KV_BLOCK), lambda i: (prev(i), KV_COL)),
            gate(0), gate(1), gate(2), gate(3),
            pl.BlockSpec((t, LANES), lambda i: (cur(i), 0)),
            pl.BlockSpec((t, LANES), lambda i: (cur(i), 0)),
            pl.BlockSpec((WINDOW, LANES), lambda i: (prev(i), 0)),
            pl.BlockSpec((WINDOW, LANES), lambda i: (prev(i), 0)),
            _whole(a_ln_g),
            _whole(a_ln_b),
            _layer_block((A_GROUPS, CHUNK, CHUNK), layer),
            _layer_block((CHUNK, A_GROUPS), layer),
        ] + tail_in,
        out_specs=out_specs,
        out_shape=out_shape,
        scratch_shapes=tail_scratch,
        compiler_params=_compiler_params("arbitrary"),
        name="even_layer",
    )(sinks, h, h, h, h, h, h, h, h, h, h, cos_t, sin_t, cos_t, sin_t, a_ln_g, a_ln_b, ws, bst,
      *tail_args)


def _odd_layer_kernel(*refs, layer, n_tiles, n_s):
    *io_refs, mix_a, mix_b, z_a, z_b = refs[:22]
    (ext_c, ext_d, lvl2, lvl4, hstate, sa, sb, sh, wax_bf, wpool_bf) = scratch = refs[22:]
    xc_ref = io_refs[0]
    wa_ref, wx_ref, wpool_ref = io_refs[6], io_refs[7], io_refs[11]
    t_rows = xc_ref.shape[0]
    step = pl.program_id(0)
    s_idx = jnp.minimum(step, n_tiles - 1) % n_s

    @pl.when(s_idx == 0)
    def _():
        ext_c[:, 0:CONV_HALO, :] = jnp.zeros((C_HEADS, CONV_HALO, LANES), F32)
        ext_d[:, 0:POOL_HALO, :] = jnp.zeros((D_WIDTH // LANES, POOL_HALO, LANES), F32)
        hstate[...] = jnp.zeros_like(hstate)
        wax_bf[:, :, :C_HEAD_DIM] = wa_ref[...].astype(BF16)
        wax_bf[:, :, C_HEAD_DIM:] = wx_ref[...].astype(BF16)
        wpool_bf[...] = wpool_ref[...].astype(BF16)

    @pl.when(s_idx > 0)
    def _():
        ext_c[:, 0:CONV_HALO, :] = ext_c[:, t_rows:t_rows + CONV_HALO, :]
        ext_d[:, 0:POOL_HALO, :] = ext_d[:, t_rows:t_rows + POOL_HALO, :]

    _two_phase(step, partial(_odd_step, *io_refs, *scratch, layer=layer, s_idx=s_idx),
               mix_a, mix_b, z_a, z_b)


def _odd_step(xc_ref, cg_ref, xd_ref, dg_ref, convw_ref, convb_ref, wa_ref, wx_ref, ba_ref, bx_ref,
              lam_ref, wpool_ref, dscale_ref, x_ref, wout_ref, g_ref, b_ref, o_ref,
              ext_c, ext_d, lvl2, lvl4, hstate, sa, sb, sh, wax_bf, wpool_bf,
              mix_ref, mix_prev, z_ref, z_prev, *, layer, s_idx):
    t_rows = xc_ref.shape[0]
    row = slice(layer, layer + 1)
    n_groups = t_rows // SUBLANES
    ln_rows = [slice(r * t_rows // 2, (r + 1) * t_rows // 2) for r in range(2)]
    out_proj = _OutProj(mix_prev, wout_ref, x_ref, z_ref)
    out_proj.emit()

    z = -lam_ref[row, :]
    softplus = jnp.maximum(z, 0.0) + jnp.log1p(jnp.exp(-jnp.abs(z)))
    for hd in range(C_HEADS):
        cols = slice(hd * C_HEAD_DIM, (hd + 1) * C_HEAD_DIM)
        ext_c[hd, CONV_HALO:, :] = xc_ref[:, cols]
        xconv = convb_ref[row, cols]
        for j in range(CONV_WIDTH):
            lo = CONV_HALO - (CONV_WIDTH - 1) + j
            xconv = xconv + ext_c[hd, lo:lo + t_rows, :] * convw_ref[j:j + 1, cols]
        ri = jnp.dot(xconv.astype(BF16), wax_bf[hd], preferred_element_type=F32)
        r = _sigmoid(ri[:, :C_HEAD_DIM] + ba_ref[row, cols])
        i = _sigmoid(ri[:, C_HEAD_DIM:] + bx_ref[row, cols])
        log_a = -LRU_C * r * softplus[:, cols]
        a = jnp.exp(log_a)
        m2 = -jnp.tanh(log_a) * (a * a + 1.0)
        mult = jnp.where(m2 == 0.0, 0.0, m2 * lax.rsqrt(m2))
        b = mult * (i * xconv)
        for g in range(n_groups):
            base = (g * C_HEADS + hd) * SUBLANES
            sa[base:base + SUBLANES, :] = a[g * SUBLANES:(g + 1) * SUBLANES]
            sb[base:base + SUBLANES, :] = b[g * SUBLANES:(g + 1) * SUBLANES]
        if hd in (2, 5):
            _ln_rows(z_prev, g_ref, b_ref, layer, o_ref, ln_rows[hd // 4])
        if hd in (1, 3, 4, 6, 7):
            out_proj.emit()

    h = hstate[...]
    for t in range(t_rows):
        g, s = divmod(t, SUBLANES)
        at_t = pl.ds(g * C_HEADS * SUBLANES + s, C_HEADS, stride=SUBLANES)
        h = sa[at_t, :] * h + sb[at_t, :]
        sh[at_t, :] = h
    hstate[...] = h
    out_proj.emit()
    for hd in range(C_HEADS):
        cols = slice(hd * C_HEAD_DIM, (hd + 1) * C_HEAD_DIM)
        hcol = jnp.concatenate(
            [sh[(g * C_HEADS + hd) * SUBLANES:(g * C_HEADS + hd + 1) * SUBLANES, :]
             for g in range(n_groups)], axis=0)
        mix_ref[:, cols] = (hcol * _silu(cg_ref[:, cols])).astype(BF16)

    n_ext = POOL_HALO + t_rows
    tiles_per_group = D_GROUP_DIM // LANES
    pos1 = s_idx * t_rows + lax.broadcasted_iota(jnp.int32, (t_rows, LANES), 0) + 1
    for g, w in enumerate(POOL_WINDOWS):
        if g == 2:
            out_proj.emit()
        inv = 1.0 / jnp.minimum(pos1, w).astype(F32)
        pooled = []
        for k in range(tiles_per_group):
            tile = g * tiles_per_group + k
            cols = slice(tile * LANES, (tile + 1) * LANES)
            ext_d[tile, POOL_HALO:, :] = xd_ref[:, cols]
            lo = SUBLANES
            win = ext_d[tile, lo:n_ext, :] + ext_d[tile, lo - 1:n_ext - 1, :]
            if w >= 4:
                lvl2[k, lo:n_ext, :] = win
                lo += SUBLANES
                win = lvl2[k, lo:n_ext, :] + lvl2[k, lo - 2:n_ext - 2, :]
            if w >= 8:
                lvl4[k, lo:n_ext, :] = win
                lo += SUBLANES
                win = lvl4[k, lo:n_ext, :] + lvl4[k, lo - 4:n_ext - 4, :]
            if w >= 16:
                lo += SUBLANES
                win = win[SUBLANES:] + win[:-SUBLANES]
            pooled.append((win[POOL_HALO - lo:] * inv - xd_ref[:, cols]).astype(BF16))
        cols = slice(g * D_GROUP_DIM, (g + 1) * D_GROUP_DIM)
        mixed = jnp.dot(jnp.concatenate(pooled, axis=1), wpool_bf[g], preferred_element_type=F32)
        out_cols = slice(C_WIDTH + g * D_GROUP_DIM, C_WIDTH + (g + 1) * D_GROUP_DIM)
        mix_ref[:, out_cols] = (mixed * dscale_ref[row, cols] * _silu(dg_ref[:, cols])).astype(BF16)

    out_proj.finish()


def _odd_layer(h, x, conv_w, conv_b, w_a, w_x, b_a, b_x, lam, w_pool, d_scale, w_out_bf, ln_g, ln_b,
               layer, seq):
    m = h.shape[0]
    t = MIX_T
    n_tiles = m // t
    n_s = seq // t
    cur, res, out = _lagged_tiles(n_tiles)
    wide = lambda c: pl.BlockSpec((t, C_WIDTH), lambda i: (cur(i), c))
    n_pool = len(POOL_WINDOWS)
    tail_in, out_specs, out_shape, tail_scratch, tail_args = _tail_specs(x, w_out_bf, ln_g, ln_b, res, out)
    return pl.pallas_call(
        partial(_odd_layer_kernel, layer=layer, n_tiles=n_tiles, n_s=n_s),
        grid=(n_tiles + 2,),
        in_specs=[
            wide(0), wide(1), wide(2), wide(3),
            _layer_block((CONV_WIDTH, C_WIDTH), layer),
            _whole(conv_b),
            _layer_block((C_HEADS, C_HEAD_DIM, C_HEAD_DIM), layer),
            _layer_block((C_HEADS, C_HEAD_DIM, C_HEAD_DIM), layer),
            _whole(b_a), _whole(b_x), _whole(lam),
            _layer_block((n_pool, D_GROUP_DIM, D_GROUP_DIM), layer),
            _whole(d_scale),
        ] + tail_in,
        out_specs=out_specs,
        out_shape=out_shape,
        scratch_shapes=tail_scratch + [
            pltpu.VMEM((C_WIDTH // LANES, CONV_HALO + t, LANES), F32),
            pltpu.VMEM((D_WIDTH // LANES, POOL_HALO + t, LANES), F32),
            pltpu.VMEM((D_GROUP_DIM // LANES, POOL_HALO + t, LANES), F32),
            pltpu.VMEM((D_GROUP_DIM // LANES, POOL_HALO + t, LANES), F32),
            pltpu.VMEM((SUBLANES, LANES), F32),
            pltpu.VMEM((t * C_HEADS, LANES), F32),
            pltpu.VMEM((t * C_HEADS, LANES), F32),
            pltpu.VMEM((t * C_HEADS, LANES), F32),
            pltpu.VMEM((C_HEADS, C_HEAD_DIM, 2 * C_HEAD_DIM), BF16),
            pltpu.VMEM((n_pool, D_GROUP_DIM, D_GROUP_DIM), BF16),
        ],
        compiler_params=_compiler_params("arbitrary"),
        name="odd_layer",
    )(h, h, h, h, conv_w, conv_b, w_a, w_x, b_a, b_x, lam, w_pool, d_scale, *tail_args)


def _rope_tables(positions):
    half = ROT_DIM // 2
    inv_freq = ROPE_THETA ** (-jnp.arange(0, ROT_DIM, 2, dtype=F32) / ROT_DIM)
    rest = jnp.zeros((B_HEAD_DIM - ROT_DIM,), F32)
    freq_head = jnp.concatenate([inv_freq, inv_freq, rest])
    sign_head = jnp.concatenate([-jnp.ones((half,), F32), jnp.ones((half,), F32), rest])
    reps = LANES // B_HEAD_DIM
    ang = positions.astype(F32).reshape(-1, 1) * jnp.tile(freq_head, reps)[None, :]
    return jnp.cos(ang), jnp.sin(ang) * jnp.tile(sign_head, reps)[None, :]


def kernel(x, positions, even_w_in, even_a_ln_g, even_a_ln_b, even_a_ws, even_a_bs, even_b_sinks, even_w_out, even_ln_g, even_ln_b, odd_w_in, odd_conv_w, odd_conv_b, odd_w_a, odd_b_a, odd_w_x, odd_b_x, odd_lam, odd_w_pool, odd_d_scale, odd_w_out, odd_ln_g, odd_ln_b):
    batch, seq, d = x.shape
    m = batch * seq
    xf = x.reshape(m, d)
    cos_t, sin_t = _rope_tables(positions)
    even_bst = jnp.swapaxes(even_a_bs, 1, 2)
    w_in = even_w_in[0].astype(BF16)
    for layer in range(DEPTH):
        j = layer // 2
        if layer % 2 == 0:
            h, w_out, w_in = _proj_in(xf, w_in, even_w_out, j, odd_w_in, j)
            xf = _even_layer(h, xf, even_b_sinks, cos_t, sin_t, even_a_ln_g, even_a_ln_b, even_a_ws,
                             even_bst, w_out, even_ln_g, even_ln_b, j, seq)
        else:
            w_next = even_w_in if layer + 1 < DEPTH else None
            h, w_out, w_in = _proj_in(xf, w_in, odd_w_out, j, w_next, j + 1)
            xf = _odd_layer(h, xf, odd_conv_w, odd_conv_b, odd_w_a, odd_w_x, odd_b_a, odd_b_x, odd_lam,
                            odd_w_pool, odd_d_scale, w_out, odd_ln_g, odd_ln_b, j, seq)
    return xf.reshape(batch, seq, d)
```

```python
from functools import partial

import jax
import jax.numpy as jnp
from jax import lax
from jax.experimental import pallas as pl
from jax.experimental.pallas import tpu as pltpu

D_MODEL = 2048
DEPTH = 4
A_WIDTH = 1024
A_GROUPS = 8
CHUNK = 128
B_HEAD_DIM = 64
B_Q_HEADS = 16
B_KV_HEADS = 2
B_WIDTH = 1024
B_KV_WIDTH = 128
WINDOW = 128
ROT_DIM = 16
ROPE_THETA = 500000.0
C_WIDTH = 1024
C_HEADS = 8
C_HEAD_DIM = 128
CONV_WIDTH = 4
LRU_C = 8.0
D_WIDTH = 1024
POOL_WINDOWS = (2, 4, 8, 16)
D_GROUP_DIM = 256
MIX_WIDTH = 2048
DN_ALPHA = (2 * DEPTH) ** 0.25
LN_EPS = 1e-5

LANES = 128
SUBLANES = 8
VMEM_LIMIT_BYTES = 56 * 1024 * 1024

PROJ_TM = 256
MIX_T = 256
OUT_BLOCK = 256
OUT_BLOCKS = D_MODEL // OUT_BLOCK
CONV_HALO = SUBLANES
POOL_HALO = 4 * SUBLANES

KV_BLOCK = 2 * B_KV_WIDTH
KV_COL = (3 * A_WIDTH + B_WIDTH) // KV_BLOCK
GATE_B_COL = KV_COL + 1

F32 = jnp.float32
BF16 = jnp.bfloat16


def _sigmoid(x):
    return 0.5 * jnp.tanh(0.5 * x) + 0.5


def _silu(x):
    hx = 0.5 * x
    return hx * jnp.tanh(hx) + hx


def _resident(shape):
    return pl.BlockSpec(shape, lambda *_: (0,) * len(shape), pipeline_mode=pl.Buffered(1))


def _whole(arr):
    return pl.BlockSpec(arr.shape, lambda *_: (0,) * arr.ndim)


def _layer_block(shape, layer):
    return pl.BlockSpec((None,) + shape, lambda *_: (layer,) + (0,) * len(shape))


def _compiler_params(*semantics):
    return pltpu.CompilerParams(dimension_semantics=semantics, vmem_limit_bytes=VMEM_LIMIT_BYTES)


def _proj_in_kernel(*refs, cast_next):
    if cast_next:
        x_ref, w_ref, wout_ref, wn_ref, o_ref, wout_bf_ref, wn_bf_ref = refs
        wn_bf_ref[...] = wn_ref[...].astype(BF16)
    else:
        x_ref, w_ref, wout_ref, o_ref, wout_bf_ref = refs
    for p in range(OUT_BLOCKS):
        wout_bf_ref[p] = wout_ref[:, p * OUT_BLOCK:(p + 1) * OUT_BLOCK].astype(BF16)
    o_ref[...] = jnp.dot(x_ref[...].astype(BF16), w_ref[...], preferred_element_type=F32)


def _proj_in(x, w, w_out, layer, w_next, layer_next):
    m, k = x.shape
    n = w.shape[1]
    steps = m // PROJ_TM

    def slab_specs(stacked, which):
        rows, cols = stacked.shape[1:]
        slab = rows // steps
        return (pl.BlockSpec((None, slab, cols), lambda i: (which, i, 0)),
                pl.BlockSpec((slab, cols), lambda i: (i, 0)),
                jax.ShapeDtypeStruct((rows, cols), BF16))

    k_out = w_out.shape[1]
    slab_out = k_out // steps
    casts = [(pl.BlockSpec((None, slab_out, D_MODEL), lambda i: (layer, i, 0)),
              pl.BlockSpec((OUT_BLOCKS, slab_out, OUT_BLOCK), lambda i: (0, i, 0)),
              jax.ShapeDtypeStruct((OUT_BLOCKS, k_out, OUT_BLOCK), BF16))]
    args = [x, w, w_out]
    if w_next is not None:
        casts.append(slab_specs(w_next, layer_next))
        args.append(w_next)
    outs = pl.pallas_call(
        partial(_proj_in_kernel, cast_next=w_next is not None),
        grid=(steps,),
        in_specs=[pl.BlockSpec((PROJ_TM, k), lambda i: (i, 0)), _resident((k, n))]
        + [c[0] for c in casts],
        out_specs=[pl.BlockSpec((PROJ_TM, n), lambda i: (i, 0))] + [c[1] for c in casts],
        out_shape=[jax.ShapeDtypeStruct((m, n), F32)] + [c[2] for c in casts],
        compiler_params=_compiler_params("parallel"),
        name="proj_in",
    )(*args)
    return outs if w_next is not None else (*outs, None)


class _Tail:
    def __init__(self, mix_prev, wout_ref, x_ref, z_ref, z_prev, g_ref, b_ref, layer, o_ref,
                 proj, ln):
        self.mix = mix_prev[...] if proj else None
        self.proj_refs = (wout_ref, x_ref, z_ref)
        self.ln_refs = (z_prev, g_ref, b_ref, layer, o_ref)
        self.blocks_left = list(range(OUT_BLOCKS)) if proj else []
        rows = z_ref.shape[0]
        self.halves_left = [slice(r * rows // 2, (r + 1) * rows // 2) for r in range(2)] if ln else []

    def emit(self, count=1):
        wout_ref, x_ref, z_ref = self.proj_refs
        for _ in range(min(count, len(self.blocks_left))):
            blk = self.blocks_left.pop(0)
            cols = slice(blk * OUT_BLOCK, (blk + 1) * OUT_BLOCK)
            z_ref[:, cols] = DN_ALPHA * x_ref[:, cols] + jnp.dot(
                self.mix, wout_ref[blk], preferred_element_type=F32)

    def ln_half(self):
        if self.halves_left:
            z_prev, g_ref, b_ref, layer, o_ref = self.ln_refs
            rows = self.halves_left.pop(0)
            z = z_prev[rows, :]
            mu = jnp.mean(z, axis=-1, keepdims=True)
            zc = z - mu
            var = jnp.mean(zc * zc, axis=-1, keepdims=True)
            o_ref[rows, :] = (zc * lax.rsqrt(var + LN_EPS) * g_ref[layer:layer + 1, :]
                              + b_ref[layer:layer + 1, :])

    def finish(self):
        self.ln_half()
        self.ln_half()
        self.emit(OUT_BLOCKS)


def _lagged_tiles(n_tiles):
    def clamp(i):
        return jnp.clip(i, 0, n_tiles - 1)

    return clamp, (lambda i: clamp(i - 1)), (lambda i: clamp(i - 2))


def _run_stages(step, n_tiles, body, mix_a, mix_b, z_a, z_b):
    def stage(pred, parity, **stages):
        bufs = (mix_a, mix_b, z_a, z_b) if parity == 0 else (mix_b, mix_a, z_b, z_a)

        @pl.when(pred)
        def _():
            body(*bufs, **stages)

    steady = (step >= 2) & (step < n_tiles)
    stage(step == 0, 0, mix=True, proj=False, ln=False)
    stage(step == 1, 1, mix=True, proj=True, ln=False)
    stage(steady & (step % 2 == 0), 0, mix=True, proj=True, ln=True)
    stage(steady & (step % 2 == 1), 1, mix=True, proj=True, ln=True)
    stage(step == n_tiles, n_tiles % 2, mix=False, proj=True, ln=True)
    stage(step == n_tiles + 1, (n_tiles + 1) % 2, mix=False, proj=False, ln=True)


def _tail_specs(x, w_out_bf, ln_g, ln_b, res, out):
    m, d = x.shape
    in_specs = [
        pl.BlockSpec((MIX_T, d), lambda i: (res(i), 0)),
        _resident(w_out_bf.shape),
        _whole(ln_g),
        _whole(ln_b),
    ]
    out_spec = pl.BlockSpec((MIX_T, d), lambda i: (out(i), 0))
    out_shape = jax.ShapeDtypeStruct((m, d), F32)
    scratch = [pltpu.VMEM((MIX_T, MIX_WIDTH), BF16), pltpu.VMEM((MIX_T, MIX_WIDTH), BF16),
               pltpu.VMEM((MIX_T, d), F32), pltpu.VMEM((MIX_T, d), F32)]
    return in_specs, out_spec, out_shape, scratch, [x, w_out_bf, ln_g, ln_b]


def _rope_tile(t, cos, sin, take_upper):
    upper = pltpu.roll(t, LANES - ROT_DIM // 2, 1)
    lower = pltpu.roll(t, ROT_DIM // 2, 1)
    return t * cos + jnp.where(take_upper, upper, lower) * sin


def _even_layer_kernel(*refs, layer, n_tiles, n_s):
    *io_refs, mix_a, mix_b, z_a, z_b = refs
    step = pl.program_id(0)
    seq_tile = jnp.minimum(step, n_tiles - 1) % n_s
    _run_stages(step, n_tiles, partial(_even_step, *io_refs, layer=layer, seq_tile=seq_tile),
                mix_a, mix_b, z_a, z_b)


def _even_step(sinks_ref, h_ref, kvp_ref, tabc_ref, tabp_ref, lng_ref, lnb_ref, ws_ref, bst_ref,
               x_ref, wout_ref, g_ref, b_ref, o_ref, mix_ref, mix_prev, z_ref, z_prev,
               *, layer, seq_tile, mix, proj, ln):
    u_ref, v_ref, ag_ref, q_ref = (h_ref.at[:, c * A_WIDTH:(c + 1) * A_WIDTH] for c in range(4))
    kvc_ref = h_ref.at[:, KV_COL * KV_BLOCK:(KV_COL + 1) * KV_BLOCK]
    bg_refs = [h_ref.at[:, (GATE_B_COL + c) * KV_BLOCK:(GATE_B_COL + c + 1) * KV_BLOCK]
               for c in range(B_WIDTH // KV_BLOCK)]
    cosc_ref, sinc_ref = tabc_ref.at[:, :LANES], tabc_ref.at[:, LANES:]
    cosp_ref, sinp_ref = tabp_ref.at[:, :LANES], tabp_ref.at[:, LANES:]
    t_rows = h_ref.shape[0]
    n_chunks = t_rows // CHUNK
    tail = _Tail(mix_prev, wout_ref, x_ref, z_ref, z_prev, g_ref, b_ref, layer, o_ref, proj, ln)
    if not mix:
        tail.finish()
        return
    tail.emit()

    v = v_ref[...]
    mu = jnp.mean(v, axis=-1, keepdims=True)
    vc = v - mu
    var = jnp.mean(vc * vc, axis=-1, keepdims=True)
    vln = (vc * lax.rsqrt(var + LN_EPS) * lng_ref[layer:layer + 1, :]
           + lnb_ref[layer:layer + 1, :]).astype(BF16)
    causal = (lax.broadcasted_iota(jnp.int32, (CHUNK, CHUNK), 0)
              >= lax.broadcasted_iota(jnp.int32, (CHUNK, CHUNK), 1))
    lane = lax.broadcasted_iota(jnp.int32, (1, LANES), 1)
    take_upper = (lane % B_HEAD_DIM) < (ROT_DIM // 2)
    low_half = lane < B_HEAD_DIM
    cosc = cosc_ref[...]
    sinc = sinc_ref[...]
    scale = B_HEAD_DIM ** -0.5
    q_tiles = []
    for g in range(A_GROUPS):
        cols = slice(g * CHUNK, (g + 1) * CHUNK)
        w = jnp.where(causal, ws_ref[g], 0.0).astype(BF16)
        rhs = jnp.concatenate(
            [vln[c * CHUNK:(c + 1) * CHUNK, cols] for c in range(n_chunks)], axis=1)
        mixed = jnp.dot(w, rhs, preferred_element_type=F32) + bst_ref[:, g:g + 1]
        for c in range(n_chunks):
            rows = slice(c * CHUNK, (c + 1) * CHUNK)
            gate = _silu(ag_ref[rows, cols])
            mix_ref[rows, cols] = (u_ref[rows, cols] * mixed[:, c * CHUNK:(c + 1) * CHUNK]
                                   * gate).astype(BF16)
        q_tiles.append(
            (_rope_tile(q_ref[:, cols], cosc, sinc, take_upper) * scale).astype(BF16))
        if g in (1, 5):
            tail.ln_half()
        if g in (2, 4, 6):
            tail.emit()

    k_all = jnp.concatenate(
        [_rope_tile(kvp_ref[:, :B_KV_WIDTH], cosp_ref[...], sinp_ref[...], take_upper),
         _rope_tile(kvc_ref[:, :B_KV_WIDTH], cosc, sinc, take_upper)], axis=0)
    k_swap = pltpu.roll(k_all, B_HEAD_DIM, 1)
    zero = jnp.zeros_like(k_all)
    k_lo = (jnp.where(low_half, k_all, zero).astype(BF16), jnp.where(low_half, k_swap, zero).astype(BF16))
    k_hi = (jnp.where(low_half, zero, k_swap).astype(BF16), jnp.where(low_half, zero, k_all).astype(BF16))
    v_all = jnp.concatenate([kvp_ref[:, B_KV_WIDTH:], kvc_ref[:, B_KV_WIDTH:]], axis=0)
    vt_all = v_all.T.astype(BF16)

    tiles_per_kv = B_WIDTH // LANES // B_KV_HEADS
    band_keys = 2 * WINDOW
    kj = lax.broadcasted_iota(jnp.int32, (band_keys, WINDOW), 0)
    qi = lax.broadcasted_iota(jnp.int32, (band_keys, WINDOW), 1)
    diff = qi + WINDOW - kj
    band = (diff >= 0) & (diff < WINDOW)
    first_key = jnp.where(seq_tile == 0, WINDOW, 0)
    neg_inf = jnp.full((band_keys, WINDOW), -jnp.inf, F32)
    mask_any = jnp.where(band, 0.0, neg_inf)
    mask_first = jnp.where(band & (kj >= first_key), 0.0, neg_inf)

    for n in range(t_rows // WINDOW):
        rows = slice(n * WINDOW, (n + 1) * WINDOW)
        band_rows = slice(n * WINDOW, (n + 2) * WINDOW)
        mask = jnp.concatenate([mask_first if n == 0 else mask_any] * tiles_per_kv, axis=1)
        for kv in range(B_KV_HEADS):
            tail.emit()
            q4 = jnp.concatenate(
                [q_tiles[kv * tiles_per_kv + j][rows] for j in range(tiles_per_kv)], axis=0)
            k_cat = jnp.concatenate([k_lo[kv][band_rows], k_hi[kv][band_rows]], axis=0)
            st = lax.dot_general(k_cat, q4, (((1,), (1,)), ((), ())), preferred_element_type=F32)
            vt = vt_all[kv * B_HEAD_DIM:(kv + 1) * B_HEAD_DIM, band_rows]
            o_halves = []
            for half in range(2):
                sh = st[half * band_keys:(half + 1) * band_keys] + mask
                sink = jnp.concatenate(
                    [jnp.full((1, WINDOW), sinks_ref[layer, 2 * (kv * tiles_per_kv + j) + half], F32)
                     for j in range(tiles_per_kv)], axis=1)
                m = jnp.maximum(jnp.max(sh, axis=0, keepdims=True), sink)
                p = jnp.exp(sh - m)
                l = jnp.sum(p, axis=0, keepdims=True) + jnp.exp(sink - m)
                o_half = jnp.dot(vt, p.astype(BF16), preferred_element_type=F32)
                o_halves.append(o_half * (1.0 / l))
            ot = jnp.concatenate(o_halves, axis=0)
            for j in range(tiles_per_kv):
                jj = kv * tiles_per_kv + j
                gate_ref = bg_refs[jj // 2]
                gate_cols = slice((jj % 2) * LANES, (jj % 2 + 1) * LANES)
                out_cols = slice(A_WIDTH + jj * LANES, A_WIDTH + (jj + 1) * LANES)
                o_tile = ot[:, j * WINDOW:(j + 1) * WINDOW].T
                mix_ref[rows, out_cols] = (o_tile * _silu(gate_ref[rows, gate_cols])).astype(BF16)

    tail.finish()


def _even_layer(h, x, sinks, rope_tab, a_ln_g, a_ln_b, ws, bst, w_out_bf, ln_g, ln_b, layer, seq):
    m = h.shape[0]
    t = MIX_T
    n_tiles = m // t
    n_s = seq // t
    blocks_per_tile = t // WINDOW
    cur, res, out = _lagged_tiles(n_tiles)

    def prev(i):
        return jnp.maximum(cur(i) * blocks_per_tile - 1, 0)

    tail_in, out_specs, out_shape, tail_scratch, tail_args = _tail_specs(x, w_out_bf, ln_g, ln_b, res, out)
    return pl.pallas_call(
        partial(_even_layer_kernel, layer=layer, n_tiles=n_tiles, n_s=n_s),
        grid=(n_tiles + 2,),
        in_specs=[
            pl.BlockSpec(memory_space=pltpu.SMEM),
            pl.BlockSpec((t, h.shape[1]), lambda i: (cur(i), 0)),
            pl.BlockSpec((WINDOW, KV_BLOCK), lambda i: (prev(i), KV_COL)),
            pl.BlockSpec((t, 2 * LANES), lambda i: (cur(i), 0)),
            pl.BlockSpec((WINDOW, 2 * LANES), lambda i: (prev(i), 0)),
            _whole(a_ln_g),
            _whole(a_ln_b),
            _layer_block((A_GROUPS, CHUNK, CHUNK), layer),
            _layer_block((CHUNK, A_GROUPS), layer),
        ] + tail_in,
        out_specs=out_specs,
        out_shape=out_shape,
        scratch_shapes=tail_scratch,
        compiler_params=_compiler_params("arbitrary"),
        name="even_layer",
    )(sinks, h, h, rope_tab, rope_tab, a_ln_g, a_ln_b, ws, bst, *tail_args)


def _odd_layer_kernel(*refs, layer, n_tiles, n_s):
    *io_refs, mix_a, mix_b, z_a, z_b = refs[:19]
    (ext_c, ext_d, lvl2, lvl4, hstate, sa, sb, sh, wax_bf, wpool_bf) = scratch = refs[19:]
    wa_ref, wx_ref, wpool_ref = io_refs[3], io_refs[4], io_refs[8]
    t_rows = io_refs[0].shape[0]
    step = pl.program_id(0)
    s_idx = jnp.minimum(step, n_tiles - 1) % n_s

    @pl.when(s_idx == 0)
    def _():
        ext_c[:, 0:CONV_HALO, :] = jnp.zeros((C_HEADS, CONV_HALO, LANES), F32)
        ext_d[:, 0:POOL_HALO, :] = jnp.zeros((D_WIDTH // LANES, POOL_HALO, LANES), F32)
        hstate[...] = jnp.zeros_like(hstate)
        wax_bf[:, :, :C_HEAD_DIM] = wa_ref[...].astype(BF16)
        wax_bf[:, :, C_HEAD_DIM:] = wx_ref[...].astype(BF16)
        wpool_bf[...] = wpool_ref[...].astype(BF16)

    @pl.when(s_idx > 0)
    def _():
        ext_c[:, 0:CONV_HALO, :] = ext_c[:, t_rows:t_rows + CONV_HALO, :]
        ext_d[:, 0:POOL_HALO, :] = ext_d[:, t_rows:t_rows + POOL_HALO, :]

    _run_stages(step, n_tiles, partial(_odd_step, *io_refs, *scratch, layer=layer, s_idx=s_idx),
                mix_a, mix_b, z_a, z_b)


def _odd_step(h_ref, convw_ref, convb_ref, wa_ref, wx_ref, ba_ref, bx_ref,
              lam_ref, wpool_ref, dscale_ref, x_ref, wout_ref, g_ref, b_ref, o_ref,
              ext_c, ext_d, lvl2, lvl4, hstate, sa, sb, sh, wax_bf, wpool_bf,
              mix_ref, mix_prev, z_ref, z_prev, *, layer, s_idx, mix, proj, ln):
    xc_ref, cg_ref, xd_ref, dg_ref = (h_ref.at[:, c * C_WIDTH:(c + 1) * C_WIDTH] for c in range(4))
    t_rows = h_ref.shape[0]
    row = slice(layer, layer + 1)
    n_groups = t_rows // SUBLANES
    tail = _Tail(mix_prev, wout_ref, x_ref, z_ref, z_prev, g_ref, b_ref, layer, o_ref, proj, ln)
    if not mix:
        tail.finish()
        return
    tail.emit()

    z = -lam_ref[row, :]
    softplus = jnp.maximum(z, 0.0) + jnp.log1p(jnp.exp(-jnp.abs(z)))
    for hd in range(C_HEADS):
        cols = slice(hd * C_HEAD_DIM, (hd + 1) * C_HEAD_DIM)
        ext_c[hd, CONV_HALO:, :] = xc_ref[:, cols]
        xconv = convb_ref[row, cols]
        for j in range(CONV_WIDTH):
            lo = CONV_HALO - (CONV_WIDTH - 1) + j
            xconv = xconv + ext_c[hd, lo:lo + t_rows, :] * convw_ref[j:j + 1, cols]
        ri = jnp.dot(xconv.astype(BF16), wax_bf[hd], preferred_element_type=F32)
        r = _sigmoid(ri[:, :C_HEAD_DIM] + ba_ref[row, cols])
        i = _sigmoid(ri[:, C_HEAD_DIM:] + bx_ref[row, cols])
        log_a = -LRU_C * r * softplus[:, cols]
        a = jnp.exp(log_a)
        m2 = -jnp.tanh(log_a) * (a * a + 1.0)
        mult = jnp.where(m2 == 0.0, 0.0, m2 * lax.rsqrt(m2))
        b = mult * (i * xconv)
        for g in range(n_groups):
            base = (g * C_HEADS + hd) * SUBLANES
            sa[base:base + SUBLANES, :] = a[g * SUBLANES:(g + 1) * SUBLANES]
            sb[base:base + SUBLANES, :] = b[g * SUBLANES:(g + 1) * SUBLANES]
        if hd in (2, 5):
            tail.ln_half()
        if hd in (1, 3, 4, 6, 7):
            tail.emit()

    h = hstate[...]
    for t in range(t_rows):
        g, s = divmod(t, SUBLANES)
        at_t = pl.ds(g * C_HEADS * SUBLANES + s, C_HEADS, stride=SUBLANES)
        h = sa[at_t, :] * h + sb[at_t, :]
        sh[at_t, :] = h
    hstate[...] = h
    tail.emit()
    for hd in range(C_HEADS):
        cols = slice(hd * C_HEAD_DIM, (hd + 1) * C_HEAD_DIM)
        hcol = jnp.concatenate(
            [sh[(g * C_HEADS + hd) * SUBLANES:(g * C_HEADS + hd + 1) * SUBLANES, :]
             for g in range(n_groups)], axis=0)
        mix_ref[:, cols] = (hcol * _silu(cg_ref[:, cols])).astype(BF16)

    n_ext = POOL_HALO + t_rows
    tiles_per_group = D_GROUP_DIM // LANES
    pos1 = s_idx * t_rows + lax.broadcasted_iota(jnp.int32, (t_rows, LANES), 0) + 1
    for g, w in enumerate(POOL_WINDOWS):
        if g == 2:
            tail.emit()
        inv = 1.0 / jnp.minimum(pos1, w).astype(F32)
        pooled = []
        for k in range(tiles_per_group):
            tile = g * tiles_per_group + k
            cols = slice(tile * LANES, (tile + 1) * LANES)
            ext_d[tile, POOL_HALO:, :] = xd_ref[:, cols]
            lo = SUBLANES
            win = ext_d[tile, lo:n_ext, :] + ext_d[tile, lo - 1:n_ext - 1, :]
            if w >= 4:
                lvl2[k, lo:n_ext, :] = win
                lo += SUBLANES
                win = lvl2[k, lo:n_ext, :] + lvl2[k, lo - 2:n_ext - 2, :]
            if w >= 8:
                lvl4[k, lo:n_ext, :] = win
                lo += SUBLANES
                win = lvl4[k, lo:n_ext, :] + lvl4[k, lo - 4:n_ext - 4, :]
            if w >= 16:
                lo += SUBLANES
                win = win[SUBLANES:] + win[:-SUBLANES]
            pooled.append((win[POOL_HALO - lo:] * inv - xd_ref[:, cols]).astype(BF16))
        cols = slice(g * D_GROUP_DIM, (g + 1) * D_GROUP_DIM)
        mixed = jnp.dot(jnp.concatenate(pooled, axis=1), wpool_bf[g], preferred_element_type=F32)
        out_cols = slice(C_WIDTH + g * D_GROUP_DIM, C_WIDTH + (g + 1) * D_GROUP_DIM)
        mix_ref[:, out_cols] = (mixed * dscale_ref[row, cols] * _silu(dg_ref[:, cols])).astype(BF16)

    tail.finish()


def _odd_layer(h, x, conv_w, conv_b, w_a, w_x, b_a, b_x, lam, w_pool, d_scale, w_out_bf, ln_g, ln_b,
               layer, seq):
    m = h.shape[0]
    t = MIX_T
    n_tiles = m // t
    n_s = seq // t
    cur, res, out = _lagged_tiles(n_tiles)
    n_pool = len(POOL_WINDOWS)
    tail_in, out_specs, out_shape, tail_scratch, tail_args = _tail_specs(x, w_out_bf, ln_g, ln_b, res, out)
    return pl.pallas_call(
        partial(_odd_layer_kernel, layer=layer, n_tiles=n_tiles, n_s=n_s),
        grid=(n_tiles + 2,),
        in_specs=[
            pl.BlockSpec((t, h.shape[1]), lambda i: (cur(i), 0)),
            _layer_block((CONV_WIDTH, C_WIDTH), layer),
            _whole(conv_b),
            _layer_block((C_HEADS, C_HEAD_DIM, C_HEAD_DIM), layer),
            _layer_block((C_HEADS, C_HEAD_DIM, C_HEAD_DIM), layer),
            _whole(b_a), _whole(b_x), _whole(lam),
            _layer_block((n_pool, D_GROUP_DIM, D_GROUP_DIM), layer),
            _whole(d_scale),
        ] + tail_in,
        out_specs=out_specs,
        out_shape=out_shape,
        scratch_shapes=tail_scratch + [
            pltpu.VMEM((C_WIDTH // LANES, CONV_HALO + t, LANES), F32),
            pltpu.VMEM((D_WIDTH // LANES, POOL_HALO + t, LANES), F32),
            pltpu.VMEM((D_GROUP_DIM // LANES, POOL_HALO + t, LANES), F32),
            pltpu.VMEM((D_GROUP_DIM // LANES, POOL_HALO + t, LANES), F32),
            pltpu.VMEM((SUBLANES, LANES), F32),
            pltpu.VMEM((t * C_HEADS, LANES), F32),
            pltpu.VMEM((t * C_HEADS, LANES), F32),
            pltpu.VMEM((t * C_HEADS, LANES), F32),
            pltpu.VMEM((C_HEADS, C_HEAD_DIM, 2 * C_HEAD_DIM), BF16),
            pltpu.VMEM((n_pool, D_GROUP_DIM, D_GROUP_DIM), BF16),
        ],
        compiler_params=_compiler_params("arbitrary"),
        name="odd_layer",
    )(h, conv_w, conv_b, w_a, w_x, b_a, b_x, lam, w_pool, d_scale, *tail_args)


def _rope_tables(positions):
    half = ROT_DIM // 2
    inv_freq = ROPE_THETA ** (-jnp.arange(0, ROT_DIM, 2, dtype=F32) / ROT_DIM)
    rest = jnp.zeros((B_HEAD_DIM - ROT_DIM,), F32)
    freq_head = jnp.concatenate([inv_freq, inv_freq, rest])
    sign_head = jnp.concatenate([-jnp.ones((half,), F32), jnp.ones((half,), F32), rest])
    reps = LANES // B_HEAD_DIM
    ang = positions.astype(F32).reshape(-1, 1) * jnp.tile(freq_head, reps)[None, :]
    return jnp.concatenate([jnp.cos(ang), jnp.sin(ang) * jnp.tile(sign_head, reps)[None, :]], axis=1)


def kernel(x, positions, even_w_in, even_a_ln_g, even_a_ln_b, even_a_ws, even_a_bs, even_b_sinks, even_w_out, even_ln_g, even_ln_b, odd_w_in, odd_conv_w, odd_conv_b, odd_w_a, odd_b_a, odd_w_x, odd_b_x, odd_lam, odd_w_pool, odd_d_scale, odd_w_out, odd_ln_g, odd_ln_b):
    batch, seq, d = x.shape
    m = batch * seq
    xf = x.reshape(m, d)
    rope_tab = _rope_tables(positions)
    even_bst = jnp.swapaxes(even_a_bs, 1, 2)
    w_in = even_w_in[0].astype(BF16)
    for layer in range(DEPTH):
        j = layer // 2
        if layer % 2 == 0:
            h, w_out, w_in = _proj_in(xf, w_in, even_w_out, j, odd_w_in, j)
            xf = _even_layer(h, xf, even_b_sinks, rope_tab, even_a_ln_g, even_a_ln_b, even_a_ws,
                             even_bst, w_out, even_ln_g, even_ln_b, j, seq)
        else:
            w_next = even_w_in if layer + 1 < DEPTH else None
            h, w_out, w_in = _proj_in(xf, w_in, odd_w_out, j, w_next, j + 1)
            xf = _odd_layer(h, xf, odd_conv_w, odd_conv_b, odd_w_a, odd_w_x, odd_b_a, odd_b_x, odd_lam,
                            odd_w_pool, odd_d_scale, w_out, odd_ln_g, odd_ln_b, j, seq)
    return xf.reshape(batch, seq, d)
```

```python
from functools import partial

import jax
import jax.numpy as jnp
from jax import lax
from jax.experimental import pallas as pl
from jax.experimental.pallas import tpu as pltpu

D_MODEL = 2048
DEPTH = 4
A_WIDTH = 1024
A_GROUPS = 8
CHUNK = 128
B_HEAD_DIM = 64
B_Q_HEADS = 16
B_KV_HEADS = 2
B_WIDTH = 1024
B_KV_WIDTH = 128
WINDOW = 128
ROT_DIM = 16
ROPE_THETA = 500000.0
C_WIDTH = 1024
C_HEADS = 8
C_HEAD_DIM = 128
CONV_WIDTH = 4
LRU_C = 8.0
D_WIDTH = 1024
POOL_WINDOWS = (2, 4, 8, 16)
D_GROUP_DIM = 256
MIX_WIDTH = 2048
DN_ALPHA = (2 * DEPTH) ** 0.25
LN_EPS = 1e-5

LANES = 128
SUBLANES = 8
VMEM_LIMIT_BYTES = 56 * 1024 * 1024

PROJ_TM = 256
MIX_T = 256
OUT_BLOCK = 256
OUT_BLOCKS = D_MODEL // OUT_BLOCK
W_CHUNK = 256
CONV_HALO = SUBLANES
POOL_HALO = 4 * SUBLANES

KV_BLOCK = 2 * B_KV_WIDTH
KV_COL = (3 * A_WIDTH + B_WIDTH) // KV_BLOCK
GATE_B_COL = KV_COL + 1

F32 = jnp.float32
BF16 = jnp.bfloat16


def _sigmoid(x):
    return 0.5 * jnp.tanh(0.5 * x) + 0.5


def _silu(x):
    hx = 0.5 * x
    return hx * jnp.tanh(hx) + hx


def _resident(shape):
    return pl.BlockSpec(shape, lambda *_: (0,) * len(shape), pipeline_mode=pl.Buffered(1))


def _whole(arr):
    return pl.BlockSpec(arr.shape, lambda *_: (0,) * arr.ndim)


def _layer_block(shape, layer):
    return pl.BlockSpec((None,) + shape, lambda *_: (layer,) + (0,) * len(shape))


def _compiler_params(*semantics):
    return pltpu.CompilerParams(dimension_semantics=semantics, vmem_limit_bytes=VMEM_LIMIT_BYTES)


def _proj_in_kernel(*refs, cast_next, w_layer):
    n_in = 4 if cast_next else 3
    x_ref, w_hbm, wout_ref = refs[:3]
    o_ref, wout_bf_ref = refs[n_in:n_in + 2]
    if cast_next:
        wn_ref, wn_bf_ref = refs[3], refs[n_in + 2]
        wn_bf_ref[...] = wn_ref[...].astype(BF16)
    staged = w_hbm.dtype != BF16
    w_vmem, *stage, sems = refs[2 * n_in - 1:]
    for p in range(OUT_BLOCKS):
        wout_bf_ref[p] = wout_ref[:, p * OUT_BLOCK:(p + 1) * OUT_BLOCK].astype(BF16)

    n_chunks = w_vmem.shape[0] // W_CHUNK
    w_src = w_hbm if w_layer is None else w_hbm.at[w_layer]

    def chunk_copy(c):
        rows = pl.ds(c * W_CHUNK, W_CHUNK)
        if staged:
            return pltpu.make_async_copy(w_src.at[rows, :], stage[0].at[c % 2], sems.at[c % 2])
        return pltpu.make_async_copy(w_src.at[rows, :], w_vmem.at[rows, :], sems.at[c])

    step = pl.program_id(0)

    @pl.when(step == 0)
    def _():
        xb = x_ref[...].astype(BF16)
        for c in range(2 if staged else n_chunks):
            chunk_copy(c).start()
        for c in range(n_chunks):
            rows = slice(c * W_CHUNK, (c + 1) * W_CHUNK)
            chunk_copy(c).wait()
            if staged:
                w_vmem[rows, :] = stage[0][c % 2].astype(BF16)
                if c + 2 < n_chunks:
                    chunk_copy(c + 2).start()
            part = jnp.dot(xb[:, rows], w_vmem[rows, :], preferred_element_type=F32)
            if c == 0:
                o_ref[...] = part
            else:
                o_ref[...] += part

    @pl.when(step > 0)
    def _():
        o_ref[...] = jnp.dot(x_ref[...].astype(BF16), w_vmem[...], preferred_element_type=F32)


def _proj_in(x, w, w_layer, w_out, layer, w_next, layer_next):
    m, k = x.shape
    n = w.shape[-1]
    steps = m // PROJ_TM

    def slab_specs(stacked, which):
        rows, cols = stacked.shape[1:]
        slab = rows // steps
        return (pl.BlockSpec((None, slab, cols), lambda i: (which, i, 0)),
                pl.BlockSpec((slab, cols), lambda i: (i, 0)),
                jax.ShapeDtypeStruct((rows, cols), BF16))

    k_out = w_out.shape[1]
    slab_out = k_out // steps
    casts = [(pl.BlockSpec((None, slab_out, D_MODEL), lambda i: (layer, i, 0)),
              pl.BlockSpec((OUT_BLOCKS, slab_out, OUT_BLOCK), lambda i: (0, i, 0)),
              jax.ShapeDtypeStruct((OUT_BLOCKS, k_out, OUT_BLOCK), BF16))]
    args = [x, w, w_out]
    if w_next is not None:
        casts.append(slab_specs(w_next, layer_next))
        args.append(w_next)
    staged = w.dtype != BF16
    scratch = [pltpu.VMEM((k, n), BF16)]
    if staged:
        scratch.append(pltpu.VMEM((2, W_CHUNK, n), w.dtype))
    scratch.append(pltpu.SemaphoreType.DMA((2 if staged else k // W_CHUNK,)))
    outs = pl.pallas_call(
        partial(_proj_in_kernel, cast_next=w_next is not None, w_layer=w_layer),
        grid=(steps,),
        in_specs=[pl.BlockSpec((PROJ_TM, k), lambda i: (i, 0)), pl.BlockSpec(memory_space=pl.ANY)]
        + [c[0] for c in casts],
        out_specs=[pl.BlockSpec((PROJ_TM, n), lambda i: (i, 0))] + [c[1] for c in casts],
        out_shape=[jax.ShapeDtypeStruct((m, n), F32)] + [c[2] for c in casts],
        scratch_shapes=scratch,
        compiler_params=_compiler_params("arbitrary"),
        name="proj_in",
    )(*args)
    return outs if w_next is not None else (*outs, None)


class _Tail:
    def __init__(self, mix_prev, wout_ref, x_ref, z_ref, z_prev, g_ref, b_ref, layer, o_ref,
                 proj, ln):
        self.mix = mix_prev[...] if proj else None
        self.proj_refs = (wout_ref, x_ref, z_ref)
        self.ln_refs = (z_prev, g_ref, b_ref, layer, o_ref)
        self.blocks_left = list(range(OUT_BLOCKS)) if proj else []
        rows = z_ref.shape[0]
        self.halves_left = [slice(r * rows // 2, (r + 1) * rows // 2) for r in range(2)] if ln else []

    def emit(self, count=1):
        wout_ref, x_ref, z_ref = self.proj_refs
        for _ in range(min(count, len(self.blocks_left))):
            blk = self.blocks_left.pop(0)
            cols = slice(blk * OUT_BLOCK, (blk + 1) * OUT_BLOCK)
            z_ref[:, cols] = DN_ALPHA * x_ref[:, cols] + jnp.dot(
                self.mix, wout_ref[blk], preferred_element_type=F32)

    def ln_half(self):
        if self.halves_left:
            z_prev, g_ref, b_ref, layer, o_ref = self.ln_refs
            rows = self.halves_left.pop(0)
            z = z_prev[rows, :]
            mu = jnp.mean(z, axis=-1, keepdims=True)
            zc = z - mu
            var = jnp.mean(zc * zc, axis=-1, keepdims=True)
            o_ref[rows, :] = (zc * lax.rsqrt(var + LN_EPS) * g_ref[layer:layer + 1, :]
                              + b_ref[layer:layer + 1, :])

    def finish(self):
        self.ln_half()
        self.ln_half()
        self.emit(OUT_BLOCKS)


def _lagged_tiles(n_tiles):
    def clamp(i):
        return jnp.clip(i, 0, n_tiles - 1)

    return clamp, (lambda i: clamp(i - 1)), (lambda i: clamp(i - 2))


def _run_stages(step, n_tiles, body, mix_a, mix_b, z_a, z_b):
    def stage(pred, parity, **stages):
        bufs = (mix_a, mix_b, z_a, z_b) if parity == 0 else (mix_b, mix_a, z_b, z_a)

        @pl.when(pred)
        def _():
            body(*bufs, **stages)

    steady = (step >= 2) & (step < n_tiles)
    stage(step == 0, 0, mix=True, proj=False, ln=False)
    stage(step == 1, 1, mix=True, proj=True, ln=False)
    stage(steady & (step % 2 == 0), 0, mix=True, proj=True, ln=True)
    stage(steady & (step % 2 == 1), 1, mix=True, proj=True, ln=True)
    stage(step == n_tiles, n_tiles % 2, mix=False, proj=True, ln=True)
    stage(step == n_tiles + 1, (n_tiles + 1) % 2, mix=False, proj=False, ln=True)


def _tail_specs(x, w_out_bf, ln_g, ln_b, res, out):
    m, d = x.shape
    in_specs = [
        pl.BlockSpec((MIX_T, d), lambda i: (res(i), 0)),
        _resident(w_out_bf.shape),
        _whole(ln_g),
        _whole(ln_b),
    ]
    out_spec = pl.BlockSpec((MIX_T, d), lambda i: (out(i), 0))
    out_shape = jax.ShapeDtypeStruct((m, d), F32)
    scratch = [pltpu.VMEM((MIX_T, MIX_WIDTH), BF16), pltpu.VMEM((MIX_T, MIX_WIDTH), BF16),
               pltpu.VMEM((MIX_T, d), F32), pltpu.VMEM((MIX_T, d), F32)]
    return in_specs, out_spec, out_shape, scratch, [x, w_out_bf, ln_g, ln_b]


def _rope_tile(t, cos, sin, take_upper):
    upper = pltpu.roll(t, LANES - ROT_DIM // 2, 1)
    lower = pltpu.roll(t, ROT_DIM // 2, 1)
    return t * cos + jnp.where(take_upper, upper, lower) * sin


def _even_layer_kernel(*refs, layer, n_tiles, n_s):
    *io_refs, mix_a, mix_b, z_a, z_b = refs
    step = pl.program_id(0)
    seq_tile = jnp.minimum(step, n_tiles - 1) % n_s
    _run_stages(step, n_tiles, partial(_even_step, *io_refs, layer=layer, seq_tile=seq_tile),
                mix_a, mix_b, z_a, z_b)


def _even_step(sinks_ref, h_ref, kvp_ref, tabc_ref, tabp_ref, lng_ref, lnb_ref, ws_ref, bst_ref,
               x_ref, wout_ref, g_ref, b_ref, o_ref, mix_ref, mix_prev, z_ref, z_prev,
               *, layer, seq_tile, mix, proj, ln):
    u_ref, v_ref, ag_ref, q_ref = (h_ref.at[:, c * A_WIDTH:(c + 1) * A_WIDTH] for c in range(4))
    kvc_ref = h_ref.at[:, KV_COL * KV_BLOCK:(KV_COL + 1) * KV_BLOCK]
    bg_refs = [h_ref.at[:, (GATE_B_COL + c) * KV_BLOCK:(GATE_B_COL + c + 1) * KV_BLOCK]
               for c in range(B_WIDTH // KV_BLOCK)]
    cosc_ref, sinc_ref = tabc_ref.at[:, :LANES], tabc_ref.at[:, LANES:]
    cosp_ref, sinp_ref = tabp_ref.at[:, :LANES], tabp_ref.at[:, LANES:]
    t_rows = h_ref.shape[0]
    n_chunks = t_rows // CHUNK
    tail = _Tail(mix_prev, wout_ref, x_ref, z_ref, z_prev, g_ref, b_ref, layer, o_ref, proj, ln)
    if not mix:
        tail.finish()
        return
    tail.emit()

    v = v_ref[...]
    mu = jnp.mean(v, axis=-1, keepdims=True)
    vc = v - mu
    var = jnp.mean(vc * vc, axis=-1, keepdims=True)
    vln = (vc * lax.rsqrt(var + LN_EPS) * lng_ref[layer:layer + 1, :]
           + lnb_ref[layer:layer + 1, :]).astype(BF16)
    causal = (lax.broadcasted_iota(jnp.int32, (CHUNK, CHUNK), 0)
              >= lax.broadcasted_iota(jnp.int32, (CHUNK, CHUNK), 1))
    lane = lax.broadcasted_iota(jnp.int32, (1, LANES), 1)
    take_upper = (lane % B_HEAD_DIM) < (ROT_DIM // 2)
    low_half = lane < B_HEAD_DIM
    cosc = cosc_ref[...]
    sinc = sinc_ref[...]
    scale = B_HEAD_DIM ** -0.5
    q_tiles = []
    for g in range(A_GROUPS):
        cols = slice(g * CHUNK, (g + 1) * CHUNK)
        w = jnp.where(causal, ws_ref[g], 0.0).astype(BF16)
        rhs = jnp.concatenate(
            [vln[c * CHUNK:(c + 1) * CHUNK, cols] for c in range(n_chunks)], axis=1)
        mixed = jnp.dot(w, rhs, preferred_element_type=F32) + bst_ref[:, g:g + 1]
        for c in range(n_chunks):
            rows = slice(c * CHUNK, (c + 1) * CHUNK)
            gate = _silu(ag_ref[rows, cols])
            mix_ref[rows, cols] = (u_ref[rows, cols] * mixed[:, c * CHUNK:(c + 1) * CHUNK]
                                   * gate).astype(BF16)
        q_tiles.append(
            (_rope_tile(q_ref[:, cols], cosc, sinc, take_upper) * scale).astype(BF16))
        if g in (1, 5):
            tail.ln_half()
        if g in (2, 4, 6):
            tail.emit()

    k_all = jnp.concatenate(
        [_rope_tile(kvp_ref[:, :B_KV_WIDTH], cosp_ref[...], sinp_ref[...], take_upper),
         _rope_tile(kvc_ref[:, :B_KV_WIDTH], cosc, sinc, take_upper)], axis=0)
    k_swap = pltpu.roll(k_all, B_HEAD_DIM, 1)
    zero = jnp.zeros_like(k_all)
    k_lo = (jnp.where(low_half, k_all, zero).astype(BF16), jnp.where(low_half, k_swap, zero).astype(BF16))
    k_hi = (jnp.where(low_half, zero, k_swap).astype(BF16), jnp.where(low_half, zero, k_all).astype(BF16))
    v_all = jnp.concatenate([kvp_ref[:, B_KV_WIDTH:], kvc_ref[:, B_KV_WIDTH:]], axis=0)
    vt_all = v_all.T.astype(BF16)

    tiles_per_kv = B_WIDTH // LANES // B_KV_HEADS
    band_keys = 2 * WINDOW
    kj = lax.broadcasted_iota(jnp.int32, (band_keys, WINDOW), 0)
    qi = lax.broadcasted_iota(jnp.int32, (band_keys, WINDOW), 1)
    diff = qi + WINDOW - kj
    band = (diff >= 0) & (diff < WINDOW)
    first_key = jnp.where(seq_tile == 0, WINDOW, 0)
    neg_inf = jnp.full((band_keys, WINDOW), -jnp.inf, F32)
    mask_any = jnp.where(band, 0.0, neg_inf)
    mask_first = jnp.where(band & (kj >= first_key), 0.0, neg_inf)

    for n in range(t_rows // WINDOW):
        rows = slice(n * WINDOW, (n + 1) * WINDOW)
        band_rows = slice(n * WINDOW, (n + 2) * WINDOW)
        mask = jnp.concatenate([mask_first if n == 0 else mask_any] * tiles_per_kv, axis=1)
        for kv in range(B_KV_HEADS):
            tail.emit()
            q4 = jnp.concatenate(
                [q_tiles[kv * tiles_per_kv + j][rows] for j in range(tiles_per_kv)], axis=0)
            k_cat = jnp.concatenate([k_lo[kv][band_rows], k_hi[kv][band_rows]], axis=0)
            st = lax.dot_general(k_cat, q4, (((1,), (1,)), ((), ())), preferred_element_type=F32)
            vt = vt_all[kv * B_HEAD_DIM:(kv + 1) * B_HEAD_DIM, band_rows]
            o_halves = []
            for half in range(2):
                sh = st[half * band_keys:(half + 1) * band_keys] + mask
                sink = jnp.concatenate(
                    [jnp.full((1, WINDOW), sinks_ref[layer, 2 * (kv * tiles_per_kv + j) + half], F32)
                     for j in range(tiles_per_kv)], axis=1)
                m = jnp.maximum(jnp.max(sh, axis=0, keepdims=True), sink)
                p = jnp.exp(sh - m)
                l = jnp.sum(p, axis=0, keepdims=True) + jnp.exp(sink - m)
                o_half = jnp.dot(vt, p.astype(BF16), preferred_element_type=F32)
                o_halves.append(o_half * (1.0 / l))
            ot = jnp.concatenate(o_halves, axis=0)
            for j in range(tiles_per_kv):
                jj = kv * tiles_per_kv + j
                gate_ref = bg_refs[jj // 2]
                gate_cols = slice((jj % 2) * LANES, (jj % 2 + 1) * LANES)
                out_cols = slice(A_WIDTH + jj * LANES, A_WIDTH + (jj + 1) * LANES)
                o_tile = ot[:, j * WINDOW:(j + 1) * WINDOW].T
                mix_ref[rows, out_cols] = (o_tile * _silu(gate_ref[rows, gate_cols])).astype(BF16)

    tail.finish()


def _even_layer(h, x, sinks, rope_tab, a_ln_g, a_ln_b, ws, bst, w_out_bf, ln_g, ln_b, layer, seq):
    m = h.shape[0]
    t = MIX_T
    n_tiles = m // t
    n_s = seq // t
    blocks_per_tile = t // WINDOW
    cur, res, out = _lagged_tiles(n_tiles)

    def prev(i):
        return jnp.maximum(cur(i) * blocks_per_tile - 1, 0)

    tail_in, out_specs, out_shape, tail_scratch, tail_args = _tail_specs(x, w_out_bf, ln_g, ln_b, res, out)
    return pl.pallas_call(
        partial(_even_layer_kernel, layer=layer, n_tiles=n_tiles, n_s=n_s),
        grid=(n_tiles + 2,),
        in_specs=[
            pl.BlockSpec(memory_space=pltpu.SMEM),
            pl.BlockSpec((t, h.shape[1]), lambda i: (cur(i), 0)),
            pl.BlockSpec((WINDOW, KV_BLOCK), lambda i: (prev(i), KV_COL)),
            pl.BlockSpec((t, 2 * LANES), lambda i: (cur(i), 0)),
            pl.BlockSpec((WINDOW, 2 * LANES), lambda i: (prev(i), 0)),
            _whole(a_ln_g),
            _whole(a_ln_b),
            _layer_block((A_GROUPS, CHUNK, CHUNK), layer),
            _layer_block((CHUNK, A_GROUPS), layer),
        ] + tail_in,
        out_specs=out_specs,
        out_shape=out_shape,
        scratch_shapes=tail_scratch,
        compiler_params=_compiler_params("arbitrary"),
        name="even_layer",
    )(sinks, h, h, rope_tab, rope_tab, a_ln_g, a_ln_b, ws, bst, *tail_args)


def _odd_layer_kernel(*refs, layer, n_tiles, n_s):
    *io_refs, mix_a, mix_b, z_a, z_b = refs[:19]
    (ext_c, ext_d, lvl2, lvl4, hstate, sa, sb, sh, wax_bf, wpool_bf) = scratch = refs[19:]
    wa_ref, wx_ref, wpool_ref = io_refs[3], io_refs[4], io_refs[8]
    t_rows = io_refs[0].shape[0]
    step = pl.program_id(0)
    s_idx = jnp.minimum(step, n_tiles - 1) % n_s

    @pl.when(s_idx == 0)
    def _():
        ext_c[:, 0:CONV_HALO, :] = jnp.zeros((C_HEADS, CONV_HALO, LANES), F32)
        ext_d[:, 0:POOL_HALO, :] = jnp.zeros((D_WIDTH // LANES, POOL_HALO, LANES), F32)
        hstate[...] = jnp.zeros_like(hstate)
        wax_bf[:, :, :C_HEAD_DIM] = wa_ref[...].astype(BF16)
        wax_bf[:, :, C_HEAD_DIM:] = wx_ref[...].astype(BF16)
        wpool_bf[...] = wpool_ref[...].astype(BF16)

    @pl.when(s_idx > 0)
    def _():
        ext_c[:, 0:CONV_HALO, :] = ext_c[:, t_rows:t_rows + CONV_HALO, :]
        ext_d[:, 0:POOL_HALO, :] = ext_d[:, t_rows:t_rows + POOL_HALO, :]

    _run_stages(step, n_tiles, partial(_odd_step, *io_refs, *scratch, layer=layer, s_idx=s_idx),
                mix_a, mix_b, z_a, z_b)


def _odd_step(h_ref, convw_ref, convb_ref, wa_ref, wx_ref, ba_ref, bx_ref,
              lam_ref, wpool_ref, dscale_ref, x_ref, wout_ref, g_ref, b_ref, o_ref,
              ext_c, ext_d, lvl2, lvl4, hstate, sa, sb, sh, wax_bf, wpool_bf,
              mix_ref, mix_prev, z_ref, z_prev, *, layer, s_idx, mix, proj, ln):
    xc_ref, cg_ref, xd_ref, dg_ref = (h_ref.at[:, c * C_WIDTH:(c + 1) * C_WIDTH] for c in range(4))
    t_rows = h_ref.shape[0]
    row = slice(layer, layer + 1)
    n_groups = t_rows // SUBLANES
    tail = _Tail(mix_prev, wout_ref, x_ref, z_ref, z_prev, g_ref, b_ref, layer, o_ref, proj, ln)
    if not mix:
        tail.finish()
        return
    tail.emit()

    z = -lam_ref[row, :]
    softplus = jnp.maximum(z, 0.0) + jnp.log1p(jnp.exp(-jnp.abs(z)))
    for hd in range(C_HEADS):
        cols = slice(hd * C_HEAD_DIM, (hd + 1) * C_HEAD_DIM)
        ext_c[hd, CONV_HALO:, :] = xc_ref[:, cols]
        xconv = convb_ref[row, cols]
        for j in range(CONV_WIDTH):
            lo = CONV_HALO - (CONV_WIDTH - 1) + j
            xconv = xconv + ext_c[hd, lo:lo + t_rows, :] * convw_ref[j:j + 1, cols]
        ri = jnp.dot(xconv.astype(BF16), wax_bf[hd], preferred_element_type=F32)
        r = _sigmoid(ri[:, :C_HEAD_DIM] + ba_ref[row, cols])
        i = _sigmoid(ri[:, C_HEAD_DIM:] + bx_ref[row, cols])
        log_a = -LRU_C * r * softplus[:, cols]
        a = jnp.exp(log_a)
        m2 = -jnp.tanh(log_a) * (a * a + 1.0)
        mult = jnp.where(m2 == 0.0, 0.0, m2 * lax.rsqrt(m2))
        b = mult * (i * xconv)
        for g in range(n_groups):
            base = (g * C_HEADS + hd) * SUBLANES
            sa[base:base + SUBLANES, :] = a[g * SUBLANES:(g + 1) * SUBLANES]
            sb[base:base + SUBLANES, :] = b[g * SUBLANES:(g + 1) * SUBLANES]
        if hd in (2, 5):
            tail.ln_half()
        if hd in (1, 3, 4, 6, 7):
            tail.emit()

    h = hstate[...]
    for t in range(t_rows):
        g, s = divmod(t, SUBLANES)
        at_t = pl.ds(g * C_HEADS * SUBLANES + s, C_HEADS, stride=SUBLANES)
        h = sa[at_t, :] * h + sb[at_t, :]
        sh[at_t, :] = h
    hstate[...] = h
    tail.emit()
    for hd in range(C_HEADS):
        cols = slice(hd * C_HEAD_DIM, (hd + 1) * C_HEAD_DIM)
        hcol = jnp.concatenate(
            [sh[(g * C_HEADS + hd) * SUBLANES:(g * C_HEADS + hd + 1) * SUBLANES, :]
             for g in range(n_groups)], axis=0)
        mix_ref[:, cols] = (hcol * _silu(cg_ref[:, cols])).astype(BF16)

    n_ext = POOL_HALO + t_rows
    tiles_per_group = D_GROUP_DIM // LANES
    pos1 = s_idx * t_rows + lax.broadcasted_iota(jnp.int32, (t_rows, LANES), 0) + 1
    for g, w in enumerate(POOL_WINDOWS):
        if g == 2:
            tail.emit()
        inv = 1.0 / jnp.minimum(pos1, w).astype(F32)
        pooled = []
        for k in range(tiles_per_group):
            tile = g * tiles_per_group + k
            cols = slice(tile * LANES, (tile + 1) * LANES)
            ext_d[tile, POOL_HALO:, :] = xd_ref[:, cols]
            lo = SUBLANES
            win = ext_d[tile, lo:n_ext, :] + ext_d[tile, lo - 1:n_ext - 1, :]
            if w >= 4:
                lvl2[k, lo:n_ext, :] = win
                lo += SUBLANES
                win = lvl2[k, lo:n_ext, :] + lvl2[k, lo - 2:n_ext - 2, :]
            if w >= 8:
                lvl4[k, lo:n_ext, :] = win
                lo += SUBLANES
                win = lvl4[k, lo:n_ext, :] + lvl4[k, lo - 4:n_ext - 4, :]
            if w >= 16:
                lo += SUBLANES
                win = win[SUBLANES:] + win[:-SUBLANES]
            pooled.append((win[POOL_HALO - lo:] * inv - xd_ref[:, cols]).astype(BF16))
        cols = slice(g * D_GROUP_DIM, (g + 1) * D_GROUP_DIM)
        mixed = jnp.dot(jnp.concatenate(pooled, axis=1), wpool_bf[g], preferred_element_type=F32)
        out_cols = slice(C_WIDTH + g * D_GROUP_DIM, C_WIDTH + (g + 1) * D_GROUP_DIM)
        mix_ref[:, out_cols] = (mixed * dscale_ref[row, cols] * _silu(dg_ref[:, cols])).astype(BF16)

    tail.finish()


def _odd_layer(h, x, conv_w, conv_b, w_a, w_x, b_a, b_x, lam, w_pool, d_scale, w_out_bf, ln_g, ln_b,
               layer, seq):
    m = h.shape[0]
    t = MIX_T
    n_tiles = m // t
    n_s = seq // t
    cur, res, out = _lagged_tiles(n_tiles)
    n_pool = len(POOL_WINDOWS)
    tail_in, out_specs, out_shape, tail_scratch, tail_args = _tail_specs(x, w_out_bf, ln_g, ln_b, res, out)
    return pl.pallas_call(
        partial(_odd_layer_kernel, layer=layer, n_tiles=n_tiles, n_s=n_s),
        grid=(n_tiles + 2,),
        in_specs=[
            pl.BlockSpec((t, h.shape[1]), lambda i: (cur(i), 0)),
            _layer_block((CONV_WIDTH, C_WIDTH), layer),
            _whole(conv_b),
            _layer_block((C_HEADS, C_HEAD_DIM, C_HEAD_DIM), layer),
            _layer_block((C_HEADS, C_HEAD_DIM, C_HEAD_DIM), layer),
            _whole(b_a), _whole(b_x), _whole(lam),
            _layer_block((n_pool, D_GROUP_DIM, D_GROUP_DIM), layer),
            _whole(d_scale),
        ] + tail_in,
        out_specs=out_specs,
        out_shape=out_shape,
        scratch_shapes=tail_scratch + [
            pltpu.VMEM((C_WIDTH // LANES, CONV_HALO + t, LANES), F32),
            pltpu.VMEM((D_WIDTH // LANES, POOL_HALO + t, LANES), F32),
            pltpu.VMEM((D_GROUP_DIM // LANES, POOL_HALO + t, LANES), F32),
            pltpu.VMEM((D_GROUP_DIM // LANES, POOL_HALO + t, LANES), F32),
            pltpu.VMEM((SUBLANES, LANES), F32),
            pltpu.VMEM((t * C_HEADS, LANES), F32),
            pltpu.VMEM((t * C_HEADS, LANES), F32),
            pltpu.VMEM((t * C_HEADS, LANES), F32),
            pltpu.VMEM((C_HEADS, C_HEAD_DIM, 2 * C_HEAD_DIM), BF16),
            pltpu.VMEM((n_pool, D_GROUP_DIM, D_GROUP_DIM), BF16),
        ],
        compiler_params=_compiler_params("arbitrary"),
        name="odd_layer",
    )(h, conv_w, conv_b, w_a, w_x, b_a, b_x, lam, w_pool, d_scale, *tail_args)


def _rope_tables(positions):
    half = ROT_DIM // 2
    inv_freq = ROPE_THETA ** (-jnp.arange(0, ROT_DIM, 2, dtype=F32) / ROT_DIM)
    rest = jnp.zeros((B_HEAD_DIM - ROT_DIM,), F32)
    freq_head = jnp.concatenate([inv_freq, inv_freq, rest])
    sign_head = jnp.concatenate([-jnp.ones((half,), F32), jnp.ones((half,), F32), rest])
    reps = LANES // B_HEAD_DIM
    ang = positions.astype(F32).reshape(-1, 1) * jnp.tile(freq_head, reps)[None, :]
    return jnp.concatenate([jnp.cos(ang), jnp.sin(ang) * jnp.tile(sign_head, reps)[None, :]], axis=1)


def kernel(x, positions, even_w_in, even_a_ln_g, even_a_ln_b, even_a_ws, even_a_bs, even_b_sinks, even_w_out, even_ln_g, even_ln_b, odd_w_in, odd_conv_w, odd_conv_b, odd_w_a, odd_b_a, odd_w_x, odd_b_x, odd_lam, odd_w_pool, odd_d_scale, odd_w_out, odd_ln_g, odd_ln_b):
    batch, seq, d = x.shape
    m = batch * seq
    xf = x.reshape(m, d)
    rope_tab = _rope_tables(positions)
    even_bst = jnp.swapaxes(even_a_bs, 1, 2)
    w_in, w_layer = even_w_in, 0
    for layer in range(DEPTH):
        j = layer // 2
        if layer % 2 == 0:
            h, w_out, w_in = _proj_in(xf, w_in, w_layer, even_w_out, j, odd_w_in, j)
            xf = _even_layer(h, xf, even_b_sinks, rope_tab, even_a_ln_g, even_a_ln_b, even_a_ws,
                             even_bst, w_out, even_ln_g, even_ln_b, j, seq)
        else:
            w_next = even_w_in if layer + 1 < DEPTH else None
            h, w_out, w_in = _proj_in(xf, w_in, w_layer, odd_w_out, j, w_next, j + 1)
            xf = _odd_layer(h, xf, odd_conv_w, odd_conv_b, odd_w_a, odd_w_x, odd_b_a, odd_b_x, odd_lam,
                            odd_w_pool, odd_d_scale, w_out, odd_ln_g, odd_ln_b, j, seq)
        w_layer = None
    return xf.reshape(batch, seq, d)
```

```python
from functools import partial

import jax
import jax.numpy as jnp
import numpy as np
from jax import lax
from jax.experimental import pallas as pl
from jax.experimental.pallas import tpu as pltpu

D_MODEL = 2048
DEPTH = 4
A_WIDTH = 1024
A_GROUPS = 8
CHUNK = 128
B_HEAD_DIM = 64
B_Q_HEADS = 16
B_KV_HEADS = 2
B_WIDTH = 1024
B_KV_WIDTH = 128
WINDOW = 128
ROT_DIM = 16
ROPE_THETA = 500000.0
C_WIDTH = 1024
C_HEADS = 8
C_HEAD_DIM = 128
CONV_WIDTH = 4
LRU_C = 8.0
D_WIDTH = 1024
POOL_WINDOWS = (2, 4, 8, 16)
D_GROUP_DIM = 256
MIX_WIDTH = 2048
DN_ALPHA = (2 * DEPTH) ** 0.25
LN_EPS = 1e-5

LANES = 128
SUBLANES = 8
VMEM_LIMIT_BYTES = 56 * 1024 * 1024

PROJ_TM = 256
MIX_T = 256
OUT_BLOCK = 256
OUT_BLOCKS = D_MODEL // OUT_BLOCK
W_CHUNK = 256
CONV_HALO = SUBLANES
POOL_HALO = 4 * SUBLANES

KV_BLOCK = 2 * B_KV_WIDTH
KV_COL = (3 * A_WIDTH + B_WIDTH) // KV_BLOCK
GATE_B_COL = KV_COL + 1

F32 = jnp.float32
BF16 = jnp.bfloat16


def _sigmoid(x):
    return 0.5 * jnp.tanh(0.5 * x) + 0.5


def _silu(x):
    hx = 0.5 * x
    return hx * jnp.tanh(hx) + hx


def _resident(shape):
    return pl.BlockSpec(shape, lambda *_: (0,) * len(shape), pipeline_mode=pl.Buffered(1))


def _whole(arr):
    return pl.BlockSpec(arr.shape, lambda *_: (0,) * arr.ndim)


def _layer_block(shape, layer):
    return pl.BlockSpec((None,) + shape, lambda *_: (layer,) + (0,) * len(shape))


def _compiler_params(*semantics):
    return pltpu.CompilerParams(dimension_semantics=semantics, vmem_limit_bytes=VMEM_LIMIT_BYTES)


def _proj_in_kernel(*refs, cast_next, w_layer):
    n_in = 4 if cast_next else 3
    x_ref, w_hbm, wout_ref = refs[:3]
    o_ref, wout_bf_ref = refs[n_in:n_in + 2]
    if cast_next:
        wn_ref, wn_bf_ref = refs[3], refs[n_in + 2]
        wn_bf_ref[...] = wn_ref[...].astype(BF16)
    staged = w_hbm.dtype != BF16
    w_vmem, *stage, sems = refs[2 * n_in - 1:]
    for p in range(OUT_BLOCKS):
        wout_bf_ref[p] = wout_ref[:, p * OUT_BLOCK:(p + 1) * OUT_BLOCK].astype(BF16)

    n_chunks = w_vmem.shape[0] // W_CHUNK
    w_src = w_hbm if w_layer is None else w_hbm.at[w_layer]

    def chunk_copy(c):
        rows = pl.ds(c * W_CHUNK, W_CHUNK)
        if staged:
            return pltpu.make_async_copy(w_src.at[rows, :], stage[0].at[c % 2], sems.at[c % 2])
        return pltpu.make_async_copy(w_src.at[rows, :], w_vmem.at[rows, :], sems.at[c])

    step = pl.program_id(0)

    @pl.when(step == 0)
    def _():
        xb = x_ref[...].astype(BF16)
        for c in range(2 if staged else n_chunks):
            chunk_copy(c).start()
        for c in range(n_chunks):
            rows = slice(c * W_CHUNK, (c + 1) * W_CHUNK)
            chunk_copy(c).wait()
            if staged:
                w_vmem[rows, :] = stage[0][c % 2].astype(BF16)
                if c + 2 < n_chunks:
                    chunk_copy(c + 2).start()
            part = jnp.dot(xb[:, rows], w_vmem[rows, :], preferred_element_type=F32)
            if c == 0:
                o_ref[...] = part
            else:
                o_ref[...] += part

    @pl.when(step > 0)
    def _():
        o_ref[...] = jnp.dot(x_ref[...].astype(BF16), w_vmem[...], preferred_element_type=F32)


def _proj_in(x, w, w_layer, w_out, layer, w_next, layer_next):
    m, k = x.shape
    n = w.shape[-1]
    steps = m // PROJ_TM

    def slab_specs(stacked, which):
        rows, cols = stacked.shape[1:]
        slab = rows // steps
        return (pl.BlockSpec((None, slab, cols), lambda i: (which, i, 0)),
                pl.BlockSpec((slab, cols), lambda i: (i, 0)),
                jax.ShapeDtypeStruct((rows, cols), BF16))

    k_out = w_out.shape[1]
    slab_out = k_out // steps
    casts = [(pl.BlockSpec((None, slab_out, D_MODEL), lambda i: (layer, i, 0)),
              pl.BlockSpec((OUT_BLOCKS, slab_out, OUT_BLOCK), lambda i: (0, i, 0)),
              jax.ShapeDtypeStruct((OUT_BLOCKS, k_out, OUT_BLOCK), BF16))]
    args = [x, w, w_out]
    if w_next is not None:
        casts.append(slab_specs(w_next, layer_next))
        args.append(w_next)
    staged = w.dtype != BF16
    scratch = [pltpu.VMEM((k, n), BF16)]
    if staged:
        scratch.append(pltpu.VMEM((2, W_CHUNK, n), w.dtype))
    scratch.append(pltpu.SemaphoreType.DMA((2 if staged else k // W_CHUNK,)))
    outs = pl.pallas_call(
        partial(_proj_in_kernel, cast_next=w_next is not None, w_layer=w_layer),
        grid=(steps,),
        in_specs=[pl.BlockSpec((PROJ_TM, k), lambda i: (i, 0)), pl.BlockSpec(memory_space=pl.ANY)]
        + [c[0] for c in casts],
        out_specs=[pl.BlockSpec((PROJ_TM, n), lambda i: (i, 0))] + [c[1] for c in casts],
        out_shape=[jax.ShapeDtypeStruct((m, n), F32)] + [c[2] for c in casts],
        scratch_shapes=scratch,
        compiler_params=_compiler_params("arbitrary"),
        name="proj_in",
    )(*args)
    return outs if w_next is not None else (*outs, None)


class _Tail:
    def __init__(self, mix_prev, wout_ref, x_ref, z_ref, z_prev, g_ref, b_ref, layer, o_ref,
                 proj, ln):
        self.mix = mix_prev[...] if proj else None
        self.proj_refs = (wout_ref, x_ref, z_ref)
        self.ln_refs = (z_prev, g_ref, b_ref, layer, o_ref)
        self.blocks_left = list(range(OUT_BLOCKS)) if proj else []
        rows = z_ref.shape[0]
        self.halves_left = [slice(r * rows // 2, (r + 1) * rows // 2) for r in range(2)] if ln else []

    def emit(self, count=1):
        wout_ref, x_ref, z_ref = self.proj_refs
        for _ in range(min(count, len(self.blocks_left))):
            blk = self.blocks_left.pop(0)
            cols = slice(blk * OUT_BLOCK, (blk + 1) * OUT_BLOCK)
            z_ref[:, cols] = DN_ALPHA * x_ref[:, cols] + jnp.dot(
                self.mix, wout_ref[blk], preferred_element_type=F32)

    def ln_half(self):
        if self.halves_left:
            z_prev, g_ref, b_ref, layer, o_ref = self.ln_refs
            rows = self.halves_left.pop(0)
            z = z_prev[rows, :]
            mu = jnp.mean(z, axis=-1, keepdims=True)
            zc = z - mu
            var = jnp.mean(zc * zc, axis=-1, keepdims=True)
            o_ref[rows, :] = (zc * lax.rsqrt(var + LN_EPS) * g_ref[layer:layer + 1, :]
                              + b_ref[layer:layer + 1, :])

    def finish(self):
        self.ln_half()
        self.ln_half()
        self.emit(OUT_BLOCKS)


def _lagged_tiles(n_tiles):
    def clamp(i):
        return jnp.clip(i, 0, n_tiles - 1)

    return clamp, (lambda i: clamp(i - 1)), (lambda i: clamp(i - 2))


def _run_stages(step, n_tiles, body, wout_hbm, wout_vmem, wout_sem, mix_a, mix_b, z_a, z_b):
    wout_copy = pltpu.make_async_copy(wout_hbm, wout_vmem, wout_sem.at[0])

    @pl.when(step == 0)
    def _():
        wout_copy.start()

    @pl.when(step == 1)
    def _():
        wout_copy.wait()

    def stage(pred, parity, **stages):
        bufs = (mix_a, mix_b, z_a, z_b) if parity == 0 else (mix_b, mix_a, z_b, z_a)

        @pl.when(pred)
        def _():
            body(*bufs, **stages)

    steady = (step >= 2) & (step < n_tiles)
    stage(step == 0, 0, mix=True, proj=False, ln=False)
    stage(step == 1, 1, mix=True, proj=True, ln=False)
    stage(steady & (step % 2 == 0), 0, mix=True, proj=True, ln=True)
    stage(steady & (step % 2 == 1), 1, mix=True, proj=True, ln=True)
    stage(step == n_tiles, n_tiles % 2, mix=False, proj=True, ln=True)
    stage(step == n_tiles + 1, (n_tiles + 1) % 2, mix=False, proj=False, ln=True)


def _tail_specs(x, w_out_bf, ln_g, ln_b, res, out):
    m, d = x.shape
    in_specs = [
        pl.BlockSpec((MIX_T, d), lambda i: (res(i), 0)),
        pl.BlockSpec(memory_space=pl.ANY),
        _whole(ln_g),
        _whole(ln_b),
    ]
    out_spec = pl.BlockSpec((MIX_T, d), lambda i: (out(i), 0))
    out_shape = jax.ShapeDtypeStruct((m, d), F32)
    scratch = [pltpu.VMEM((MIX_T, MIX_WIDTH), BF16), pltpu.VMEM((MIX_T, MIX_WIDTH), BF16),
               pltpu.VMEM((MIX_T, d), F32), pltpu.VMEM((MIX_T, d), F32),
               pltpu.VMEM(w_out_bf.shape, BF16), pltpu.SemaphoreType.DMA((1,))]
    return in_specs, out_spec, out_shape, scratch, [x, w_out_bf, ln_g, ln_b]


def _rope_tile(t, cos, sin, take_upper):
    upper = pltpu.roll(t, LANES - ROT_DIM // 2, 1)
    lower = pltpu.roll(t, ROT_DIM // 2, 1)
    return t * cos + jnp.where(take_upper, upper, lower) * sin


EVEN_WOUT_ARG = 10
ODD_WOUT_ARG = 11


def _even_layer_kernel(*refs, layer, n_tiles, n_s):
    *io_refs, mix_a, mix_b, z_a, z_b, wout_vmem, wout_sem = refs
    wout_hbm = io_refs[EVEN_WOUT_ARG]
    io_refs[EVEN_WOUT_ARG] = wout_vmem
    step = pl.program_id(0)
    seq_tile = jnp.minimum(step, n_tiles - 1) % n_s
    _run_stages(step, n_tiles, partial(_even_step, *io_refs, layer=layer, seq_tile=seq_tile),
                wout_hbm, wout_vmem, wout_sem, mix_a, mix_b, z_a, z_b)


def _even_step(sinks_ref, h_ref, kvp_ref, tabc_ref, tabp_ref, lng_ref, lnb_ref, ws_ref, bst_ref,
               x_ref, wout_ref, g_ref, b_ref, o_ref, mix_ref, mix_prev, z_ref, z_prev,
               *, layer, seq_tile, mix, proj, ln):
    u_ref, v_ref, ag_ref, q_ref = (h_ref.at[:, c * A_WIDTH:(c + 1) * A_WIDTH] for c in range(4))
    kvc_ref = h_ref.at[:, KV_COL * KV_BLOCK:(KV_COL + 1) * KV_BLOCK]
    bg_refs = [h_ref.at[:, (GATE_B_COL + c) * KV_BLOCK:(GATE_B_COL + c + 1) * KV_BLOCK]
               for c in range(B_WIDTH // KV_BLOCK)]
    cosc_ref, sinc_ref = tabc_ref.at[:, :LANES], tabc_ref.at[:, LANES:]
    cosp_ref, sinp_ref = tabp_ref.at[:, :LANES], tabp_ref.at[:, LANES:]
    t_rows = h_ref.shape[0]
    n_chunks = t_rows // CHUNK
    tail = _Tail(mix_prev, wout_ref, x_ref, z_ref, z_prev, g_ref, b_ref, layer, o_ref, proj, ln)
    if not mix:
        tail.finish()
        return
    tail.emit()

    v = v_ref[...]
    mu = jnp.mean(v, axis=-1, keepdims=True)
    vc = v - mu
    var = jnp.mean(vc * vc, axis=-1, keepdims=True)
    vln = (vc * lax.rsqrt(var + LN_EPS) * lng_ref[layer:layer + 1, :]
           + lnb_ref[layer:layer + 1, :]).astype(BF16)
    causal = (lax.broadcasted_iota(jnp.int32, (CHUNK, CHUNK), 0)
              >= lax.broadcasted_iota(jnp.int32, (CHUNK, CHUNK), 1))
    lane = lax.broadcasted_iota(jnp.int32, (1, LANES), 1)
    take_upper = (lane % B_HEAD_DIM) < (ROT_DIM // 2)
    low_half = lane < B_HEAD_DIM
    cosc = cosc_ref[...]
    sinc = sinc_ref[...]
    scale = B_HEAD_DIM ** -0.5
    q_tiles = []
    for g in range(A_GROUPS):
        cols = slice(g * CHUNK, (g + 1) * CHUNK)
        w = jnp.where(causal, ws_ref[g], 0.0).astype(BF16)
        rhs = jnp.concatenate(
            [vln[c * CHUNK:(c + 1) * CHUNK, cols] for c in range(n_chunks)], axis=1)
        mixed = jnp.dot(w, rhs, preferred_element_type=F32) + bst_ref[:, g:g + 1]
        for c in range(n_chunks):
            rows = slice(c * CHUNK, (c + 1) * CHUNK)
            gate = _silu(ag_ref[rows, cols])
            mix_ref[rows, cols] = (u_ref[rows, cols] * mixed[:, c * CHUNK:(c + 1) * CHUNK]
                                   * gate).astype(BF16)
        q_tiles.append(
            (_rope_tile(q_ref[:, cols], cosc, sinc, take_upper) * scale).astype(BF16))
        if g in (1, 5):
            tail.ln_half()
        if g in (2, 4, 6):
            tail.emit()

    k_all = jnp.concatenate(
        [_rope_tile(kvp_ref[:, :B_KV_WIDTH], cosp_ref[...], sinp_ref[...], take_upper),
         _rope_tile(kvc_ref[:, :B_KV_WIDTH], cosc, sinc, take_upper)], axis=0)
    k_swap = pltpu.roll(k_all, B_HEAD_DIM, 1)
    zero = jnp.zeros_like(k_all)
    k_lo = (jnp.where(low_half, k_all, zero).astype(BF16), jnp.where(low_half, k_swap, zero).astype(BF16))
    k_hi = (jnp.where(low_half, zero, k_swap).astype(BF16), jnp.where(low_half, zero, k_all).astype(BF16))
    v_all = jnp.concatenate([kvp_ref[:, B_KV_WIDTH:], kvc_ref[:, B_KV_WIDTH:]], axis=0)
    vt_all = v_all.T.astype(BF16)

    tiles_per_kv = B_WIDTH // LANES // B_KV_HEADS
    band_keys = 2 * WINDOW
    kj = lax.broadcasted_iota(jnp.int32, (band_keys, WINDOW), 0)
    qi = lax.broadcasted_iota(jnp.int32, (band_keys, WINDOW), 1)
    diff = qi + WINDOW - kj
    band = (diff >= 0) & (diff < WINDOW)
    first_key = jnp.where(seq_tile == 0, WINDOW, 0)
    neg_inf = jnp.full((band_keys, WINDOW), -jnp.inf, F32)
    mask_any = jnp.where(band, 0.0, neg_inf)
    mask_first = jnp.where(band & (kj >= first_key), 0.0, neg_inf)

    for n in range(t_rows // WINDOW):
        rows = slice(n * WINDOW, (n + 1) * WINDOW)
        band_rows = slice(n * WINDOW, (n + 2) * WINDOW)
        mask = jnp.concatenate([mask_first if n == 0 else mask_any] * tiles_per_kv, axis=1)
        for kv in range(B_KV_HEADS):
            tail.emit()
            q4 = jnp.concatenate(
                [q_tiles[kv * tiles_per_kv + j][rows] for j in range(tiles_per_kv)], axis=0)
            k_cat = jnp.concatenate([k_lo[kv][band_rows], k_hi[kv][band_rows]], axis=0)
            st = lax.dot_general(k_cat, q4, (((1,), (1,)), ((), ())), preferred_element_type=F32)
            vt = vt_all[kv * B_HEAD_DIM:(kv + 1) * B_HEAD_DIM, band_rows]
            o_halves = []
            for half in range(2):
                sh = st[half * band_keys:(half + 1) * band_keys] + mask
                sink = jnp.concatenate(
                    [jnp.full((1, WINDOW), sinks_ref[layer, 2 * (kv * tiles_per_kv + j) + half], F32)
                     for j in range(tiles_per_kv)], axis=1)
                m = jnp.maximum(jnp.max(sh, axis=0, keepdims=True), sink)
                p = jnp.exp(sh - m)
                l = jnp.sum(p, axis=0, keepdims=True) + jnp.exp(sink - m)
                o_half = jnp.dot(vt, p.astype(BF16), preferred_element_type=F32)
                o_halves.append(o_half * (1.0 / l))
            ot = jnp.concatenate(o_halves, axis=0)
            for j in range(tiles_per_kv):
                jj = kv * tiles_per_kv + j
                gate_ref = bg_refs[jj // 2]
                gate_cols = slice((jj % 2) * LANES, (jj % 2 + 1) * LANES)
                out_cols = slice(A_WIDTH + jj * LANES, A_WIDTH + (jj + 1) * LANES)
                o_tile = ot[:, j * WINDOW:(j + 1) * WINDOW].T
                mix_ref[rows, out_cols] = (o_tile * _silu(gate_ref[rows, gate_cols])).astype(BF16)

    tail.finish()


def _even_layer(h, x, sinks, rope_tab, a_ln_g, a_ln_b, ws, bst, w_out_bf, ln_g, ln_b, layer, seq):
    m = h.shape[0]
    t = MIX_T
    n_tiles = m // t
    n_s = seq // t
    blocks_per_tile = t // WINDOW
    cur, res, out = _lagged_tiles(n_tiles)

    def prev(i):
        return jnp.maximum(cur(i) * blocks_per_tile - 1, 0)

    tail_in, out_specs, out_shape, tail_scratch, tail_args = _tail_specs(x, w_out_bf, ln_g, ln_b, res, out)
    return pl.pallas_call(
        partial(_even_layer_kernel, layer=layer, n_tiles=n_tiles, n_s=n_s),
        grid=(n_tiles + 2,),
        in_specs=[
            pl.BlockSpec(memory_space=pltpu.SMEM),
            pl.BlockSpec((t, h.shape[1]), lambda i: (cur(i), 0)),
            pl.BlockSpec((WINDOW, KV_BLOCK), lambda i: (prev(i), KV_COL)),
            pl.BlockSpec((t, 2 * LANES), lambda i: (cur(i), 0)),
            pl.BlockSpec((WINDOW, 2 * LANES), lambda i: (prev(i), 0)),
            _whole(a_ln_g),
            _whole(a_ln_b),
            _layer_block((A_GROUPS, CHUNK, CHUNK), layer),
            _layer_block((CHUNK, A_GROUPS), layer),
        ] + tail_in,
        out_specs=out_specs,
        out_shape=out_shape,
        scratch_shapes=tail_scratch,
        compiler_params=_compiler_params("arbitrary"),
        name="even_layer",
    )(sinks, h, h, rope_tab, rope_tab, a_ln_g, a_ln_b, ws, bst, *tail_args)


def _odd_layer_kernel(*refs, layer, n_tiles, n_s):
    *io_refs, mix_a, mix_b, z_a, z_b, wout_vmem, wout_sem = refs[:21]
    (ext_c, ext_d, lvl2, lvl4, hstate, sa, sb, sh, wax_bf, wpool_bf) = scratch = refs[21:]
    wout_hbm = io_refs[ODD_WOUT_ARG]
    io_refs[ODD_WOUT_ARG] = wout_vmem
    wa_ref, wx_ref, wpool_ref = io_refs[3], io_refs[4], io_refs[8]
    t_rows = io_refs[0].shape[0]
    step = pl.program_id(0)
    s_idx = jnp.minimum(step, n_tiles - 1) % n_s

    @pl.when(s_idx == 0)
    def _():
        ext_c[:, 0:CONV_HALO, :] = jnp.zeros((C_HEADS, CONV_HALO, LANES), F32)
        ext_d[:, 0:POOL_HALO, :] = jnp.zeros((D_WIDTH // LANES, POOL_HALO, LANES), F32)
        hstate[...] = jnp.zeros_like(hstate)
        wax_bf[:, :, :C_HEAD_DIM] = wa_ref[...].astype(BF16)
        wax_bf[:, :, C_HEAD_DIM:] = wx_ref[...].astype(BF16)
        wpool_bf[...] = wpool_ref[...].astype(BF16)

    @pl.when(s_idx > 0)
    def _():
        ext_c[:, 0:CONV_HALO, :] = ext_c[:, t_rows:t_rows + CONV_HALO, :]
        ext_d[:, 0:POOL_HALO, :] = ext_d[:, t_rows:t_rows + POOL_HALO, :]

    _run_stages(step, n_tiles, partial(_odd_step, *io_refs, *scratch, layer=layer, s_idx=s_idx),
                wout_hbm, wout_vmem, wout_sem, mix_a, mix_b, z_a, z_b)


def _odd_step(h_ref, convw_ref, convb_ref, wa_ref, wx_ref, ba_ref, bx_ref,
              lam_ref, wpool_ref, dscale_ref, x_ref, wout_ref, g_ref, b_ref, o_ref,
              ext_c, ext_d, lvl2, lvl4, hstate, sa, sb, sh, wax_bf, wpool_bf,
              mix_ref, mix_prev, z_ref, z_prev, *, layer, s_idx, mix, proj, ln):
    xc_ref, cg_ref, xd_ref, dg_ref = (h_ref.at[:, c * C_WIDTH:(c + 1) * C_WIDTH] for c in range(4))
    t_rows = h_ref.shape[0]
    row = slice(layer, layer + 1)
    n_groups = t_rows // SUBLANES
    tail = _Tail(mix_prev, wout_ref, x_ref, z_ref, z_prev, g_ref, b_ref, layer, o_ref, proj, ln)
    if not mix:
        tail.finish()
        return
    tail.emit()

    z = -lam_ref[row, :]
    softplus = jnp.maximum(z, 0.0) + jnp.log1p(jnp.exp(-jnp.abs(z)))
    for hd in range(C_HEADS):
        cols = slice(hd * C_HEAD_DIM, (hd + 1) * C_HEAD_DIM)
        ext_c[hd, CONV_HALO:, :] = xc_ref[:, cols]
        xconv = convb_ref[row, cols]
        for j in range(CONV_WIDTH):
            lo = CONV_HALO - (CONV_WIDTH - 1) + j
            xconv = xconv + ext_c[hd, lo:lo + t_rows, :] * convw_ref[j:j + 1, cols]
        ri = jnp.dot(xconv.astype(BF16), wax_bf[hd], preferred_element_type=F32)
        r = _sigmoid(ri[:, :C_HEAD_DIM] + ba_ref[row, cols])
        i = _sigmoid(ri[:, C_HEAD_DIM:] + bx_ref[row, cols])
        log_a = -LRU_C * r * softplus[:, cols]
        a = jnp.exp(log_a)
        m2 = -jnp.tanh(log_a) * (a * a + 1.0)
        mult = jnp.where(m2 == 0.0, 0.0, m2 * lax.rsqrt(m2))
        b = mult * (i * xconv)
        for g in range(n_groups):
            base = (g * C_HEADS + hd) * SUBLANES
            sa[base:base + SUBLANES, :] = a[g * SUBLANES:(g + 1) * SUBLANES]
            sb[base:base + SUBLANES, :] = b[g * SUBLANES:(g + 1) * SUBLANES]
        if hd in (2, 5):
            tail.ln_half()
        if hd in (1, 3, 4, 6, 7):
            tail.emit()

    h = hstate[...]
    for t in range(t_rows):
        g, s = divmod(t, SUBLANES)
        at_t = pl.ds(g * C_HEADS * SUBLANES + s, C_HEADS, stride=SUBLANES)
        h = sa[at_t, :] * h + sb[at_t, :]
        sh[at_t, :] = h
    hstate[...] = h
    tail.emit()
    for hd in range(C_HEADS):
        cols = slice(hd * C_HEAD_DIM, (hd + 1) * C_HEAD_DIM)
        hcol = jnp.concatenate(
            [sh[(g * C_HEADS + hd) * SUBLANES:(g * C_HEADS + hd + 1) * SUBLANES, :]
             for g in range(n_groups)], axis=0)
        mix_ref[:, cols] = (hcol * _silu(cg_ref[:, cols])).astype(BF16)

    n_ext = POOL_HALO + t_rows
    tiles_per_group = D_GROUP_DIM // LANES
    pos1 = s_idx * t_rows + lax.broadcasted_iota(jnp.int32, (t_rows, LANES), 0) + 1
    for g, w in enumerate(POOL_WINDOWS):
        if g == 2:
            tail.emit()
        inv = 1.0 / jnp.minimum(pos1, w).astype(F32)
        pooled = []
        for k in range(tiles_per_group):
            tile = g * tiles_per_group + k
            cols = slice(tile * LANES, (tile + 1) * LANES)
            ext_d[tile, POOL_HALO:, :] = xd_ref[:, cols]
            lo = SUBLANES
            win = ext_d[tile, lo:n_ext, :] + ext_d[tile, lo - 1:n_ext - 1, :]
            if w >= 4:
                lvl2[k, lo:n_ext, :] = win
                lo += SUBLANES
                win = lvl2[k, lo:n_ext, :] + lvl2[k, lo - 2:n_ext - 2, :]
            if w >= 8:
                lvl4[k, lo:n_ext, :] = win
                lo += SUBLANES
                win = lvl4[k, lo:n_ext, :] + lvl4[k, lo - 4:n_ext - 4, :]
            if w >= 16:
                lo += SUBLANES
                win = win[SUBLANES:] + win[:-SUBLANES]
            pooled.append((win[POOL_HALO - lo:] * inv - xd_ref[:, cols]).astype(BF16))
        cols = slice(g * D_GROUP_DIM, (g + 1) * D_GROUP_DIM)
        mixed = jnp.dot(jnp.concatenate(pooled, axis=1), wpool_bf[g], preferred_element_type=F32)
        out_cols = slice(C_WIDTH + g * D_GROUP_DIM, C_WIDTH + (g + 1) * D_GROUP_DIM)
        mix_ref[:, out_cols] = (mixed * dscale_ref[row, cols] * _silu(dg_ref[:, cols])).astype(BF16)

    tail.finish()


def _odd_layer(h, x, conv_w, conv_b, w_a, w_x, b_a, b_x, lam, w_pool, d_scale, w_out_bf, ln_g, ln_b,
               layer, seq):
    m = h.shape[0]
    t = MIX_T
    n_tiles = m // t
    n_s = seq // t
    cur, res, out = _lagged_tiles(n_tiles)
    n_pool = len(POOL_WINDOWS)
    tail_in, out_specs, out_shape, tail_scratch, tail_args = _tail_specs(x, w_out_bf, ln_g, ln_b, res, out)
    return pl.pallas_call(
        partial(_odd_layer_kernel, layer=layer, n_tiles=n_tiles, n_s=n_s),
        grid=(n_tiles + 2,),
        in_specs=[
            pl.BlockSpec((t, h.shape[1]), lambda i: (cur(i), 0)),
            _layer_block((CONV_WIDTH, C_WIDTH), layer),
            _whole(conv_b),
            _layer_block((C_HEADS, C_HEAD_DIM, C_HEAD_DIM), layer),
            _layer_block((C_HEADS, C_HEAD_DIM, C_HEAD_DIM), layer),
            _whole(b_a), _whole(b_x), _whole(lam),
            _layer_block((n_pool, D_GROUP_DIM, D_GROUP_DIM), layer),
            _whole(d_scale),
        ] + tail_in,
        out_specs=out_specs,
        out_shape=out_shape,
        scratch_shapes=tail_scratch + [
            pltpu.VMEM((C_WIDTH // LANES, CONV_HALO + t, LANES), F32),
            pltpu.VMEM((D_WIDTH // LANES, POOL_HALO + t, LANES), F32),
            pltpu.VMEM((D_GROUP_DIM // LANES, POOL_HALO + t, LANES), F32),
            pltpu.VMEM((D_GROUP_DIM // LANES, POOL_HALO + t, LANES), F32),
            pltpu.VMEM((SUBLANES, LANES), F32),
            pltpu.VMEM((t * C_HEADS, LANES), F32),
            pltpu.VMEM((t * C_HEADS, LANES), F32),
            pltpu.VMEM((t * C_HEADS, LANES), F32),
            pltpu.VMEM((C_HEADS, C_HEAD_DIM, 2 * C_HEAD_DIM), BF16),
            pltpu.VMEM((n_pool, D_GROUP_DIM, D_GROUP_DIM), BF16),
        ],
        compiler_params=_compiler_params("arbitrary"),
        name="odd_layer",
    )(h, conv_w, conv_b, w_a, w_x, b_a, b_x, lam, w_pool, d_scale, *tail_args)


def _rope_tables(positions):
    half = ROT_DIM // 2
    inv_freq = ROPE_THETA ** (-jnp.arange(0, ROT_DIM, 2, dtype=F32) / ROT_DIM)
    ang = positions.astype(F32).reshape(-1, 1) * inv_freq[None, :]
    trig = jnp.concatenate([jnp.cos(ang), jnp.sin(ang)], axis=1)
    place = np.zeros((2 * half, 2 * LANES), np.float32)
    ones = np.zeros((2 * LANES,), np.float32)
    for head in range(LANES // B_HEAD_DIM):
        lo = head * B_HEAD_DIM
        for k in range(half):
            place[k, [lo + k, lo + half + k]] = 1.0
            place[half + k, LANES + lo + k] = -1.0
            place[half + k, LANES + lo + half + k] = 1.0
        ones[lo + ROT_DIM:lo + B_HEAD_DIM] = 1.0
    return jnp.dot(trig, place, precision=lax.Precision.HIGHEST) + ones[None, :]


def kernel(x, positions, even_w_in, even_a_ln_g, even_a_ln_b, even_a_ws, even_a_bs, even_b_sinks, even_w_out, even_ln_g, even_ln_b, odd_w_in, odd_conv_w, odd_conv_b, odd_w_a, odd_b_a, odd_w_x, odd_b_x, odd_lam, odd_w_pool, odd_d_scale, odd_w_out, odd_ln_g, odd_ln_b):
    batch, seq, d = x.shape
    m = batch * seq
    xf = x.reshape(m, d)
    rope_tab = _rope_tables(positions)
    even_bst = jnp.swapaxes(even_a_bs, 1, 2)
    w_in, w_layer = even_w_in, 0
    for layer in range(DEPTH):
        j = layer // 2
        if layer % 2 == 0:
            h, w_out, w_in = _proj_in(xf, w_in, w_layer, even_w_out, j, odd_w_in, j)
            xf = _even_layer(h, xf, even_b_sinks, rope_tab, even_a_ln_g, even_a_ln_b, even_a_ws,
                             even_bst, w_out, even_ln_g, even_ln_b, j, seq)
        else:
            w_next = even_w_in if layer + 1 < DEPTH else None
            h, w_out, w_in = _proj_in(xf, w_in, w_layer, odd_w_out, j, w_next, j + 1)
            xf = _odd_layer(h, xf, odd_conv_w, odd_conv_b, odd_w_a, odd_w_x, odd_b_a, odd_b_x, odd_lam,
                            odd_w_pool, odd_d_scale, w_out, odd_ln_g, odd_ln_b, j, seq)
        w_layer = None
    return xf.reshape(batch, seq, d)
```

```python
from functools import partial

import jax
import jax.numpy as jnp
from jax import lax
from jax.experimental import pallas as pl
from jax.experimental.pallas import tpu as pltpu

D_MODEL = 2048
DEPTH = 4
A_WIDTH = 1024
A_GROUPS = 8
CHUNK = 128
B_HEAD_DIM = 64
B_Q_HEADS = 16
B_KV_HEADS = 2
B_WIDTH = 1024
B_KV_WIDTH = 128
WINDOW = 128
ROT_DIM = 16
ROPE_THETA = 500000.0
C_WIDTH = 1024
C_HEADS = 8
C_HEAD_DIM = 128
CONV_WIDTH = 4
LRU_C = 8.0
D_WIDTH = 1024
POOL_WINDOWS = (2, 4, 8, 16)
D_GROUP_DIM = 256
MIX_WIDTH = 2048
DN_ALPHA = (2 * DEPTH) ** 0.25
LN_EPS = 1e-5

LANES = 128
SUBLANES = 8
VMEM_LIMIT_BYTES = 56 * 1024 * 1024

PROJ_TM = 256
MIX_T = 256
OUT_BLOCK = 256
OUT_BLOCKS = D_MODEL // OUT_BLOCK
W_CHUNK = 256
CONV_HALO = SUBLANES
POOL_HALO = 4 * SUBLANES

KV_BLOCK = 2 * B_KV_WIDTH
KV_COL = (3 * A_WIDTH + B_WIDTH) // KV_BLOCK
GATE_B_COL = KV_COL + 1

F32 = jnp.float32
BF16 = jnp.bfloat16


def _sigmoid(x):
    return 0.5 * jnp.tanh(0.5 * x) + 0.5


def _silu(x):
    hx = 0.5 * x
    return hx * jnp.tanh(hx) + hx


def _resident(shape):
    return pl.BlockSpec(shape, lambda *_: (0,) * len(shape), pipeline_mode=pl.Buffered(1))


def _whole(arr):
    return pl.BlockSpec(arr.shape, lambda *_: (0,) * arr.ndim)


def _layer_block(shape, layer):
    return pl.BlockSpec((None,) + shape, lambda *_: (layer,) + (0,) * len(shape))


def _compiler_params(*semantics):
    return pltpu.CompilerParams(dimension_semantics=semantics, vmem_limit_bytes=VMEM_LIMIT_BYTES)


def _proj_in_kernel(*refs, cast_next, w_layer, rope):
    n_in = 3 + cast_next + 2 * rope
    n_out = 2 + cast_next + rope
    x_ref, w_hbm, wout_ref = refs[:3]
    o_ref, wout_bf_ref = refs[n_in:n_in + 2]
    staged = w_hbm.dtype != BF16
    w_vmem, *stage, sems = refs[n_in + n_out:]

    def side_jobs():
        if cast_next:
            refs[n_in + 2][...] = refs[3][...].astype(BF16)
        for p in range(OUT_BLOCKS):
            wout_bf_ref[p] = wout_ref[:, p * OUT_BLOCK:(p + 1) * OUT_BLOCK].astype(BF16)
        if rope:
            pos_ref, coef_ref = refs[n_in - 2:n_in]
            tab_ref = refs[n_in + n_out - 1]
            ang = pos_ref[...].astype(F32) * coef_ref[0:1, :]
            tab_ref[:, :LANES] = jnp.cos(ang)
            tab_ref[:, LANES:] = jnp.sin(ang) * coef_ref[1:2, :]

    n_chunks = w_vmem.shape[0] // W_CHUNK
    w_src = w_hbm if w_layer is None else w_hbm.at[w_layer]

    def chunk_copy(c):
        rows = pl.ds(c * W_CHUNK, W_CHUNK)
        if staged:
            return pltpu.make_async_copy(w_src.at[rows, :], stage[0].at[c % 2], sems.at[c % 2])
        return pltpu.make_async_copy(w_src.at[rows, :], w_vmem.at[rows, :], sems.at[c])

    step = pl.program_id(0)

    @pl.when(step == 0)
    def _():
        xb = x_ref[...].astype(BF16)
        for c in range(2 if staged else n_chunks):
            chunk_copy(c).start()
        side_jobs()
        for c in range(n_chunks):
            rows = slice(c * W_CHUNK, (c + 1) * W_CHUNK)
            chunk_copy(c).wait()
            if staged:
                w_vmem[rows, :] = stage[0][c % 2].astype(BF16)
                if c + 2 < n_chunks:
                    chunk_copy(c + 2).start()
            part = jnp.dot(xb[:, rows], w_vmem[rows, :], preferred_element_type=F32)
            if c == 0:
                o_ref[...] = part
            else:
                o_ref[...] += part

    @pl.when(step > 0)
    def _():
        side_jobs()
        o_ref[...] = jnp.dot(x_ref[...].astype(BF16), w_vmem[...], preferred_element_type=F32)


def _proj_in(x, w, w_layer, w_out, layer, w_next, layer_next, positions=None):
    m, k = x.shape
    n = w.shape[-1]
    steps = m // PROJ_TM

    def slab_specs(stacked, which):
        rows, cols = stacked.shape[1:]
        slab = rows // steps
        return (pl.BlockSpec((None, slab, cols), lambda i: (which, i, 0)),
                pl.BlockSpec((slab, cols), lambda i: (i, 0)),
                jax.ShapeDtypeStruct((rows, cols), BF16))

    k_out = w_out.shape[1]
    slab_out = k_out // steps
    casts = [(pl.BlockSpec((None, slab_out, D_MODEL), lambda i: (layer, i, 0)),
              pl.BlockSpec((OUT_BLOCKS, slab_out, OUT_BLOCK), lambda i: (0, i, 0)),
              jax.ShapeDtypeStruct((OUT_BLOCKS, k_out, OUT_BLOCK), BF16))]
    args = [x, w, w_out]
    if w_next is not None:
        casts.append(slab_specs(w_next, layer_next))
        args.append(w_next)
    rope = positions is not None
    if rope:
        casts.append((pl.BlockSpec((PROJ_TM, 1), lambda i: (i, 0)), None, None))
        coef = _rope_coefficients()
        casts.append((_whole(coef), pl.BlockSpec((PROJ_TM, 2 * LANES), lambda i: (i, 0)),
                      jax.ShapeDtypeStruct((m, 2 * LANES), F32)))
        args += [positions.reshape(m, 1), coef]
    staged = w.dtype != BF16
    scratch = [pltpu.VMEM((k, n), BF16)]
    if staged:
        scratch.append(pltpu.VMEM((2, W_CHUNK, n), w.dtype))
    scratch.append(pltpu.SemaphoreType.DMA((2 if staged else k // W_CHUNK,)))
    outs = pl.pallas_call(
        partial(_proj_in_kernel, cast_next=w_next is not None, w_layer=w_layer, rope=rope),
        grid=(steps,),
        in_specs=[pl.BlockSpec((PROJ_TM, k), lambda i: (i, 0)), pl.BlockSpec(memory_space=pl.ANY)]
        + [c[0] for c in casts],
        out_specs=[pl.BlockSpec((PROJ_TM, n), lambda i: (i, 0))] + [c[1] for c in casts if c[1]],
        out_shape=[jax.ShapeDtypeStruct((m, n), F32)] + [c[2] for c in casts if c[2]],
        scratch_shapes=scratch,
        compiler_params=_compiler_params("arbitrary"),
        name="proj_in",
    )(*args)
    h, w_out_bf, *rest = outs
    return h, w_out_bf, rest.pop(0) if w_next is not None else None, rest.pop(0) if rope else None


class _Tail:
    def __init__(self, mix_prev, wout_ref, x_ref, z_ref, z_prev, g_ref, b_ref, layer, o_ref,
                 proj, ln):
        self.mix = mix_prev[...] if proj else None
        self.proj_refs = (wout_ref, x_ref, z_ref)
        self.ln_refs = (z_prev, g_ref, b_ref, layer, o_ref)
        self.blocks_left = list(range(OUT_BLOCKS)) if proj else []
        rows = z_ref.shape[0]
        self.halves_left = [slice(r * rows // 2, (r + 1) * rows // 2) for r in range(2)] if ln else []

    def emit(self, count=1):
        wout_ref, x_ref, z_ref = self.proj_refs
        for _ in range(min(count, len(self.blocks_left))):
            blk = self.blocks_left.pop(0)
            cols = slice(blk * OUT_BLOCK, (blk + 1) * OUT_BLOCK)
            z_ref[:, cols] = DN_ALPHA * x_ref[:, cols] + jnp.dot(
                self.mix, wout_ref[blk], preferred_element_type=F32)

    def ln_half(self):
        if self.halves_left:
            z_prev, g_ref, b_ref, layer, o_ref = self.ln_refs
            rows = self.halves_left.pop(0)
            z = z_prev[rows, :]
            mu = jnp.mean(z, axis=-1, keepdims=True)
            zc = z - mu
            var = jnp.mean(zc * zc, axis=-1, keepdims=True)
            o_ref[rows, :] = (zc * lax.rsqrt(var + LN_EPS) * g_ref[layer:layer + 1, :]
                              + b_ref[layer:layer + 1, :])

    def finish(self):
        self.ln_half()
        self.ln_half()
        self.emit(OUT_BLOCKS)


def _lagged_tiles(n_tiles):
    def clamp(i):
        return jnp.clip(i, 0, n_tiles - 1)

    return clamp, (lambda i: clamp(i - 1)), (lambda i: clamp(i - 2))


def _run_stages(step, n_tiles, body, wout_hbm, wout_vmem, wout_sem, mix_a, mix_b, z_a, z_b):
    wout_copy = pltpu.make_async_copy(wout_hbm, wout_vmem, wout_sem.at[0])

    @pl.when(step == 0)
    def _():
        wout_copy.start()

    @pl.when(step == 1)
    def _():
        wout_copy.wait()

    def stage(pred, parity, **stages):
        bufs = (mix_a, mix_b, z_a, z_b) if parity == 0 else (mix_b, mix_a, z_b, z_a)

        @pl.when(pred)
        def _():
            body(*bufs, **stages)

    steady = (step >= 2) & (step < n_tiles)
    stage(step == 0, 0, mix=True, proj=False, ln=False)
    stage(step == 1, 1, mix=True, proj=True, ln=False)
    stage(steady & (step % 2 == 0), 0, mix=True, proj=True, ln=True)
    stage(steady & (step % 2 == 1), 1, mix=True, proj=True, ln=True)
    stage(step == n_tiles, n_tiles % 2, mix=False, proj=True, ln=True)
    stage(step == n_tiles + 1, (n_tiles + 1) % 2, mix=False, proj=False, ln=True)


def _tail_specs(x, w_out_bf, ln_g, ln_b, res, out):
    m, d = x.shape
    in_specs = [
        pl.BlockSpec((MIX_T, d), lambda i: (res(i), 0)),
        pl.BlockSpec(memory_space=pl.ANY),
        _whole(ln_g),
        _whole(ln_b),
    ]
    out_spec = pl.BlockSpec((MIX_T, d), lambda i: (out(i), 0))
    out_shape = jax.ShapeDtypeStruct((m, d), F32)
    scratch = [pltpu.VMEM((MIX_T, MIX_WIDTH), BF16), pltpu.VMEM((MIX_T, MIX_WIDTH), BF16),
               pltpu.VMEM((MIX_T, d), F32), pltpu.VMEM((MIX_T, d), F32),
               pltpu.VMEM(w_out_bf.shape, BF16), pltpu.SemaphoreType.DMA((1,))]
    return in_specs, out_spec, out_shape, scratch, [x, w_out_bf, ln_g, ln_b]


def _rope_tile(t, cos, sin, take_upper):
    upper = pltpu.roll(t, LANES - ROT_DIM // 2, 1)
    lower = pltpu.roll(t, ROT_DIM // 2, 1)
    return t * cos + jnp.where(take_upper, upper, lower) * sin


EVEN_WOUT_ARG = 10
ODD_WOUT_ARG = 11


def _even_layer_kernel(*refs, layer, n_tiles, n_s):
    *io_refs, mix_a, mix_b, z_a, z_b, wout_vmem, wout_sem = refs
    wout_hbm = io_refs[EVEN_WOUT_ARG]
    io_refs[EVEN_WOUT_ARG] = wout_vmem
    step = pl.program_id(0)
    seq_tile = jnp.minimum(step, n_tiles - 1) % n_s
    _run_stages(step, n_tiles, partial(_even_step, *io_refs, layer=layer, seq_tile=seq_tile),
                wout_hbm, wout_vmem, wout_sem, mix_a, mix_b, z_a, z_b)


def _even_step(sinks_ref, h_ref, kvp_ref, tabc_ref, tabp_ref, lng_ref, lnb_ref, ws_ref, bst_ref,
               x_ref, wout_ref, g_ref, b_ref, o_ref, mix_ref, mix_prev, z_ref, z_prev,
               *, layer, seq_tile, mix, proj, ln):
    u_ref, v_ref, ag_ref, q_ref = (h_ref.at[:, c * A_WIDTH:(c + 1) * A_WIDTH] for c in range(4))
    kvc_ref = h_ref.at[:, KV_COL * KV_BLOCK:(KV_COL + 1) * KV_BLOCK]
    bg_refs = [h_ref.at[:, (GATE_B_COL + c) * KV_BLOCK:(GATE_B_COL + c + 1) * KV_BLOCK]
               for c in range(B_WIDTH // KV_BLOCK)]
    cosc_ref, sinc_ref = tabc_ref.at[:, :LANES], tabc_ref.at[:, LANES:]
    cosp_ref, sinp_ref = tabp_ref.at[:, :LANES], tabp_ref.at[:, LANES:]
    t_rows = h_ref.shape[0]
    n_chunks = t_rows // CHUNK
    tail = _Tail(mix_prev, wout_ref, x_ref, z_ref, z_prev, g_ref, b_ref, layer, o_ref, proj, ln)
    if not mix:
        tail.finish()
        return
    tail.emit()

    v = v_ref[...]
    mu = jnp.mean(v, axis=-1, keepdims=True)
    vc = v - mu
    var = jnp.mean(vc * vc, axis=-1, keepdims=True)
    vln = (vc * lax.rsqrt(var + LN_EPS) * lng_ref[layer:layer + 1, :]
           + lnb_ref[layer:layer + 1, :]).astype(BF16)
    causal = (lax.broadcasted_iota(jnp.int32, (CHUNK, CHUNK), 0)
              >= lax.broadcasted_iota(jnp.int32, (CHUNK, CHUNK), 1))
    lane = lax.broadcasted_iota(jnp.int32, (1, LANES), 1)
    take_upper = (lane % B_HEAD_DIM) < (ROT_DIM // 2)
    low_half = lane < B_HEAD_DIM
    cosc = cosc_ref[...]
    sinc = sinc_ref[...]
    scale = B_HEAD_DIM ** -0.5
    q_tiles = []
    for g in range(A_GROUPS):
        cols = slice(g * CHUNK, (g + 1) * CHUNK)
        w = jnp.where(causal, ws_ref[g], 0.0).astype(BF16)
        rhs = jnp.concatenate(
            [vln[c * CHUNK:(c + 1) * CHUNK, cols] for c in range(n_chunks)], axis=1)
        mixed = jnp.dot(w, rhs, preferred_element_type=F32) + bst_ref[:, g:g + 1]
        for c in range(n_chunks):
            rows = slice(c * CHUNK, (c + 1) * CHUNK)
            gate = _silu(ag_ref[rows, cols])
            mix_ref[rows, cols] = (u_ref[rows, cols] * mixed[:, c * CHUNK:(c + 1) * CHUNK]
                                   * gate).astype(BF16)
        q_tiles.append(
            (_rope_tile(q_ref[:, cols], cosc, sinc, take_upper) * scale).astype(BF16))
        if g in (1, 5):
            tail.ln_half()
        if g in (2, 4, 6):
            tail.emit()

    k_all = jnp.concatenate(
        [_rope_tile(kvp_ref[:, :B_KV_WIDTH], cosp_ref[...], sinp_ref[...], take_upper),
         _rope_tile(kvc_ref[:, :B_KV_WIDTH], cosc, sinc, take_upper)], axis=0)
    k_swap = pltpu.roll(k_all, B_HEAD_DIM, 1)
    zero = jnp.zeros_like(k_all)
    k_lo = (jnp.where(low_half, k_all, zero).astype(BF16), jnp.where(low_half, k_swap, zero).astype(BF16))
    k_hi = (jnp.where(low_half, zero, k_swap).astype(BF16), jnp.where(low_half, zero, k_all).astype(BF16))
    v_all = jnp.concatenate([kvp_ref[:, B_KV_WIDTH:], kvc_ref[:, B_KV_WIDTH:]], axis=0)
    vt_all = v_all.T.astype(BF16)

    tiles_per_kv = B_WIDTH // LANES // B_KV_HEADS
    band_keys = 2 * WINDOW
    kj = lax.broadcasted_iota(jnp.int32, (band_keys, WINDOW), 0)
    qi = lax.broadcasted_iota(jnp.int32, (band_keys, WINDOW), 1)
    diff = qi + WINDOW - kj
    band = (diff >= 0) & (diff < WINDOW)
    first_key = jnp.where(seq_tile == 0, WINDOW, 0)
    neg_inf = jnp.full((band_keys, WINDOW), -jnp.inf, F32)
    mask_any = jnp.where(band, 0.0, neg_inf)
    mask_first = jnp.where(band & (kj >= first_key), 0.0, neg_inf)

    for n in range(t_rows // WINDOW):
        rows = slice(n * WINDOW, (n + 1) * WINDOW)
        band_rows = slice(n * WINDOW, (n + 2) * WINDOW)
        mask = jnp.concatenate([mask_first if n == 0 else mask_any] * tiles_per_kv, axis=1)
        for kv in range(B_KV_HEADS):
            tail.emit()
            q4 = jnp.concatenate(
                [q_tiles[kv * tiles_per_kv + j][rows] for j in range(tiles_per_kv)], axis=0)
            k_cat = jnp.concatenate([k_lo[kv][band_rows], k_hi[kv][band_rows]], axis=0)
            st = lax.dot_general(k_cat, q4, (((1,), (1,)), ((), ())), preferred_element_type=F32)
            vt = vt_all[kv * B_HEAD_DIM:(kv + 1) * B_HEAD_DIM, band_rows]
            o_halves = []
            for half in range(2):
                sh = st[half * band_keys:(half + 1) * band_keys] + mask
                sink = jnp.concatenate(
                    [jnp.full((1, WINDOW), sinks_ref[layer, 2 * (kv * tiles_per_kv + j) + half], F32)
                     for j in range(tiles_per_kv)], axis=1)
                m = jnp.maximum(jnp.max(sh, axis=0, keepdims=True), sink)
                p = jnp.exp(sh - m)
                l = jnp.sum(p, axis=0, keepdims=True) + jnp.exp(sink - m)
                o_half = jnp.dot(vt, p.astype(BF16), preferred_element_type=F32)
                o_halves.append(o_half * (1.0 / l))
            ot = jnp.concatenate(o_halves, axis=0)
            for j in range(tiles_per_kv):
                jj = kv * tiles_per_kv + j
                gate_ref = bg_refs[jj // 2]
                gate_cols = slice((jj % 2) * LANES, (jj % 2 + 1) * LANES)
                out_cols = slice(A_WIDTH + jj * LANES, A_WIDTH + (jj + 1) * LANES)
                o_tile = ot[:, j * WINDOW:(j + 1) * WINDOW].T
                mix_ref[rows, out_cols] = (o_tile * _silu(gate_ref[rows, gate_cols])).astype(BF16)

    tail.finish()


def _even_layer(h, x, sinks, rope_tab, a_ln_g, a_ln_b, ws, bst, w_out_bf, ln_g, ln_b, layer, seq):
    m = h.shape[0]
    t = MIX_T
    n_tiles = m // t
    n_s = seq // t
    blocks_per_tile = t // WINDOW
    cur, res, out = _lagged_tiles(n_tiles)

    def prev(i):
        return jnp.maximum(cur(i) * blocks_per_tile - 1, 0)

    tail_in, out_specs, out_shape, tail_scratch, tail_args = _tail_specs(x, w_out_bf, ln_g, ln_b, res, out)
    return pl.pallas_call(
        partial(_even_layer_kernel, layer=layer, n_tiles=n_tiles, n_s=n_s),
        grid=(n_tiles + 2,),
        in_specs=[
            pl.BlockSpec(memory_space=pltpu.SMEM),
            pl.BlockSpec((t, h.shape[1]), lambda i: (cur(i), 0)),
            pl.BlockSpec((WINDOW, KV_BLOCK), lambda i: (prev(i), KV_COL)),
            pl.BlockSpec((t, 2 * LANES), lambda i: (cur(i), 0)),
            pl.BlockSpec((WINDOW, 2 * LANES), lambda i: (prev(i), 0)),
            _whole(a_ln_g),
            _whole(a_ln_b),
            _layer_block((A_GROUPS, CHUNK, CHUNK), layer),
            _layer_block((CHUNK, A_GROUPS), layer),
        ] + tail_in,
        out_specs=out_specs,
        out_shape=out_shape,
        scratch_shapes=tail_scratch,
        compiler_params=_compiler_params("arbitrary"),
        name="even_layer",
    )(sinks, h, h, rope_tab, rope_tab, a_ln_g, a_ln_b, ws, bst, *tail_args)


def _odd_layer_kernel(*refs, layer, n_tiles, n_s):
    *io_refs, mix_a, mix_b, z_a, z_b, wout_vmem, wout_sem = refs[:21]
    (ext_c, ext_d, lvl2, lvl4, hstate, sa, sb, sh, wax_bf, wpool_bf) = scratch = refs[21:]
    wout_hbm = io_refs[ODD_WOUT_ARG]
    io_refs[ODD_WOUT_ARG] = wout_vmem
    wa_ref, wx_ref, wpool_ref = io_refs[3], io_refs[4], io_refs[8]
    t_rows = io_refs[0].shape[0]
    step = pl.program_id(0)
    s_idx = jnp.minimum(step, n_tiles - 1) % n_s

    @pl.when(s_idx == 0)
    def _():
        ext_c[:, 0:CONV_HALO, :] = jnp.zeros((C_HEADS, CONV_HALO, LANES), F32)
        ext_d[:, 0:POOL_HALO, :] = jnp.zeros((D_WIDTH // LANES, POOL_HALO, LANES), F32)
        hstate[...] = jnp.zeros_like(hstate)
        wax_bf[:, :, :C_HEAD_DIM] = wa_ref[...].astype(BF16)
        wax_bf[:, :, C_HEAD_DIM:] = wx_ref[...].astype(BF16)
        wpool_bf[...] = wpool_ref[...].astype(BF16)

    @pl.when(s_idx > 0)
    def _():
        ext_c[:, 0:CONV_HALO, :] = ext_c[:, t_rows:t_rows + CONV_HALO, :]
        ext_d[:, 0:POOL_HALO, :] = ext_d[:, t_rows:t_rows + POOL_HALO, :]

    _run_stages(step, n_tiles, partial(_odd_step, *io_refs, *scratch, layer=layer, s_idx=s_idx),
                wout_hbm, wout_vmem, wout_sem, mix_a, mix_b, z_a, z_b)


def _odd_step(h_ref, convw_ref, convb_ref, wa_ref, wx_ref, ba_ref, bx_ref,
              lam_ref, wpool_ref, dscale_ref, x_ref, wout_ref, g_ref, b_ref, o_ref,
              ext_c, ext_d, lvl2, lvl4, hstate, sa, sb, sh, wax_bf, wpool_bf,
              mix_ref, mix_prev, z_ref, z_prev, *, layer, s_idx, mix, proj, ln):
    xc_ref, cg_ref, xd_ref, dg_ref = (h_ref.at[:, c * C_WIDTH:(c + 1) * C_WIDTH] for c in range(4))
    t_rows = h_ref.shape[0]
    row = slice(layer, layer + 1)
    n_groups = t_rows // SUBLANES
    tail = _Tail(mix_prev, wout_ref, x_ref, z_ref, z_prev, g_ref, b_ref, layer, o_ref, proj, ln)
    if not mix:
        tail.finish()
        return
    tail.emit()

    z = -lam_ref[row, :]
    softplus = jnp.maximum(z, 0.0) + jnp.log1p(jnp.exp(-jnp.abs(z)))
    for hd in range(C_HEADS):
        cols = slice(hd * C_HEAD_DIM, (hd + 1) * C_HEAD_DIM)
        ext_c[hd, CONV_HALO:, :] = xc_ref[:, cols]
        xconv = convb_ref[row, cols]
        for j in range(CONV_WIDTH):
            lo = CONV_HALO - (CONV_WIDTH - 1) + j
            xconv = xconv + ext_c[hd, lo:lo + t_rows, :] * convw_ref[j:j + 1, cols]
        ri = jnp.dot(xconv.astype(BF16), wax_bf[hd], preferred_element_type=F32)
        r = _sigmoid(ri[:, :C_HEAD_DIM] + ba_ref[row, cols])
        i = _sigmoid(ri[:, C_HEAD_DIM:] + bx_ref[row, cols])
        log_a = -LRU_C * r * softplus[:, cols]
        a = jnp.exp(log_a)
        m2 = -jnp.tanh(log_a) * (a * a + 1.0)
        mult = jnp.where(m2 == 0.0, 0.0, m2 * lax.rsqrt(m2))
        b = mult * (i * xconv)
        for g in range(n_groups):
            base = (g * C_HEADS + hd) * SUBLANES
            sa[base:base + SUBLANES, :] = a[g * SUBLANES:(g + 1) * SUBLANES]
            sb[base:base + SUBLANES, :] = b[g * SUBLANES:(g + 1) * SUBLANES]
        if hd in (2, 5):
            tail.ln_half()
        if hd in (1, 3, 4, 6, 7):
            tail.emit()

    h = hstate[...]
    for t in range(t_rows):
        g, s = divmod(t, SUBLANES)
        at_t = pl.ds(g * C_HEADS * SUBLANES + s, C_HEADS, stride=SUBLANES)
        h = sa[at_t, :] * h + sb[at_t, :]
        sh[at_t, :] = h
    hstate[...] = h
    tail.emit()
    for hd in range(C_HEADS):
        cols = slice(hd * C_HEAD_DIM, (hd + 1) * C_HEAD_DIM)
        hcol = jnp.concatenate(
            [sh[(g * C_HEADS + hd) * SUBLANES:(g * C_HEADS + hd + 1) * SUBLANES, :]
             for g in range(n_groups)], axis=0)
        mix_ref[:, cols] = (hcol * _silu(cg_ref[:, cols])).astype(BF16)

    n_ext = POOL_HALO + t_rows
    tiles_per_group = D_GROUP_DIM // LANES
    pos1 = s_idx * t_rows + lax.broadcasted_iota(jnp.int32, (t_rows, LANES), 0) + 1
    for g, w in enumerate(POOL_WINDOWS):
        if g == 2:
            tail.emit()
        inv = 1.0 / jnp.minimum(pos1, w).astype(F32)
        pooled = []
        for k in range(tiles_per_group):
            tile = g * tiles_per_group + k
            cols = slice(tile * LANES, (tile + 1) * LANES)
            ext_d[tile, POOL_HALO:, :] = xd_ref[:, cols]
            lo = SUBLANES
            win = ext_d[tile, lo:n_ext, :] + ext_d[tile, lo - 1:n_ext - 1, :]
            if w >= 4:
                lvl2[k, lo:n_ext, :] = win
                lo += SUBLANES
                win = lvl2[k, lo:n_ext, :] + lvl2[k, lo - 2:n_ext - 2, :]
            if w >= 8:
                lvl4[k, lo:n_ext, :] = win
                lo += SUBLANES
                win = lvl4[k, lo:n_ext, :] + lvl4[k, lo - 4:n_ext - 4, :]
            if w >= 16:
                lo += SUBLANES
                win = win[SUBLANES:] + win[:-SUBLANES]
            pooled.append((win[POOL_HALO - lo:] * inv - xd_ref[:, cols]).astype(BF16))
        cols = slice(g * D_GROUP_DIM, (g + 1) * D_GROUP_DIM)
        mixed = jnp.dot(jnp.concatenate(pooled, axis=1), wpool_bf[g], preferred_element_type=F32)
        out_cols = slice(C_WIDTH + g * D_GROUP_DIM, C_WIDTH + (g + 1) * D_GROUP_DIM)
        mix_ref[:, out_cols] = (mixed * dscale_ref[row, cols] * _silu(dg_ref[:, cols])).astype(BF16)

    tail.finish()


def _odd_layer(h, x, conv_w, conv_b, w_a, w_x, b_a, b_x, lam, w_pool, d_scale, w_out_bf, ln_g, ln_b,
               layer, seq):
    m = h.shape[0]
    t = MIX_T
    n_tiles = m // t
    n_s = seq // t
    cur, res, out = _lagged_tiles(n_tiles)
    n_pool = len(POOL_WINDOWS)
    tail_in, out_specs, out_shape, tail_scratch, tail_args = _tail_specs(x, w_out_bf, ln_g, ln_b, res, out)
    return pl.pallas_call(
        partial(_odd_layer_kernel, layer=layer, n_tiles=n_tiles, n_s=n_s),
        grid=(n_tiles + 2,),
        in_specs=[
            pl.BlockSpec((t, h.shape[1]), lambda i: (cur(i), 0)),
            _layer_block((CONV_WIDTH, C_WIDTH), layer),
            _whole(conv_b),
            _layer_block((C_HEADS, C_HEAD_DIM, C_HEAD_DIM), layer),
            _layer_block((C_HEADS, C_HEAD_DIM, C_HEAD_DIM), layer),
            _whole(b_a), _whole(b_x), _whole(lam),
            _layer_block((n_pool, D_GROUP_DIM, D_GROUP_DIM), layer),
            _whole(d_scale),
        ] + tail_in,
        out_specs=out_specs,
        out_shape=out_shape,
        scratch_shapes=tail_scratch + [
            pltpu.VMEM((C_WIDTH // LANES, CONV_HALO + t, LANES), F32),
            pltpu.VMEM((D_WIDTH // LANES, POOL_HALO + t, LANES), F32),
            pltpu.VMEM((D_GROUP_DIM // LANES, POOL_HALO + t, LANES), F32),
            pltpu.VMEM((D_GROUP_DIM // LANES, POOL_HALO + t, LANES), F32),
            pltpu.VMEM((SUBLANES, LANES), F32),
            pltpu.VMEM((t * C_HEADS, LANES), F32),
            pltpu.VMEM((t * C_HEADS, LANES), F32),
            pltpu.VMEM((t * C_HEADS, LANES), F32),
            pltpu.VMEM((C_HEADS, C_HEAD_DIM, 2 * C_HEAD_DIM), BF16),
            pltpu.VMEM((n_pool, D_GROUP_DIM, D_GROUP_DIM), BF16),
        ],
        compiler_params=_compiler_params("arbitrary"),
        name="odd_layer",
    )(h, conv_w, conv_b, w_a, w_x, b_a, b_x, lam, w_pool, d_scale, *tail_args)


def _rope_coefficients():
    half = ROT_DIM // 2
    inv_freq = ROPE_THETA ** (-jnp.arange(0, ROT_DIM, 2, dtype=F32) / ROT_DIM)
    rest = jnp.zeros((B_HEAD_DIM - ROT_DIM,), F32)
    freq_head = jnp.concatenate([inv_freq, inv_freq, rest])
    sign_head = jnp.concatenate([-jnp.ones((half,), F32), jnp.ones((half,), F32), rest])
    reps = LANES // B_HEAD_DIM
    return jnp.stack([jnp.tile(freq_head, reps), jnp.tile(sign_head, reps)])


def kernel(x, positions, even_w_in, even_a_ln_g, even_a_ln_b, even_a_ws, even_a_bs, even_b_sinks, even_w_out, even_ln_g, even_ln_b, odd_w_in, odd_conv_w, odd_conv_b, odd_w_a, odd_b_a, odd_w_x, odd_b_x, odd_lam, odd_w_pool, odd_d_scale, odd_w_out, odd_ln_g, odd_ln_b):
    batch, seq, d = x.shape
    m = batch * seq
    xf = x.reshape(m, d)
    rope_tab = None
    even_bst = jnp.swapaxes(even_a_bs, 1, 2)
    w_in, w_layer = even_w_in, 0
    for layer in range(DEPTH):
        j = layer // 2
        if layer % 2 == 0:
            h, w_out, w_in, tab = _proj_in(xf, w_in, w_layer, even_w_out, j, odd_w_in, j,
                                           positions if rope_tab is None else None)
            rope_tab = tab if rope_tab is None else rope_tab
            xf = _even_layer(h, xf, even_b_sinks, rope_tab, even_a_ln_g, even_a_ln_b, even_a_ws,
                             even_bst, w_out, even_ln_g, even_ln_b, j, seq)
        else:
            w_next = even_w_in if layer + 1 < DEPTH else None
            h, w_out, w_in, _ = _proj_in(xf, w_in, w_layer, odd_w_out, j, w_next, j + 1)
            xf = _odd_layer(h, xf, odd_conv_w, odd_conv_b, odd_w_a, odd_w_x, odd_b_a, odd_b_x, odd_lam,
                            odd_w_pool, odd_d_scale, w_out, odd_ln_g, odd_ln_b, j, seq)
        w_layer = None
    return xf.reshape(batch, seq, d)
```

```python
from functools import partial

import jax
import jax.numpy as jnp
from jax import lax
from jax.experimental import pallas as pl
from jax.experimental.pallas import tpu as pltpu

D_MODEL = 2048
DEPTH = 4
A_WIDTH = 1024
A_GROUPS = 8
CHUNK = 128
B_HEAD_DIM = 64
B_Q_HEADS = 16
B_KV_HEADS = 2
B_WIDTH = 1024
B_KV_WIDTH = 128
WINDOW = 128
ROT_DIM = 16
ROPE_THETA = 500000.0
C_WIDTH = 1024
C_HEADS = 8
C_HEAD_DIM = 128
CONV_WIDTH = 4
LRU_C = 8.0
D_WIDTH = 1024
POOL_WINDOWS = (2, 4, 8, 16)
D_GROUP_DIM = 256
MIX_WIDTH = 2048
DN_ALPHA = (2 * DEPTH) ** 0.25
LN_EPS = 1e-5

LANES = 128
SUBLANES = 8
VMEM_LIMIT_BYTES = 56 * 1024 * 1024

PROJ_TM = 256
MIX_T = 256
OUT_BLOCK = 256
OUT_BLOCKS = D_MODEL // OUT_BLOCK
W_CHUNK = 256
CONV_HALO = SUBLANES
POOL_HALO = 4 * SUBLANES
SCAN_PITCH = SUBLANES + 1

KV_BLOCK = 2 * B_KV_WIDTH
KV_COL = (3 * A_WIDTH + B_WIDTH) // KV_BLOCK
GATE_B_COL = KV_COL + 1

F32 = jnp.float32
BF16 = jnp.bfloat16


def _sigmoid(x):
    return 0.5 * jnp.tanh(0.5 * x) + 0.5


def _silu(x):
    hx = 0.5 * x
    return hx * jnp.tanh(hx) + hx


def _resident(shape):
    return pl.BlockSpec(shape, lambda *_: (0,) * len(shape), pipeline_mode=pl.Buffered(1))


def _whole(arr):
    return pl.BlockSpec(arr.shape, lambda *_: (0,) * arr.ndim)


def _layer_block(shape, layer):
    return pl.BlockSpec((None,) + shape, lambda *_: (layer,) + (0,) * len(shape))


def _compiler_params(*semantics):
    return pltpu.CompilerParams(dimension_semantics=semantics, vmem_limit_bytes=VMEM_LIMIT_BYTES)


def _proj_in_kernel(*refs, cast_next, w_layer, rope):
    n_in = 3 + cast_next + 2 * rope
    n_out = 2 + cast_next + rope
    x_ref, w_hbm, wout_ref = refs[:3]
    o_ref, wout_bf_ref = refs[n_in:n_in + 2]
    staged = w_hbm.dtype != BF16
    w_vmem, *stage, sems = refs[n_in + n_out:]

    def side_jobs():
        if cast_next:
            refs[n_in + 2][...] = refs[3][...].astype(BF16)
        for p in range(OUT_BLOCKS):
            wout_bf_ref[p] = wout_ref[:, p * OUT_BLOCK:(p + 1) * OUT_BLOCK].astype(BF16)
        if rope:
            pos_ref, coef_ref = refs[n_in - 2:n_in]
            tab_ref = refs[n_in + n_out - 1]
            ang = pos_ref[...].astype(F32) * coef_ref[0:1, :]
            tab_ref[:, :LANES] = jnp.cos(ang)
            tab_ref[:, LANES:] = jnp.sin(ang) * coef_ref[1:2, :]

    n_chunks = w_vmem.shape[0] // W_CHUNK
    w_src = w_hbm if w_layer is None else w_hbm.at[w_layer]

    def chunk_copy(c):
        rows = pl.ds(c * W_CHUNK, W_CHUNK)
        if staged:
            return pltpu.make_async_copy(w_src.at[rows, :], stage[0].at[c % 2], sems.at[c % 2])
        return pltpu.make_async_copy(w_src.at[rows, :], w_vmem.at[rows, :], sems.at[c])

    step = pl.program_id(0)

    @pl.when(step == 0)
    def _():
        xb = x_ref[...].astype(BF16)
        for c in range(2 if staged else n_chunks):
            chunk_copy(c).start()
        side_jobs()
        for c in range(n_chunks):
            rows = slice(c * W_CHUNK, (c + 1) * W_CHUNK)
            chunk_copy(c).wait()
            if staged:
                w_vmem[rows, :] = stage[0][c % 2].astype(BF16)
                if c + 2 < n_chunks:
                    chunk_copy(c + 2).start()
            part = jnp.dot(xb[:, rows], w_vmem[rows, :], preferred_element_type=F32)
            if c == 0:
                o_ref[...] = part
            else:
                o_ref[...] += part

    @pl.when(step > 0)
    def _():
        side_jobs()
        o_ref[...] = jnp.dot(x_ref[...].astype(BF16), w_vmem[...], preferred_element_type=F32)


def _proj_in(x, w, w_layer, w_out, layer, w_next, layer_next, positions=None):
    m, k = x.shape
    n = w.shape[-1]
    steps = m // PROJ_TM

    def slab_specs(stacked, which):
        rows, cols = stacked.shape[1:]
        slab = rows // steps
        return (pl.BlockSpec((None, slab, cols), lambda i: (which, i, 0)),
                pl.BlockSpec((slab, cols), lambda i: (i, 0)),
                jax.ShapeDtypeStruct((rows, cols), BF16))

    k_out = w_out.shape[1]
    slab_out = k_out // steps
    casts = [(pl.BlockSpec((None, slab_out, D_MODEL), lambda i: (layer, i, 0)),
              pl.BlockSpec((OUT_BLOCKS, slab_out, OUT_BLOCK), lambda i: (0, i, 0)),
              jax.ShapeDtypeStruct((OUT_BLOCKS, k_out, OUT_BLOCK), BF16))]
    args = [x, w, w_out]
    if w_next is not None:
        casts.append(slab_specs(w_next, layer_next))
        args.append(w_next)
    rope = positions is not None
    if rope:
        casts.append((pl.BlockSpec((PROJ_TM, 1), lambda i: (i, 0)), None, None))
        coef = _rope_coefficients()
        casts.append((_whole(coef), pl.BlockSpec((PROJ_TM, 2 * LANES), lambda i: (i, 0)),
                      jax.ShapeDtypeStruct((m, 2 * LANES), F32)))
        args += [positions.reshape(m, 1), coef]
    staged = w.dtype != BF16
    scratch = [pltpu.VMEM((k, n), BF16)]
    if staged:
        scratch.append(pltpu.VMEM((2, W_CHUNK, n), w.dtype))
    scratch.append(pltpu.SemaphoreType.DMA((2 if staged else k // W_CHUNK,)))
    outs = pl.pallas_call(
        partial(_proj_in_kernel, cast_next=w_next is not None, w_layer=w_layer, rope=rope),
        grid=(steps,),
        in_specs=[pl.BlockSpec((PROJ_TM, k), lambda i: (i, 0)), pl.BlockSpec(memory_space=pl.ANY)]
        + [c[0] for c in casts],
        out_specs=[pl.BlockSpec((PROJ_TM, n), lambda i: (i, 0))] + [c[1] for c in casts if c[1]],
        out_shape=[jax.ShapeDtypeStruct((m, n), F32)] + [c[2] for c in casts if c[2]],
        scratch_shapes=scratch,
        compiler_params=_compiler_params("arbitrary"),
        name="proj_in",
    )(*args)
    h, w_out_bf, *rest = outs
    return h, w_out_bf, rest.pop(0) if w_next is not None else None, rest.pop(0) if rope else None


class _Tail:
    def __init__(self, mix_prev, wout_ref, x_ref, z_ref, z_prev, g_ref, b_ref, layer, o_ref,
                 proj, ln):
        self.mix_ref = mix_prev
        self.proj_refs = (wout_ref, x_ref, z_ref)
        self.ln_refs = (z_prev, g_ref, b_ref, layer, o_ref)
        self.blocks_left = list(range(OUT_BLOCKS)) if proj else []
        rows = z_ref.shape[0]
        self.halves_left = [slice(r * rows // 2, (r + 1) * rows // 2) for r in range(2)] if ln else []

    def emit(self, count=1):
        wout_ref, x_ref, z_ref = self.proj_refs
        for _ in range(min(count, len(self.blocks_left))):
            blk = self.blocks_left.pop(0)
            cols = slice(blk * OUT_BLOCK, (blk + 1) * OUT_BLOCK)
            z_ref[:, cols] = DN_ALPHA * x_ref[:, cols] + jnp.dot(
                self.mix_ref[...], wout_ref[blk], preferred_element_type=F32)

    def ln_half(self):
        if self.halves_left:
            z_prev, g_ref, b_ref, layer, o_ref = self.ln_refs
            rows = self.halves_left.pop(0)
            z = z_prev[rows, :]
            mu = jnp.mean(z, axis=-1, keepdims=True)
            zc = z - mu
            var = jnp.mean(zc * zc, axis=-1, keepdims=True)
            o_ref[rows, :] = (zc * lax.rsqrt(var + LN_EPS) * g_ref[layer:layer + 1, :]
                              + b_ref[layer:layer + 1, :])

    def finish(self):
        self.ln_half()
        self.ln_half()
        self.emit(OUT_BLOCKS)


def _lagged_tiles(n_tiles):
    def clamp(i):
        return jnp.clip(i, 0, n_tiles - 1)

    return clamp, (lambda i: clamp(i - 1)), (lambda i: clamp(i - 2))


def _run_stages(step, n_tiles, body, wout_hbm, wout_vmem, wout_sem, mix_a, mix_b, z_a, z_b):
    wout_copy = pltpu.make_async_copy(wout_hbm, wout_vmem, wout_sem.at[0])

    @pl.when(step == 0)
    def _():
        wout_copy.start()

    @pl.when(step == 1)
    def _():
        wout_copy.wait()

    def stage(pred, parity, **stages):
        bufs = (mix_a, mix_b, z_a, z_b) if parity == 0 else (mix_b, mix_a, z_b, z_a)

        @pl.when(pred)
        def _():
            body(*bufs, **stages)

    steady = (step >= 2) & (step < n_tiles)
    stage(step == 0, 0, mix=True, proj=False, ln=False)
    stage(step == 1, 1, mix=True, proj=True, ln=False)
    stage(steady & (step % 2 == 0), 0, mix=True, proj=True, ln=True)
    stage(steady & (step % 2 == 1), 1, mix=True, proj=True, ln=True)
    stage(step == n_tiles, n_tiles % 2, mix=False, proj=True, ln=True)
    stage(step == n_tiles + 1, (n_tiles + 1) % 2, mix=False, proj=False, ln=True)


def _tail_specs(x, w_out_bf, ln_g, ln_b, res, out):
    m, d = x.shape
    in_specs = [
        pl.BlockSpec((MIX_T, d), lambda i: (res(i), 0)),
        pl.BlockSpec(memory_space=pl.ANY),
        _whole(ln_g),
        _whole(ln_b),
    ]
    out_spec = pl.BlockSpec((MIX_T, d), lambda i: (out(i), 0))
    out_shape = jax.ShapeDtypeStruct((m, d), F32)
    scratch = [pltpu.VMEM((MIX_T, MIX_WIDTH), BF16), pltpu.VMEM((MIX_T, MIX_WIDTH), BF16),
               pltpu.VMEM((MIX_T, d), F32), pltpu.VMEM((MIX_T, d), F32),
               pltpu.VMEM(w_out_bf.shape, BF16), pltpu.SemaphoreType.DMA((1,))]
    return in_specs, out_spec, out_shape, scratch, [x, w_out_bf, ln_g, ln_b]


def _rope_tile(t, cos, sin, take_upper):
    upper = pltpu.roll(t, LANES - ROT_DIM // 2, 1)
    lower = pltpu.roll(t, ROT_DIM // 2, 1)
    return t * cos + jnp.where(take_upper, upper, lower) * sin


EVEN_WOUT_ARG = 10
ODD_WOUT_ARG = 11


def _even_layer_kernel(*refs, layer, n_tiles, n_s):
    *io_refs, mix_a, mix_b, z_a, z_b, wout_vmem, wout_sem = refs
    wout_hbm = io_refs[EVEN_WOUT_ARG]
    io_refs[EVEN_WOUT_ARG] = wout_vmem
    step = pl.program_id(0)
    seq_tile = jnp.minimum(step, n_tiles - 1) % n_s
    _run_stages(step, n_tiles, partial(_even_step, *io_refs, layer=layer, seq_tile=seq_tile),
                wout_hbm, wout_vmem, wout_sem, mix_a, mix_b, z_a, z_b)


def _even_step(sinks_ref, h_ref, kvp_ref, tabc_ref, tabp_ref, lng_ref, lnb_ref, ws_ref, bst_ref,
               x_ref, wout_ref, g_ref, b_ref, o_ref, mix_ref, mix_prev, z_ref, z_prev,
               *, layer, seq_tile, mix, proj, ln):
    u_ref, v_ref, ag_ref, q_ref = (h_ref.at[:, c * A_WIDTH:(c + 1) * A_WIDTH] for c in range(4))
    kvc_ref = h_ref.at[:, KV_COL * KV_BLOCK:(KV_COL + 1) * KV_BLOCK]
    bg_refs = [h_ref.at[:, (GATE_B_COL + c) * KV_BLOCK:(GATE_B_COL + c + 1) * KV_BLOCK]
               for c in range(B_WIDTH // KV_BLOCK)]
    cosc_ref, sinc_ref = tabc_ref.at[:, :LANES], tabc_ref.at[:, LANES:]
    cosp_ref, sinp_ref = tabp_ref.at[:, :LANES], tabp_ref.at[:, LANES:]
    t_rows = h_ref.shape[0]
    n_chunks = t_rows // CHUNK
    tail = _Tail(mix_prev, wout_ref, x_ref, z_ref, z_prev, g_ref, b_ref, layer, o_ref, proj, ln)
    if not mix:
        tail.finish()
        return
    tail.emit()

    v = v_ref[...]
    mu = jnp.mean(v, axis=-1, keepdims=True)
    vc = v - mu
    var = jnp.mean(vc * vc, axis=-1, keepdims=True)
    vln = (vc * lax.rsqrt(var + LN_EPS) * lng_ref[layer:layer + 1, :]
           + lnb_ref[layer:layer + 1, :]).astype(BF16)
    causal = (lax.broadcasted_iota(jnp.int32, (CHUNK, CHUNK), 0)
              >= lax.broadcasted_iota(jnp.int32, (CHUNK, CHUNK), 1))
    lane = lax.broadcasted_iota(jnp.int32, (1, LANES), 1)
    take_upper = (lane % B_HEAD_DIM) < (ROT_DIM // 2)
    low_half = lane < B_HEAD_DIM
    cosc = cosc_ref[...]
    sinc = sinc_ref[...]
    scale = B_HEAD_DIM ** -0.5
    q_tiles = []
    for g in range(A_GROUPS):
        cols = slice(g * CHUNK, (g + 1) * CHUNK)
        w = jnp.where(causal, ws_ref[g], 0.0).astype(BF16)
        rhs = jnp.concatenate(
            [vln[c * CHUNK:(c + 1) * CHUNK, cols] for c in range(n_chunks)], axis=1)
        mixed = jnp.dot(w, rhs, preferred_element_type=F32) + bst_ref[:, g:g + 1]
        for c in range(n_chunks):
            rows = slice(c * CHUNK, (c + 1) * CHUNK)
            gate = _silu(ag_ref[rows, cols])
            mix_ref[rows, cols] = (u_ref[rows, cols] * mixed[:, c * CHUNK:(c + 1) * CHUNK]
                                   * gate).astype(BF16)
        q_tiles.append(
            (_rope_tile(q_ref[:, cols], cosc, sinc, take_upper) * scale).astype(BF16))
        if g in (1, 5):
            tail.ln_half()
        if g in (2, 4, 6):
            tail.emit()

    k_all = jnp.concatenate(
        [_rope_tile(kvp_ref[:, :B_KV_WIDTH], cosp_ref[...], sinp_ref[...], take_upper),
         _rope_tile(kvc_ref[:, :B_KV_WIDTH], cosc, sinc, take_upper)], axis=0)
    k_swap = pltpu.roll(k_all, B_HEAD_DIM, 1)
    zero = jnp.zeros_like(k_all)
    k_lo = (jnp.where(low_half, k_all, zero).astype(BF16), jnp.where(low_half, k_swap, zero).astype(BF16))
    k_hi = (jnp.where(low_half, zero, k_swap).astype(BF16), jnp.where(low_half, zero, k_all).astype(BF16))
    v_all = jnp.concatenate([kvp_ref[:, B_KV_WIDTH:], kvc_ref[:, B_KV_WIDTH:]], axis=0)
    vt_all = v_all.T.astype(BF16)

    tiles_per_kv = B_WIDTH // LANES // B_KV_HEADS
    band_keys = 2 * WINDOW
    kj = lax.broadcasted_iota(jnp.int32, (band_keys, WINDOW), 0)
    qi = lax.broadcasted_iota(jnp.int32, (band_keys, WINDOW), 1)
    diff = qi + WINDOW - kj
    band = (diff >= 0) & (diff < WINDOW)
    first_key = jnp.where(seq_tile == 0, WINDOW, 0)
    neg_inf = jnp.full((band_keys, WINDOW), -jnp.inf, F32)
    mask_any = jnp.where(band, 0.0, neg_inf)
    mask_first = jnp.where(band & (kj >= first_key), 0.0, neg_inf)

    for n in range(t_rows // WINDOW):
        rows = slice(n * WINDOW, (n + 1) * WINDOW)
        band_rows = slice(n * WINDOW, (n + 2) * WINDOW)
        mask = jnp.concatenate([mask_first if n == 0 else mask_any] * tiles_per_kv, axis=1)
        for kv in range(B_KV_HEADS):
            tail.emit()
            q4 = jnp.concatenate(
                [q_tiles[kv * tiles_per_kv + j][rows] for j in range(tiles_per_kv)], axis=0)
            k_cat = jnp.concatenate([k_lo[kv][band_rows], k_hi[kv][band_rows]], axis=0)
            st = lax.dot_general(k_cat, q4, (((1,), (1,)), ((), ())), preferred_element_type=F32)
            vt = vt_all[kv * B_HEAD_DIM:(kv + 1) * B_HEAD_DIM, band_rows]
            o_halves = []
            for half in range(2):
                sh = st[half * band_keys:(half + 1) * band_keys] + mask
                sink = jnp.concatenate(
                    [jnp.full((1, WINDOW), sinks_ref[layer, 2 * (kv * tiles_per_kv + j) + half], F32)
                     for j in range(tiles_per_kv)], axis=1)
                m = jnp.maximum(jnp.max(sh, axis=0, keepdims=True), sink)
                p = jnp.exp(sh - m)
                l = jnp.sum(p, axis=0, keepdims=True) + jnp.exp(sink - m)
                o_half = jnp.dot(vt, p.astype(BF16), preferred_element_type=F32)
                o_halves.append(o_half * (1.0 / l))
            ot = jnp.concatenate(o_halves, axis=0)
            for j in range(tiles_per_kv):
                jj = kv * tiles_per_kv + j
                gate_ref = bg_refs[jj // 2]
                gate_cols = slice((jj % 2) * LANES, (jj % 2 + 1) * LANES)
                out_cols = slice(A_WIDTH + jj * LANES, A_WIDTH + (jj + 1) * LANES)
                o_tile = ot[:, j * WINDOW:(j + 1) * WINDOW].T
                mix_ref[rows, out_cols] = (o_tile * _silu(gate_ref[rows, gate_cols])).astype(BF16)

    tail.finish()


def _even_layer(h, x, sinks, rope_tab, a_ln_g, a_ln_b, ws, bst, w_out_bf, ln_g, ln_b, layer, seq):
    m = h.shape[0]
    t = MIX_T
    n_tiles = m // t
    n_s = seq // t
    blocks_per_tile = t // WINDOW
    cur, res, out = _lagged_tiles(n_tiles)

    def prev(i):
        return jnp.maximum(cur(i) * blocks_per_tile - 1, 0)

    tail_in, out_specs, out_shape, tail_scratch, tail_args = _tail_specs(x, w_out_bf, ln_g, ln_b, res, out)
    return pl.pallas_call(
        partial(_even_layer_kernel, layer=layer, n_tiles=n_tiles, n_s=n_s),
        grid=(n_tiles + 2,),
        in_specs=[
            pl.BlockSpec(memory_space=pltpu.SMEM),
            pl.BlockSpec((t, h.shape[1]), lambda i: (cur(i), 0)),
            pl.BlockSpec((WINDOW, KV_BLOCK), lambda i: (prev(i), KV_COL)),
            pl.BlockSpec((t, 2 * LANES), lambda i: (cur(i), 0)),
            pl.BlockSpec((WINDOW, 2 * LANES), lambda i: (prev(i), 0)),
            _whole(a_ln_g),
            _whole(a_ln_b),
            _layer_block((A_GROUPS, CHUNK, CHUNK), layer),
            _layer_block((CHUNK, A_GROUPS), layer),
        ] + tail_in,
        out_specs=out_specs,
        out_shape=out_shape,
        scratch_shapes=tail_scratch,
        compiler_params=_compiler_params("arbitrary"),
        name="even_layer",
    )(sinks, h, h, rope_tab, rope_tab, a_ln_g, a_ln_b, ws, bst, *tail_args)


def _odd_layer_kernel(*refs, layer, n_tiles, n_s):
    *io_refs, mix_a, mix_b, z_a, z_b, wout_vmem, wout_sem = refs[:21]
    (ext_c, ext_d, lvl2, lvl4, hstate, sa, sb, sh, wax_bf, wpool_bf) = scratch = refs[21:]
    wout_hbm = io_refs[ODD_WOUT_ARG]
    io_refs[ODD_WOUT_ARG] = wout_vmem
    wa_ref, wx_ref, wpool_ref = io_refs[3], io_refs[4], io_refs[8]
    t_rows = io_refs[0].shape[0]
    step = pl.program_id(0)
    s_idx = jnp.minimum(step, n_tiles - 1) % n_s

    @pl.when(s_idx == 0)
    def _():
        ext_c[:, 0:CONV_HALO, :] = jnp.zeros((C_HEADS, CONV_HALO, LANES), F32)
        ext_d[:, 0:POOL_HALO, :] = jnp.zeros((D_WIDTH // LANES, POOL_HALO, LANES), F32)
        hstate[...] = jnp.zeros_like(hstate)
        wax_bf[:, :, :C_HEAD_DIM] = wa_ref[...].astype(BF16)
        wax_bf[:, :, C_HEAD_DIM:] = wx_ref[...].astype(BF16)
        wpool_bf[...] = wpool_ref[...].astype(BF16)

    @pl.when(s_idx > 0)
    def _():
        ext_c[:, 0:CONV_HALO, :] = ext_c[:, t_rows:t_rows + CONV_HALO, :]
        ext_d[:, 0:POOL_HALO, :] = ext_d[:, t_rows:t_rows + POOL_HALO, :]

    _run_stages(step, n_tiles, partial(_odd_step, *io_refs, *scratch, layer=layer, s_idx=s_idx),
                wout_hbm, wout_vmem, wout_sem, mix_a, mix_b, z_a, z_b)


def _odd_step(h_ref, convw_ref, convb_ref, wa_ref, wx_ref, ba_ref, bx_ref,
              lam_ref, wpool_ref, dscale_ref, x_ref, wout_ref, g_ref, b_ref, o_ref,
              ext_c, ext_d, lvl2, lvl4, hstate, sa, sb, sh, wax_bf, wpool_bf,
              mix_ref, mix_prev, z_ref, z_prev, *, layer, s_idx, mix, proj, ln):
    xc_ref, cg_ref, xd_ref, dg_ref = (h_ref.at[:, c * C_WIDTH:(c + 1) * C_WIDTH] for c in range(4))
    t_rows = h_ref.shape[0]
    row = slice(layer, layer + 1)
    n_groups = t_rows // SUBLANES
    tail = _Tail(mix_prev, wout_ref, x_ref, z_ref, z_prev, g_ref, b_ref, layer, o_ref, proj, ln)
    if not mix:
        tail.finish()
        return
    tail.emit()

    z = -lam_ref[row, :]
    softplus = jnp.maximum(z, 0.0) + jnp.log1p(jnp.exp(-jnp.abs(z)))
    for hd in range(C_HEADS):
        cols = slice(hd * C_HEAD_DIM, (hd + 1) * C_HEAD_DIM)
        ext_c[hd, CONV_HALO:, :] = xc_ref[:, cols]
        xconv = convb_ref[row, cols]
        for j in range(CONV_WIDTH):
            lo = CONV_HALO - (CONV_WIDTH - 1) + j
            xconv = xconv + ext_c[hd, lo:lo + t_rows, :] * convw_ref[j:j + 1, cols]
        ri = jnp.dot(xconv.astype(BF16), wax_bf[hd], preferred_element_type=F32)
        r = _sigmoid(ri[:, :C_HEAD_DIM] + ba_ref[row, cols])
        i = _sigmoid(ri[:, C_HEAD_DIM:] + bx_ref[row, cols])
        log_a = -LRU_C * r * softplus[:, cols]
        a = jnp.exp(log_a)
        m2 = -jnp.tanh(log_a) * (a * a + 1.0)
        mult = jnp.where(m2 == 0.0, 0.0, m2 * lax.rsqrt(m2))
        b = mult * (i * xconv)
        for g in range(n_groups):
            base = (g * C_HEADS + hd) * SCAN_PITCH
            sa[base:base + SUBLANES, :] = a[g * SUBLANES:(g + 1) * SUBLANES]
            sb[base:base + SUBLANES, :] = b[g * SUBLANES:(g + 1) * SUBLANES]
        if hd in (2, 5):
            tail.ln_half()
        if hd in (1, 3, 4, 6, 7):
            tail.emit()

    h = hstate[...]
    for t in range(t_rows):
        g, s = divmod(t, SUBLANES)
        at_t = pl.ds(g * C_HEADS * SCAN_PITCH + s, C_HEADS, stride=SCAN_PITCH)
        h = sa[at_t, :] * h + sb[at_t, :]
        sh[at_t, :] = h
    hstate[...] = h
    tail.emit()
    for hd in range(C_HEADS):
        cols = slice(hd * C_HEAD_DIM, (hd + 1) * C_HEAD_DIM)
        hcol = jnp.concatenate(
            [sh[(g * C_HEADS + hd) * SCAN_PITCH:(g * C_HEADS + hd) * SCAN_PITCH + SUBLANES, :]
             for g in range(n_groups)], axis=0)
        mix_ref[:, cols] = (hcol * _silu(cg_ref[:, cols])).astype(BF16)

    n_ext = POOL_HALO + t_rows
    tiles_per_group = D_GROUP_DIM // LANES
    pos1 = s_idx * t_rows + lax.broadcasted_iota(jnp.int32, (t_rows, LANES), 0) + 1
    for g, w in enumerate(POOL_WINDOWS):
        if g == 2:
            tail.emit()
        inv = 1.0 / jnp.minimum(pos1, w).astype(F32)
        pooled = []
        for k in range(tiles_per_group):
            tile = g * tiles_per_group + k
            cols = slice(tile * LANES, (tile + 1) * LANES)
            ext_d[tile, POOL_HALO:, :] = xd_ref[:, cols]
            lo = SUBLANES
            win = ext_d[tile, lo:n_ext, :] + ext_d[tile, lo - 1:n_ext - 1, :]
            if w >= 4:
                lvl2[k, lo:n_ext, :] = win
                lo += SUBLANES
                win = lvl2[k, lo:n_ext, :] + lvl2[k, lo - 2:n_ext - 2, :]
            if w >= 8:
                lvl4[k, lo:n_ext, :] = win
                lo += SUBLANES
                win = lvl4[k, lo:n_ext, :] + lvl4[k, lo - 4:n_ext - 4, :]
            if w >= 16:
                lo += SUBLANES
                win = win[SUBLANES:] + win[:-SUBLANES]
            pooled.append((win[POOL_HALO - lo:] * inv - xd_ref[:, cols]).astype(BF16))
        cols = slice(g * D_GROUP_DIM, (g + 1) * D_GROUP_DIM)
        mixed = jnp.dot(jnp.concatenate(pooled, axis=1), wpool_bf[g], preferred_element_type=F32)
        out_cols = slice(C_WIDTH + g * D_GROUP_DIM, C_WIDTH + (g + 1) * D_GROUP_DIM)
        mix_ref[:, out_cols] = (mixed * dscale_ref[row, cols] * _silu(dg_ref[:, cols])).astype(BF16)

    tail.finish()


def _odd_layer(h, x, conv_w, conv_b, w_a, w_x, b_a, b_x, lam, w_pool, d_scale, w_out_bf, ln_g, ln_b,
               layer, seq):
    m = h.shape[0]
    t = MIX_T
    n_tiles = m // t
    n_s = seq // t
    cur, res, out = _lagged_tiles(n_tiles)
    n_pool = len(POOL_WINDOWS)
    tail_in, out_specs, out_shape, tail_scratch, tail_args = _tail_specs(x, w_out_bf, ln_g, ln_b, res, out)
    return pl.pallas_call(
        partial(_odd_layer_kernel, layer=layer, n_tiles=n_tiles, n_s=n_s),
        grid=(n_tiles + 2,),
        in_specs=[
            pl.BlockSpec((t, h.shape[1]), lambda i: (cur(i), 0)),
            _layer_block((CONV_WIDTH, C_WIDTH), layer),
            _whole(conv_b),
            _layer_block((C_HEADS, C_HEAD_DIM, C_HEAD_DIM), layer),
            _layer_block((C_HEADS, C_HEAD_DIM, C_HEAD_DIM), layer),
            _whole(b_a), _whole(b_x), _whole(lam),
            _layer_block((n_pool, D_GROUP_DIM, D_GROUP_DIM), layer),
            _whole(d_scale),
        ] + tail_in,
        out_specs=out_specs,
        out_shape=out_shape,
        scratch_shapes=tail_scratch + [
            pltpu.VMEM((C_WIDTH // LANES, CONV_HALO + t, LANES), F32),
            pltpu.VMEM((D_WIDTH // LANES, POOL_HALO + t, LANES), F32),
            pltpu.VMEM((D_GROUP_DIM // LANES, POOL_HALO + t, LANES), F32),
            pltpu.VMEM((D_GROUP_DIM // LANES, POOL_HALO + t, LANES), F32),
            pltpu.VMEM((SUBLANES, LANES), F32),
            pltpu.VMEM((t // SUBLANES * C_HEADS * SCAN_PITCH, LANES), F32),
            pltpu.VMEM((t // SUBLANES * C_HEADS * SCAN_PITCH, LANES), F32),
            pltpu.VMEM((t // SUBLANES * C_HEADS * SCAN_PITCH, LANES), F32),
            pltpu.VMEM((C_HEADS, C_HEAD_DIM, 2 * C_HEAD_DIM), BF16),
            pltpu.VMEM((n_pool, D_GROUP_DIM, D_GROUP_DIM), BF16),
        ],
        compiler_params=_compiler_params("arbitrary"),
        name="odd_layer",
    )(h, conv_w, conv_b, w_a, w_x, b_a, b_x, lam, w_pool, d_scale, *tail_args)


def _rope_coefficients():
    half = ROT_DIM // 2
    inv_freq = ROPE_THETA ** (-jnp.arange(0, ROT_DIM, 2, dtype=F32) / ROT_DIM)
    rest = jnp.zeros((B_HEAD_DIM - ROT_DIM,), F32)
    freq_head = jnp.concatenate([inv_freq, inv_freq, rest])
    sign_head = jnp.concatenate([-jnp.ones((half,), F32), jnp.ones((half,), F32), rest])
    reps = LANES // B_HEAD_DIM
    return jnp.stack([jnp.tile(freq_head, reps), jnp.tile(sign_head, reps)])


def kernel(x, positions, even_w_in, even_a_ln_g, even_a_ln_b, even_a_ws, even_a_bs, even_b_sinks, even_w_out, even_ln_g, even_ln_b, odd_w_in, odd_conv_w, odd_conv_b, odd_w_a, odd_b_a, odd_w_x, odd_b_x, odd_lam, odd_w_pool, odd_d_scale, odd_w_out, odd_ln_g, odd_ln_b):
    batch, seq, d = x.shape
    m = batch * seq
    xf = x.reshape(m, d)
    rope_tab = None
    even_bst = jnp.swapaxes(even_a_bs, 1, 2)
    w_in, w_layer = even_w_in, 0
    for layer in range(DEPTH):
        j = layer // 2
        if layer % 2 == 0:
            h, w_out, w_in, tab = _proj_in(xf, w_in, w_layer, even_w_out, j, odd_w_in, j,
                                           positions if rope_tab is None else None)
            rope_tab = tab if rope_tab is None else rope_tab
            xf = _even_layer(h, xf, even_b_sinks, rope_tab, even_a_ln_g, even_a_ln_b, even_a_ws,
                             even_bst, w_out, even_ln_g, even_ln_b, j, seq)
        else:
            w_next = even_w_in if layer + 1 < DEPTH else None
            h, w_out, w_in, _ = _proj_in(xf, w_in, w_layer, odd_w_out, j, w_next, j + 1)
            xf = _odd_layer(h, xf, odd_conv_w, odd_conv_b, odd_w_a, odd_w_x, odd_b_a, odd_b_x, odd_lam,
                            odd_w_pool, odd_d_scale, w_out, odd_ln_g, odd_ln_b, j, seq)
        w_layer = None
    return xf.reshape(batch, seq, d)
```

```python
from functools import partial

import jax
import jax.numpy as jnp
from jax import lax
from jax.experimental import pallas as pl
from jax.experimental.pallas import tpu as pltpu

D_MODEL = 2048
DEPTH = 4
A_WIDTH = 1024
A_GROUPS = 8
CHUNK = 128
B_HEAD_DIM = 64
B_Q_HEADS = 16
B_KV_HEADS = 2
B_WIDTH = 1024
B_KV_WIDTH = 128
WINDOW = 128
ROT_DIM = 16
ROPE_THETA = 500000.0
C_WIDTH = 1024
C_HEADS = 8
C_HEAD_DIM = 128
CONV_WIDTH = 4
LRU_C = 8.0
D_WIDTH = 1024
POOL_WINDOWS = (2, 4, 8, 16)
D_GROUP_DIM = 256
MIX_WIDTH = 2048
DN_ALPHA = (2 * DEPTH) ** 0.25
LN_EPS = 1e-5

LANES = 128
SUBLANES = 8
VMEM_LIMIT_BYTES = 56 * 1024 * 1024

PROJ_TM = 256
MIX_T = 256
OUT_BLOCK = 256
OUT_BLOCKS = D_MODEL // OUT_BLOCK
W_CHUNK = 256
CONV_HALO = SUBLANES
POOL_HALO = 4 * SUBLANES
SCAN_PITCH = SUBLANES + 1

KV_BLOCK = 2 * B_KV_WIDTH
KV_COL = (3 * A_WIDTH + B_WIDTH) // KV_BLOCK
GATE_B_COL = KV_COL + 1

F32 = jnp.float32
BF16 = jnp.bfloat16


def _sigmoid(x):
    return 0.5 * jnp.tanh(0.5 * x) + 0.5


def _silu(x):
    hx = 0.5 * x
    return hx * jnp.tanh(hx) + hx


def _resident(shape):
    return pl.BlockSpec(shape, lambda *_: (0,) * len(shape), pipeline_mode=pl.Buffered(1))


def _whole(arr):
    return pl.BlockSpec(arr.shape, lambda *_: (0,) * arr.ndim)


def _layer_block(shape, layer):
    return pl.BlockSpec((None,) + shape, lambda *_: (layer,) + (0,) * len(shape))


def _compiler_params(*semantics):
    return pltpu.CompilerParams(dimension_semantics=semantics, vmem_limit_bytes=VMEM_LIMIT_BYTES)


def _proj_in_kernel(*refs, cast_next, w_layer, rope):
    n_in = 3 + cast_next + 2 * rope
    n_out = 2 + cast_next + rope
    x_ref, w_hbm, wout_ref = refs[:3]
    o_ref, wout_bf_ref = refs[n_in:n_in + 2]
    staged = w_hbm.dtype != BF16
    w_vmem, *stage, sems = refs[n_in + n_out:]

    def side_jobs():
        if cast_next:
            refs[n_in + 2][...] = refs[3][...].astype(BF16)
        for p in range(OUT_BLOCKS):
            wout_bf_ref[p] = wout_ref[:, p * OUT_BLOCK:(p + 1) * OUT_BLOCK].astype(BF16)
        if rope:
            pos_ref, coef_ref = refs[n_in - 2:n_in]
            tab_ref = refs[n_in + n_out - 1]
            ang = pos_ref[...].astype(F32) * coef_ref[0:1, :]
            tab_ref[:, :LANES] = jnp.cos(ang)
            tab_ref[:, LANES:] = jnp.sin(ang) * coef_ref[1:2, :]

    n_chunks = w_vmem.shape[0] // W_CHUNK
    w_src = w_hbm if w_layer is None else w_hbm.at[w_layer]

    def chunk_copy(c):
        rows = pl.ds(c * W_CHUNK, W_CHUNK)
        if staged:
            return pltpu.make_async_copy(w_src.at[rows, :], stage[0].at[c % 2], sems.at[c % 2])
        return pltpu.make_async_copy(w_src.at[rows, :], w_vmem.at[rows, :], sems.at[c])

    step = pl.program_id(0)

    @pl.when(step == 0)
    def _():
        xb = x_ref[...].astype(BF16)
        for c in range(2 if staged else n_chunks):
            chunk_copy(c).start()
        side_jobs()
        for c in range(n_chunks):
            rows = slice(c * W_CHUNK, (c + 1) * W_CHUNK)
            chunk_copy(c).wait()
            if staged:
                w_vmem[rows, :] = stage[0][c % 2].astype(BF16)
                if c + 2 < n_chunks:
                    chunk_copy(c + 2).start()
            part = jnp.dot(xb[:, rows], w_vmem[rows, :], preferred_element_type=F32)
            if c == 0:
                o_ref[...] = part
            else:
                o_ref[...] += part

    @pl.when(step > 0)
    def _():
        side_jobs()
        o_ref[...] = jnp.dot(x_ref[...].astype(BF16), w_vmem[...], preferred_element_type=F32)


def _proj_in(x, w, w_layer, w_out, layer, w_next, layer_next, positions=None):
    m, k = x.shape
    n = w.shape[-1]
    steps = m // PROJ_TM

    def slab_specs(stacked, which):
        rows, cols = stacked.shape[1:]
        slab = rows // steps
        return (pl.BlockSpec((None, slab, cols), lambda i: (which, i, 0)),
                pl.BlockSpec((slab, cols), lambda i: (i, 0)),
                jax.ShapeDtypeStruct((rows, cols), BF16))

    k_out = w_out.shape[1]
    slab_out = k_out // steps
    casts = [(pl.BlockSpec((None, slab_out, D_MODEL), lambda i: (layer, i, 0)),
              pl.BlockSpec((OUT_BLOCKS, slab_out, OUT_BLOCK), lambda i: (0, i, 0)),
              jax.ShapeDtypeStruct((OUT_BLOCKS, k_out, OUT_BLOCK), BF16))]
    args = [x, w, w_out]
    if w_next is not None:
        casts.append(slab_specs(w_next, layer_next))
        args.append(w_next)
    rope = positions is not None
    if rope:
        casts.append((pl.BlockSpec((PROJ_TM, 1), lambda i: (i, 0)), None, None))
        coef = _rope_coefficients()
        casts.append((_whole(coef), pl.BlockSpec((PROJ_TM, 2 * LANES), lambda i: (i, 0)),
                      jax.ShapeDtypeStruct((m, 2 * LANES), F32)))
        args += [positions.reshape(m, 1), coef]
    staged = w.dtype != BF16
    scratch = [pltpu.VMEM((k, n), BF16)]
    if staged:
        scratch.append(pltpu.VMEM((2, W_CHUNK, n), w.dtype))
    scratch.append(pltpu.SemaphoreType.DMA((2 if staged else k // W_CHUNK,)))
    outs = pl.pallas_call(
        partial(_proj_in_kernel, cast_next=w_next is not None, w_layer=w_layer, rope=rope),
        grid=(steps,),
        in_specs=[pl.BlockSpec((PROJ_TM, k), lambda i: (i, 0)), pl.BlockSpec(memory_space=pl.ANY)]
        + [c[0] for c in casts],
        out_specs=[pl.BlockSpec((PROJ_TM, n), lambda i: (i, 0))] + [c[1] for c in casts if c[1]],
        out_shape=[jax.ShapeDtypeStruct((m, n), F32)] + [c[2] for c in casts if c[2]],
        scratch_shapes=scratch,
        compiler_params=_compiler_params("arbitrary"),
        name="proj_in",
    )(*args)
    h, w_out_bf, *rest = outs
    return h, w_out_bf, rest.pop(0) if w_next is not None else None, rest.pop(0) if rope else None


class _Tail:
    def __init__(self, mix_prev, wout_ref, x_ref, z_ref, z_prev, g_ref, b_ref, layer, o_ref,
                 proj, ln):
        self.mix = mix_prev[...] if proj else None
        self.proj_refs = (wout_ref, x_ref, z_ref)
        self.ln_refs = (z_prev, g_ref, b_ref, layer, o_ref)
        self.blocks_left = list(range(OUT_BLOCKS)) if proj else []
        rows = z_ref.shape[0]
        self.halves_left = [slice(r * rows // 2, (r + 1) * rows // 2) for r in range(2)] if ln else []

    def emit(self, count=1):
        wout_ref, x_ref, z_ref = self.proj_refs
        for _ in range(min(count, len(self.blocks_left))):
            blk = self.blocks_left.pop(0)
            cols = slice(blk * OUT_BLOCK, (blk + 1) * OUT_BLOCK)
            z_ref[:, cols] = DN_ALPHA * x_ref[:, cols] + jnp.dot(
                self.mix, wout_ref[blk], preferred_element_type=F32)

    def ln_half(self):
        if self.halves_left:
            z_prev, g_ref, b_ref, layer, o_ref = self.ln_refs
            rows = self.halves_left.pop(0)
            z = z_prev[rows, :]
            mu = jnp.mean(z, axis=-1, keepdims=True)
            zc = z - mu
            var = jnp.mean(zc * zc, axis=-1, keepdims=True)
            o_ref[rows, :] = (zc * lax.rsqrt(var + LN_EPS) * g_ref[layer:layer + 1, :]
                              + b_ref[layer:layer + 1, :])

    def finish(self):
        self.ln_half()
        self.ln_half()
        self.emit(OUT_BLOCKS)


def _lagged_tiles(n_tiles):
    def clamp(i):
        return jnp.clip(i, 0, n_tiles - 1)

    return clamp, (lambda i: clamp(i - 1)), (lambda i: clamp(i - 2))


def _run_stages(step, n_tiles, body, wout_hbm, wout_vmem, wout_sem, mix_a, mix_b, z_a, z_b):
    wout_copy = pltpu.make_async_copy(wout_hbm, wout_vmem, wout_sem.at[0])

    @pl.when(step == 0)
    def _():
        wout_copy.start()

    @pl.when(step == 1)
    def _():
        wout_copy.wait()

    def stage(pred, parity, **stages):
        bufs = (mix_a, mix_b, z_a, z_b) if parity == 0 else (mix_b, mix_a, z_b, z_a)

        @pl.when(pred)
        def _():
            body(*bufs, **stages)

    steady = (step >= 2) & (step < n_tiles)
    stage(step == 0, 0, mix=True, proj=False, ln=False)
    stage(step == 1, 1, mix=True, proj=True, ln=False)
    stage(steady & (step % 2 == 0), 0, mix=True, proj=True, ln=True)
    stage(steady & (step % 2 == 1), 1, mix=True, proj=True, ln=True)
    stage(step == n_tiles, n_tiles % 2, mix=False, proj=True, ln=True)
    stage(step == n_tiles + 1, (n_tiles + 1) % 2, mix=False, proj=False, ln=True)


def _tail_specs(x, w_out_bf, ln_g, ln_b, res, out):
    m, d = x.shape
    in_specs = [
        pl.BlockSpec((MIX_T, d), lambda i: (res(i), 0)),
        pl.BlockSpec(memory_space=pl.ANY),
        _whole(ln_g),
        _whole(ln_b),
    ]
    out_spec = pl.BlockSpec((MIX_T, d), lambda i: (out(i), 0))
    out_shape = jax.ShapeDtypeStruct((m, d), F32)
    scratch = [pltpu.VMEM((MIX_T, MIX_WIDTH), BF16), pltpu.VMEM((MIX_T, MIX_WIDTH), BF16),
               pltpu.VMEM((MIX_T, d), F32), pltpu.VMEM((MIX_T, d), F32),
               pltpu.VMEM(w_out_bf.shape, BF16), pltpu.SemaphoreType.DMA((1,))]
    return in_specs, out_spec, out_shape, scratch, [x, w_out_bf, ln_g, ln_b]


def _rope_tile(t, cos, sin, take_upper):
    upper = pltpu.roll(t, LANES - ROT_DIM // 2, 1)
    lower = pltpu.roll(t, ROT_DIM // 2, 1)
    return t * cos + jnp.where(take_upper, upper, lower) * sin


EVEN_WOUT_ARG = 10
ODD_WOUT_ARG = 11


def _even_layer_kernel(*refs, layer, n_tiles, n_s):
    *io_refs, mix_a, mix_b, z_a, z_b, wout_vmem, wout_sem = refs
    wout_hbm = io_refs[EVEN_WOUT_ARG]
    io_refs[EVEN_WOUT_ARG] = wout_vmem
    step = pl.program_id(0)
    seq_tile = jnp.minimum(step, n_tiles - 1) % n_s
    _run_stages(step, n_tiles, partial(_even_step, *io_refs, layer=layer, seq_tile=seq_tile),
                wout_hbm, wout_vmem, wout_sem, mix_a, mix_b, z_a, z_b)


def _even_step(sinks_ref, h_ref, kvp_ref, tabc_ref, tabp_ref, lng_ref, lnb_ref, ws_ref, bst_ref,
               x_ref, wout_ref, g_ref, b_ref, o_ref, mix_ref, mix_prev, z_ref, z_prev,
               *, layer, seq_tile, mix, proj, ln):
    u_ref, v_ref, ag_ref, q_ref = (h_ref.at[:, c * A_WIDTH:(c + 1) * A_WIDTH] for c in range(4))
    kvc_ref = h_ref.at[:, KV_COL * KV_BLOCK:(KV_COL + 1) * KV_BLOCK]
    bg_refs = [h_ref.at[:, (GATE_B_COL + c) * KV_BLOCK:(GATE_B_COL + c + 1) * KV_BLOCK]
               for c in range(B_WIDTH // KV_BLOCK)]
    cosc_ref, sinc_ref = tabc_ref.at[:, :LANES], tabc_ref.at[:, LANES:]
    cosp_ref, sinp_ref = tabp_ref.at[:, :LANES], tabp_ref.at[:, LANES:]
    t_rows = h_ref.shape[0]
    n_chunks = t_rows // CHUNK
    tail = _Tail(mix_prev, wout_ref, x_ref, z_ref, z_prev, g_ref, b_ref, layer, o_ref, proj, ln)
    if not mix:
        tail.finish()
        return
    tail.emit()

    v = v_ref[...]
    mu = jnp.mean(v, axis=-1, keepdims=True)
    vc = v - mu
    var = jnp.mean(vc * vc, axis=-1, keepdims=True)
    vln = (vc * lax.rsqrt(var + LN_EPS) * lng_ref[layer:layer + 1, :]
           + lnb_ref[layer:layer + 1, :]).astype(BF16)
    causal = (lax.broadcasted_iota(jnp.int32, (CHUNK, CHUNK), 0)
              >= lax.broadcasted_iota(jnp.int32, (CHUNK, CHUNK), 1))
    lane = lax.broadcasted_iota(jnp.int32, (1, LANES), 1)
    take_upper = (lane % B_HEAD_DIM) < (ROT_DIM // 2)
    low_half = lane < B_HEAD_DIM
    cosc = cosc_ref[...]
    sinc = sinc_ref[...]
    scale = B_HEAD_DIM ** -0.5
    q_tiles = []
    for g in range(A_GROUPS):
        cols = slice(g * CHUNK, (g + 1) * CHUNK)
        w = jnp.where(causal, ws_ref[g], 0.0).astype(BF16)
        rhs = jnp.concatenate(
            [vln[c * CHUNK:(c + 1) * CHUNK, cols] for c in range(n_chunks)], axis=1)
        mixed = jnp.dot(w, rhs, preferred_element_type=F32) + bst_ref[:, g:g + 1]
        for c in range(n_chunks):
            rows = slice(c * CHUNK, (c + 1) * CHUNK)
            gate = _silu(ag_ref[rows, cols])
            mix_ref[rows, cols] = (u_ref[rows, cols] * mixed[:, c * CHUNK:(c + 1) * CHUNK]
                                   * gate).astype(BF16)
        q_tiles.append(
            (_rope_tile(q_ref[:, cols], cosc, sinc, take_upper) * scale).astype(BF16))
        if g in (1, 5):
            tail.ln_half()
        if g in (2, 4, 6):
            tail.emit()

    k_all = jnp.concatenate(
        [_rope_tile(kvp_ref[:, :B_KV_WIDTH], cosp_ref[...], sinp_ref[...], take_upper),
         _rope_tile(kvc_ref[:, :B_KV_WIDTH], cosc, sinc, take_upper)], axis=0)
    k_swap = pltpu.roll(k_all, B_HEAD_DIM, 1)
    zero = jnp.zeros_like(k_all)
    k_lo = (jnp.where(low_half, k_all, zero).astype(BF16), jnp.where(low_half, k_swap, zero).astype(BF16))
    k_hi = (jnp.where(low_half, zero, k_swap).astype(BF16), jnp.where(low_half, zero, k_all).astype(BF16))
    v_all = jnp.concatenate([kvp_ref[:, B_KV_WIDTH:], kvc_ref[:, B_KV_WIDTH:]], axis=0)
    vt_all = v_all.T.astype(BF16)

    tiles_per_kv = B_WIDTH // LANES // B_KV_HEADS
    band_keys = 2 * WINDOW
    kj = lax.broadcasted_iota(jnp.int32, (band_keys, WINDOW), 0)
    qi = lax.broadcasted_iota(jnp.int32, (band_keys, WINDOW), 1)
    diff = qi + WINDOW - kj
    band = (diff >= 0) & (diff < WINDOW)
    first_key = jnp.where(seq_tile == 0, WINDOW, 0)
    neg_inf = jnp.full((band_keys, WINDOW), -jnp.inf, F32)
    mask_any = jnp.where(band, 0.0, neg_inf)
    mask_first = jnp.where(band & (kj >= first_key), 0.0, neg_inf)

    for n in range(t_rows // WINDOW):
        rows = slice(n * WINDOW, (n + 1) * WINDOW)
        band_rows = slice(n * WINDOW, (n + 2) * WINDOW)
        mask = jnp.concatenate([mask_first if n == 0 else mask_any] * tiles_per_kv, axis=1)
        for kv in range(B_KV_HEADS):
            tail.emit()
            q4 = jnp.concatenate(
                [q_tiles[kv * tiles_per_kv + j][rows] for j in range(tiles_per_kv)], axis=0)
            k_cat = jnp.concatenate([k_lo[kv][band_rows], k_hi[kv][band_rows]], axis=0)
            st = lax.dot_general(k_cat, q4, (((1,), (1,)), ((), ())), preferred_element_type=F32)
            vt = vt_all[kv * B_HEAD_DIM:(kv + 1) * B_HEAD_DIM, band_rows]
            o_halves = []
            for half in range(2):
                sh = st[half * band_keys:(half + 1) * band_keys] + mask
                sink = jnp.concatenate(
                    [jnp.full((1, WINDOW), sinks_ref[layer, 2 * (kv * tiles_per_kv + j) + half], F32)
                     for j in range(tiles_per_kv)], axis=1)
                m = jnp.maximum(jnp.max(sh, axis=0, keepdims=True), sink)
                p = jnp.exp(sh - m)
                l = jnp.sum(p, axis=0, keepdims=True) + jnp.exp(sink - m)
                o_half = jnp.dot(vt, p.astype(BF16), preferred_element_type=F32)
                o_halves.append(o_half * (1.0 / l))
            ot = jnp.concatenate(o_halves, axis=0)
            for j in range(tiles_per_kv):
                jj = kv * tiles_per_kv + j
                gate_ref = bg_refs[jj // 2]
                gate_cols = slice((jj % 2) * LANES, (jj % 2 + 1) * LANES)
                out_cols = slice(A_WIDTH + jj * LANES, A_WIDTH + (jj + 1) * LANES)
                o_tile = ot[:, j * WINDOW:(j + 1) * WINDOW].T
                mix_ref[rows, out_cols] = (o_tile * _silu(gate_ref[rows, gate_cols])).astype(BF16)

    tail.finish()


def _even_layer(h, x, sinks, rope_tab, a_ln_g, a_ln_b, ws, bst, w_out_bf, ln_g, ln_b, layer, seq):
    m = h.shape[0]
    t = MIX_T
    n_tiles = m // t
    n_s = seq // t
    blocks_per_tile = t // WINDOW
    cur, res, out = _lagged_tiles(n_tiles)

    def prev(i):
        return jnp.maximum(cur(i) * blocks_per_tile - 1, 0)

    tail_in, out_specs, out_shape, tail_scratch, tail_args = _tail_specs(x, w_out_bf, ln_g, ln_b, res, out)
    return pl.pallas_call(
        partial(_even_layer_kernel, layer=layer, n_tiles=n_tiles, n_s=n_s),
        grid=(n_tiles + 2,),
        in_specs=[
            pl.BlockSpec(memory_space=pltpu.SMEM),
            pl.BlockSpec((t, h.shape[1]), lambda i: (cur(i), 0)),
            pl.BlockSpec((WINDOW, KV_BLOCK), lambda i: (prev(i), KV_COL)),
            pl.BlockSpec((t, 2 * LANES), lambda i: (cur(i), 0)),
            pl.BlockSpec((WINDOW, 2 * LANES), lambda i: (prev(i), 0)),
            _whole(a_ln_g),
            _whole(a_ln_b),
            _layer_block((A_GROUPS, CHUNK, CHUNK), layer),
            _layer_block((CHUNK, A_GROUPS), layer),
        ] + tail_in,
        out_specs=out_specs,
        out_shape=out_shape,
        scratch_shapes=tail_scratch,
        compiler_params=_compiler_params("arbitrary"),
        name="even_layer",
    )(sinks, h, h, rope_tab, rope_tab, a_ln_g, a_ln_b, ws, bst, *tail_args)


def _odd_layer_kernel(*refs, layer, n_tiles, n_s):
    *io_refs, mix_a, mix_b, z_a, z_b, wout_vmem, wout_sem = refs[:21]
    (ext_c, ext_d, lvl2, lvl4, hstate, sa, sb, sh, wax_bf, wpool_bf) = scratch = refs[21:]
    wout_hbm = io_refs[ODD_WOUT_ARG]
    io_refs[ODD_WOUT_ARG] = wout_vmem
    wa_ref, wx_ref, wpool_ref = io_refs[3], io_refs[4], io_refs[8]
    t_rows = io_refs[0].shape[0]
    step = pl.program_id(0)
    s_idx = jnp.minimum(step, n_tiles - 1) % n_s

    @pl.when(s_idx == 0)
    def _():
        ext_c[:, 0:CONV_HALO, :] = jnp.zeros((C_HEADS, CONV_HALO, LANES), F32)
        ext_d[:, 0:POOL_HALO, :] = jnp.zeros((D_WIDTH // LANES, POOL_HALO, LANES), F32)
        hstate[...] = jnp.zeros_like(hstate)
        wax_bf[:, :, :C_HEAD_DIM] = wa_ref[...].astype(BF16)
        wax_bf[:, :, C_HEAD_DIM:] = wx_ref[...].astype(BF16)
        wpool_bf[...] = wpool_ref[...].astype(BF16)

    @pl.when(s_idx > 0)
    def _():
        ext_c[:, 0:CONV_HALO, :] = ext_c[:, t_rows:t_rows + CONV_HALO, :]
        ext_d[:, 0:POOL_HALO, :] = ext_d[:, t_rows:t_rows + POOL_HALO, :]

    _run_stages(step, n_tiles, partial(_odd_step, *io_refs, *scratch, layer=layer, s_idx=s_idx),
                wout_hbm, wout_vmem, wout_sem, mix_a, mix_b, z_a, z_b)


def _odd_step(h_ref, convw_ref, convb_ref, wa_ref, wx_ref, ba_ref, bx_ref,
              lam_ref, wpool_ref, dscale_ref, x_ref, wout_ref, g_ref, b_ref, o_ref,
              ext_c, ext_d, lvl2, lvl4, hstate, sa, sb, sh, wax_bf, wpool_bf,
              mix_ref, mix_prev, z_ref, z_prev, *, layer, s_idx, mix, proj, ln):
    xc_ref, cg_ref, xd_ref, dg_ref = (h_ref.at[:, c * C_WIDTH:(c + 1) * C_WIDTH] for c in range(4))
    t_rows = h_ref.shape[0]
    row = slice(layer, layer + 1)
    n_groups = t_rows // SUBLANES
    tail = _Tail(mix_prev, wout_ref, x_ref, z_ref, z_prev, g_ref, b_ref, layer, o_ref, proj, ln)
    if not mix:
        tail.finish()
        return
    tail.emit()

    z = -lam_ref[row, :]
    softplus = jnp.maximum(z, 0.0) + jnp.log1p(jnp.exp(-jnp.abs(z)))
    for hd in range(C_HEADS):
        cols = slice(hd * C_HEAD_DIM, (hd + 1) * C_HEAD_DIM)
        ext_c[hd, CONV_HALO:, :] = xc_ref[:, cols]
        xconv = convb_ref[row, cols]
        for j in range(CONV_WIDTH):
            lo = CONV_HALO - (CONV_WIDTH - 1) + j
            xconv = xconv + ext_c[hd, lo:lo + t_rows, :] * convw_ref[j:j + 1, cols]
        ri = jnp.dot(xconv.astype(BF16), wax_bf[hd], preferred_element_type=F32)
        r = _sigmoid(ri[:, :C_HEAD_DIM] + ba_ref[row, cols])
        i = _sigmoid(ri[:, C_HEAD_DIM:] + bx_ref[row, cols])
        log_a = -LRU_C * r * softplus[:, cols]
        a = jnp.exp(log_a)
        m2 = -jnp.tanh(log_a) * (a * a + 1.0)
        mult = jnp.where(m2 == 0.0, 0.0, m2 * lax.rsqrt(m2))
        b = mult * (i * xconv)
        for g in range(n_groups):
            base = (g * C_HEADS + hd) * SCAN_PITCH
            sa[base:base + SUBLANES, :] = a[g * SUBLANES:(g + 1) * SUBLANES]
            sb[base:base + SUBLANES, :] = b[g * SUBLANES:(g + 1) * SUBLANES]
        if hd in (2, 5):
            tail.ln_half()
        if hd in (1, 3, 4, 6, 7):
            tail.emit()

    h = hstate[...]
    for t in range(t_rows):
        g, s = divmod(t, SUBLANES)
        at_t = pl.ds(g * C_HEADS * SCAN_PITCH + s, C_HEADS, stride=SCAN_PITCH)
        h = sa[at_t, :] * h + sb[at_t, :]
        sh[at_t, :] = h
    hstate[...] = h
    tail.emit()
    for hd in range(C_HEADS):
        cols = slice(hd * C_HEAD_DIM, (hd + 1) * C_HEAD_DIM)
        hcol = jnp.concatenate(
            [sh[(g * C_HEADS + hd) * SCAN_PITCH:(g * C_HEADS + hd) * SCAN_PITCH + SUBLANES, :]
             for g in range(n_groups)], axis=0)
        mix_ref[:, cols] = (hcol * _silu(cg_ref[:, cols])).astype(BF16)

    n_ext = POOL_HALO + t_rows
    tiles_per_group = D_GROUP_DIM // LANES
    pos1 = s_idx * t_rows + lax.broadcasted_iota(jnp.int32, (t_rows, LANES), 0) + 1
    for g, w in enumerate(POOL_WINDOWS):
        if g == 2:
            tail.emit()
        inv = 1.0 / jnp.minimum(pos1, w).astype(F32)
        pooled = []
        for k in range(tiles_per_group):
            tile = g * tiles_per_group + k
            cols = slice(tile * LANES, (tile + 1) * LANES)
            ext_d[tile, POOL_HALO:, :] = xd_ref[:, cols]
            lo = SUBLANES
            win = ext_d[tile, lo:n_ext, :] + ext_d[tile, lo - 1:n_ext - 1, :]
            if w >= 4:
                lvl2[k, lo:n_ext, :] = win
                lo += SUBLANES
                win = lvl2[k, lo:n_ext, :] + lvl2[k, lo - 2:n_ext - 2, :]
            if w >= 8:
                lvl4[k, lo:n_ext, :] = win
                lo += SUBLANES
                win = lvl4[k, lo:n_ext, :] + lvl4[k, lo - 4:n_ext - 4, :]
            if w >= 16:
                lo += SUBLANES
                win = win[SUBLANES:] + win[:-SUBLANES]
            pooled.append((win[POOL_HALO - lo:] * inv - xd_ref[:, cols]).astype(BF16))
        cols = slice(g * D_GROUP_DIM, (g + 1) * D_GROUP_DIM)
        mixed = jnp.dot(jnp.concatenate(pooled, axis=1), wpool_bf[g], preferred_element_type=F32)
        out_cols = slice(C_WIDTH + g * D_GROUP_DIM, C_WIDTH + (g + 1) * D_GROUP_DIM)
        mix_ref[:, out_cols] = (mixed * dscale_ref[row, cols] * _silu(dg_ref[:, cols])).astype(BF16)

    tail.finish()


def _odd_layer(h, x, conv_w, conv_b, w_a, w_x, b_a, b_x, lam, w_pool, d_scale, w_out_bf, ln_g, ln_b,
               layer, seq):
    m = h.shape[0]
    t = MIX_T
    n_tiles = m // t
    n_s = seq // t
    cur, res, out = _lagged_tiles(n_tiles)
    n_pool = len(POOL_WINDOWS)
    tail_in, out_specs, out_shape, tail_scratch, tail_args = _tail_specs(x, w_out_bf, ln_g, ln_b, res, out)
    return pl.pallas_call(
        partial(_odd_layer_kernel, layer=layer, n_tiles=n_tiles, n_s=n_s),
        grid=(n_tiles + 2,),
        in_specs=[
            pl.BlockSpec((t, h.shape[1]), lambda i: (cur(i), 0)),
            _layer_block((CONV_WIDTH, C_WIDTH), layer),
            _whole(conv_b),
            _layer_block((C_HEADS, C_HEAD_DIM, C_HEAD_DIM), layer),
            _layer_block((C_HEADS, C_HEAD_DIM, C_HEAD_DIM), layer),
            _whole(b_a), _whole(b_x), _whole(lam),
            _layer_block((n_pool, D_GROUP_DIM, D_GROUP_DIM), layer),
            _whole(d_scale),
        ] + tail_in,
        out_specs=out_specs,
        out_shape=out_shape,
        scratch_shapes=tail_scratch + [
            pltpu.VMEM((C_WIDTH // LANES, CONV_HALO + t, LANES), F32),
            pltpu.VMEM((D_WIDTH // LANES, POOL_HALO + t, LANES), F32),
            pltpu.VMEM((D_GROUP_DIM // LANES, POOL_HALO + t, LANES), F32),
            pltpu.VMEM((D_GROUP_DIM // LANES, POOL_HALO + t, LANES), F32),
            pltpu.VMEM((SUBLANES, LANES), F32),
            pltpu.VMEM((t // SUBLANES * C_HEADS * SCAN_PITCH, LANES), F32),
            pltpu.VMEM((t // SUBLANES * C_HEADS * SCAN_PITCH, LANES), F32),
            pltpu.VMEM((t // SUBLANES * C_HEADS * SCAN_PITCH, LANES), F32),
            pltpu.VMEM((C_HEADS, C_HEAD_DIM, 2 * C_HEAD_DIM), BF16),
            pltpu.VMEM((n_pool, D_GROUP_DIM, D_GROUP_DIM), BF16),
        ],
        compiler_params=_compiler_params("arbitrary"),
        name="odd_layer",
    )(h, conv_w, conv_b, w_a, w_x, b_a, b_x, lam, w_pool, d_scale, *tail_args)


def _rope_coefficients():
    half = ROT_DIM // 2
    inv_freq = ROPE_THETA ** (-jnp.arange(0, ROT_DIM, 2, dtype=F32) / ROT_DIM)
    rest = jnp.zeros((B_HEAD_DIM - ROT_DIM,), F32)
    freq_head = jnp.concatenate([inv_freq, inv_freq, rest])
    sign_head = jnp.concatenate([-jnp.ones((half,), F32), jnp.ones((half,), F32), rest])
    reps = LANES // B_HEAD_DIM
    return jnp.stack([jnp.tile(freq_head, reps), jnp.tile(sign_head, reps)])


def kernel(x, positions, even_w_in, even_a_ln_g, even_a_ln_b, even_a_ws, even_a_bs, even_b_sinks, even_w_out, even_ln_g, even_ln_b, odd_w_in, odd_conv_w, odd_conv_b, odd_w_a, odd_b_a, odd_w_x, odd_b_x, odd_lam, odd_w_pool, odd_d_scale, odd_w_out, odd_ln_g, odd_ln_b):
    batch, seq, d = x.shape
    m = batch * seq
    xf = x.reshape(m, d)
    rope_tab = None
    even_bst = jnp.swapaxes(even_a_bs, 1, 2)
    w_in, w_layer = even_w_in, 0
    for layer in range(DEPTH):
        j = layer // 2
        if layer % 2 == 0:
            h, w_out, w_in, tab = _proj_in(xf, w_in, w_layer, even_w_out, j, odd_w_in, j,
                                           positions if rope_tab is None else None)
            rope_tab = tab if rope_tab is None else rope_tab
            xf = _even_layer(h, xf, even_b_sinks, rope_tab, even_a_ln_g, even_a_ln_b, even_a_ws,
                             even_bst, w_out, even_ln_g, even_ln_b, j, seq)
        else:
            w_next = even_w_in if layer + 1 < DEPTH else None
            h, w_out, w_in, _ = _proj_in(xf, w_in, w_layer, odd_w_out, j, w_next, j + 1)
            xf = _odd_layer(h, xf, odd_conv_w, odd_conv_b, odd_w_a, odd_w_x, odd_b_a, odd_b_x, odd_lam,
                            odd_w_pool, odd_d_scale, w_out, odd_ln_g, odd_ln_b, j, seq)
        w_layer = None
    return xf.reshape(batch, seq, d)
```

```python
from functools import partial

import jax
import jax.numpy as jnp
from jax import lax
from jax.experimental import pallas as pl
from jax.experimental.pallas import tpu as pltpu

D_MODEL = 2048
DEPTH = 4
A_WIDTH = 1024
A_GROUPS = 8
CHUNK = 128
B_HEAD_DIM = 64
B_Q_HEADS = 16
B_KV_HEADS = 2
B_WIDTH = 1024
B_KV_WIDTH = 128
WINDOW = 128
ROT_DIM = 16
ROPE_THETA = 500000.0
C_WIDTH = 1024
C_HEADS = 8
C_HEAD_DIM = 128
CONV_WIDTH = 4
LRU_C = 8.0
D_WIDTH = 1024
POOL_WINDOWS = (2, 4, 8, 16)
D_GROUP_DIM = 256
MIX_WIDTH = 2048
DN_ALPHA = (2 * DEPTH) ** 0.25
LN_EPS = 1e-5

LANES = 128
SUBLANES = 8
VMEM_LIMIT_BYTES = 56 * 1024 * 1024

PROJ_TM = 256
MIX_T = 256
OUT_BLOCK = 256
OUT_BLOCKS = D_MODEL // OUT_BLOCK
W_CHUNK = 256
CONV_HALO = SUBLANES
POOL_HALO = 4 * SUBLANES
SCAN_PITCH = SUBLANES + 1

KV_BLOCK = 2 * B_KV_WIDTH
KV_COL = (3 * A_WIDTH + B_WIDTH) // KV_BLOCK
GATE_B_COL = KV_COL + 1

F32 = jnp.float32
BF16 = jnp.bfloat16


def _sigmoid(x):
    return 0.5 * jnp.tanh(0.5 * x) + 0.5


def _silu(x):
    hx = 0.5 * x
    return hx * jnp.tanh(hx) + hx


def _resident(shape):
    return pl.BlockSpec(shape, lambda *_: (0,) * len(shape), pipeline_mode=pl.Buffered(1))


def _whole(arr):
    return pl.BlockSpec(arr.shape, lambda *_: (0,) * arr.ndim)


def _layer_block(shape, layer):
    return pl.BlockSpec((None,) + shape, lambda *_: (layer,) + (0,) * len(shape))


def _compiler_params(*semantics):
    return pltpu.CompilerParams(dimension_semantics=semantics, vmem_limit_bytes=VMEM_LIMIT_BYTES)


def _proj_in_kernel(*refs, cast_next, w_layer, rope):
    n_in = 3 + cast_next + 2 * rope
    n_out = 2 + cast_next + rope
    x_ref, w_hbm, wout_ref = refs[:3]
    o_ref, wout_bf_ref = refs[n_in:n_in + 2]
    staged = w_hbm.dtype != BF16
    w_vmem, *stage, sems = refs[n_in + n_out:]

    def side_jobs():
        if cast_next:
            refs[n_in + 2][...] = refs[3][...].astype(BF16)
        for p in range(OUT_BLOCKS):
            wout_bf_ref[p] = wout_ref[:, p * OUT_BLOCK:(p + 1) * OUT_BLOCK].astype(BF16)
        if rope:
            pos_ref, coef_ref = refs[n_in - 2:n_in]
            tab_ref = refs[n_in + n_out - 1]
            ang = pos_ref[...].astype(F32) * coef_ref[0:1, :]
            tab_ref[:, :LANES] = jnp.cos(ang)
            tab_ref[:, LANES:] = jnp.sin(ang) * coef_ref[1:2, :]

    n_chunks = w_vmem.shape[0] // W_CHUNK
    w_src = w_hbm if w_layer is None else w_hbm.at[w_layer]

    def chunk_copy(c):
        rows = pl.ds(c * W_CHUNK, W_CHUNK)
        if staged:
            return pltpu.make_async_copy(w_src.at[rows, :], stage[0].at[c % 2], sems.at[c % 2])
        return pltpu.make_async_copy(w_src.at[rows, :], w_vmem.at[rows, :], sems.at[c])

    step = pl.program_id(0)

    @pl.when(step == 0)
    def _():
        xb = x_ref[...].astype(BF16)
        for c in range(2 if staged else n_chunks):
            chunk_copy(c).start()
        side_jobs()
        for c in range(n_chunks):
            rows = slice(c * W_CHUNK, (c + 1) * W_CHUNK)
            chunk_copy(c).wait()
            if staged:
                w_vmem[rows, :] = stage[0][c % 2].astype(BF16)
                if c + 2 < n_chunks:
                    chunk_copy(c + 2).start()
            part = jnp.dot(xb[:, rows], w_vmem[rows, :], preferred_element_type=F32)
            if c == 0:
                o_ref[...] = part
            else:
                o_ref[...] += part

    @pl.when(step > 0)
    def _():
        side_jobs()
        o_ref[...] = jnp.dot(x_ref[...].astype(BF16), w_vmem[...], preferred_element_type=F32)


def _proj_in(x, w, w_layer, w_out, layer, w_next, layer_next, positions=None):
    m, k = x.shape
    n = w.shape[-1]
    steps = m // PROJ_TM

    def slab_specs(stacked, which):
        rows, cols = stacked.shape[1:]
        slab = rows // steps
        return (pl.BlockSpec((None, slab, cols), lambda i: (which, i, 0)),
                pl.BlockSpec((slab, cols), lambda i: (i, 0)),
                jax.ShapeDtypeStruct((rows, cols), BF16))

    k_out = w_out.shape[1]
    slab_out = k_out // steps
    casts = [(pl.BlockSpec((None, slab_out, D_MODEL), lambda i: (layer, i, 0)),
              pl.BlockSpec((OUT_BLOCKS, slab_out, OUT_BLOCK), lambda i: (0, i, 0)),
              jax.ShapeDtypeStruct((OUT_BLOCKS, k_out, OUT_BLOCK), BF16))]
    args = [x, w, w_out]
    if w_next is not None:
        casts.append(slab_specs(w_next, layer_next))
        args.append(w_next)
    rope = positions is not None
    if rope:
        casts.append((pl.BlockSpec((PROJ_TM, 1), lambda i: (i, 0)), None, None))
        coef = _rope_coefficients()
        casts.append((_whole(coef), pl.BlockSpec((PROJ_TM, 2 * LANES), lambda i: (i, 0)),
                      jax.ShapeDtypeStruct((m, 2 * LANES), F32)))
        args += [positions.reshape(m, 1), coef]
    staged = w.dtype != BF16
    scratch = [pltpu.VMEM((k, n), BF16)]
    if staged:
        scratch.append(pltpu.VMEM((2, W_CHUNK, n), w.dtype))
    scratch.append(pltpu.SemaphoreType.DMA((2 if staged else k // W_CHUNK,)))
    outs = pl.pallas_call(
        partial(_proj_in_kernel, cast_next=w_next is not None, w_layer=w_layer, rope=rope),
        grid=(steps,),
        in_specs=[pl.BlockSpec((PROJ_TM, k), lambda i: (i, 0)), pl.BlockSpec(memory_space=pl.ANY)]
        + [c[0] for c in casts],
        out_specs=[pl.BlockSpec((PROJ_TM, n), lambda i: (i, 0))] + [c[1] for c in casts if c[1]],
        out_shape=[jax.ShapeDtypeStruct((m, n), F32)] + [c[2] for c in casts if c[2]],
        scratch_shapes=scratch,
        compiler_params=_compiler_params("arbitrary"),
        name="proj_in",
    )(*args)
    h, w_out_bf, *rest = outs
    return h, w_out_bf, rest.pop(0) if w_next is not None else None, rest.pop(0) if rope else None


class _Tail:
    def __init__(self, mix_prev, wout_ref, x_ref, z_ref, z_prev, g_ref, b_ref, layer, o_ref,
                 proj, ln):
        self.mix_ref = mix_prev.at[:, :MIX_WIDTH]
        self.proj_refs = (wout_ref, x_ref, z_ref)
        self.ln_refs = (z_prev, g_ref, b_ref, layer, o_ref)
        self.blocks_left = list(range(OUT_BLOCKS)) if proj else []
        rows = z_ref.shape[0]
        self.halves_left = [slice(r * rows // 2, (r + 1) * rows // 2) for r in range(2)] if ln else []

    def emit(self, count=1):
        wout_ref, x_ref, z_ref = self.proj_refs
        for _ in range(min(count, len(self.blocks_left))):
            blk = self.blocks_left.pop(0)
            cols = slice(blk * OUT_BLOCK, (blk + 1) * OUT_BLOCK)
            z_ref[:, cols] = DN_ALPHA * x_ref[:, cols] + jnp.dot(
                self.mix_ref[...], wout_ref[blk], preferred_element_type=F32)

    def ln_half(self):
        if self.halves_left:
            z_prev, g_ref, b_ref, layer, o_ref = self.ln_refs
            rows = self.halves_left.pop(0)
            z = z_prev[rows, :D_MODEL]
            mu = jnp.mean(z, axis=-1, keepdims=True)
            zc = z - mu
            var = jnp.mean(zc * zc, axis=-1, keepdims=True)
            o_ref[rows, :] = (zc * lax.rsqrt(var + LN_EPS) * g_ref[layer:layer + 1, :]
                              + b_ref[layer:layer + 1, :])

    def finish(self):
        self.ln_half()
        self.ln_half()
        self.emit(OUT_BLOCKS)


def _lagged_tiles(n_tiles):
    def clamp(i):
        return jnp.clip(i, 0, n_tiles - 1)

    return clamp, (lambda i: clamp(i - 1)), (lambda i: clamp(i - 2))


def _run_stages(step, n_tiles, body, wout_hbm, wout_vmem, wout_sem, mix_a, mix_b, z_a, z_b):
    wout_copy = pltpu.make_async_copy(wout_hbm, wout_vmem, wout_sem.at[0])

    @pl.when(step == 0)
    def _():
        wout_copy.start()

    @pl.when(step == 1)
    def _():
        wout_copy.wait()

    def stage(pred, parity, **stages):
        bufs = (mix_a, mix_b, z_a, z_b) if parity == 0 else (mix_b, mix_a, z_b, z_a)

        @pl.when(pred)
        def _():
            body(*bufs, **stages)

    steady = (step >= 2) & (step < n_tiles)
    stage(step == 0, 0, mix=True, proj=False, ln=False)
    stage(step == 1, 1, mix=True, proj=True, ln=False)
    stage(steady & (step % 2 == 0), 0, mix=True, proj=True, ln=True)
    stage(steady & (step % 2 == 1), 1, mix=True, proj=True, ln=True)
    stage(step == n_tiles, n_tiles % 2, mix=False, proj=True, ln=True)
    stage(step == n_tiles + 1, (n_tiles + 1) % 2, mix=False, proj=False, ln=True)


def _tail_specs(x, w_out_bf, ln_g, ln_b, res, out):
    m, d = x.shape
    in_specs = [
        pl.BlockSpec((MIX_T, d), lambda i: (res(i), 0)),
        pl.BlockSpec(memory_space=pl.ANY),
        _whole(ln_g),
        _whole(ln_b),
    ]
    out_spec = pl.BlockSpec((MIX_T, d), lambda i: (out(i), 0))
    out_shape = jax.ShapeDtypeStruct((m, d), F32)
    scratch = [pltpu.VMEM((MIX_T, MIX_WIDTH + LANES), BF16), pltpu.VMEM((MIX_T, MIX_WIDTH + LANES), BF16),
               pltpu.VMEM((MIX_T, d + LANES), F32), pltpu.VMEM((MIX_T, d + LANES), F32),
               pltpu.VMEM(w_out_bf.shape, BF16), pltpu.SemaphoreType.DMA((1,))]
    return in_specs, out_spec, out_shape, scratch, [x, w_out_bf, ln_g, ln_b]


def _rope_tile(t, cos, sin, take_upper):
    upper = pltpu.roll(t, LANES - ROT_DIM // 2, 1)
    lower = pltpu.roll(t, ROT_DIM // 2, 1)
    return t * cos + jnp.where(take_upper, upper, lower) * sin


EVEN_WOUT_ARG = 10
ODD_WOUT_ARG = 11


def _even_layer_kernel(*refs, layer, n_tiles, n_s):
    *io_refs, mix_a, mix_b, z_a, z_b, wout_vmem, wout_sem = refs
    wout_hbm = io_refs[EVEN_WOUT_ARG]
    io_refs[EVEN_WOUT_ARG] = wout_vmem
    step = pl.program_id(0)
    seq_tile = jnp.minimum(step, n_tiles - 1) % n_s
    _run_stages(step, n_tiles, partial(_even_step, *io_refs, layer=layer, seq_tile=seq_tile),
                wout_hbm, wout_vmem, wout_sem, mix_a, mix_b, z_a, z_b)


def _even_step(sinks_ref, h_ref, kvp_ref, tabc_ref, tabp_ref, lng_ref, lnb_ref, ws_ref, bst_ref,
               x_ref, wout_ref, g_ref, b_ref, o_ref, mix_ref, mix_prev, z_ref, z_prev,
               *, layer, seq_tile, mix, proj, ln):
    u_ref, v_ref, ag_ref, q_ref = (h_ref.at[:, c * A_WIDTH:(c + 1) * A_WIDTH] for c in range(4))
    kvc_ref = h_ref.at[:, KV_COL * KV_BLOCK:(KV_COL + 1) * KV_BLOCK]
    bg_refs = [h_ref.at[:, (GATE_B_COL + c) * KV_BLOCK:(GATE_B_COL + c + 1) * KV_BLOCK]
               for c in range(B_WIDTH // KV_BLOCK)]
    cosc_ref, sinc_ref = tabc_ref.at[:, :LANES], tabc_ref.at[:, LANES:]
    cosp_ref, sinp_ref = tabp_ref.at[:, :LANES], tabp_ref.at[:, LANES:]
    t_rows = h_ref.shape[0]
    n_chunks = t_rows // CHUNK
    tail = _Tail(mix_prev, wout_ref, x_ref, z_ref, z_prev, g_ref, b_ref, layer, o_ref, proj, ln)
    if not mix:
        tail.finish()
        return
    tail.emit()

    v = v_ref[...]
    mu = jnp.mean(v, axis=-1, keepdims=True)
    vc = v - mu
    var = jnp.mean(vc * vc, axis=-1, keepdims=True)
    vln = (vc * lax.rsqrt(var + LN_EPS) * lng_ref[layer:layer + 1, :]
           + lnb_ref[layer:layer + 1, :]).astype(BF16)
    causal = (lax.broadcasted_iota(jnp.int32, (CHUNK, CHUNK), 0)
              >= lax.broadcasted_iota(jnp.int32, (CHUNK, CHUNK), 1))
    lane = lax.broadcasted_iota(jnp.int32, (1, LANES), 1)
    take_upper = (lane % B_HEAD_DIM) < (ROT_DIM // 2)
    low_half = lane < B_HEAD_DIM
    cosc = cosc_ref[...]
    sinc = sinc_ref[...]
    scale = B_HEAD_DIM ** -0.5
    q_tiles = []
    for g in range(A_GROUPS):
        cols = slice(g * CHUNK, (g + 1) * CHUNK)
        w = jnp.where(causal, ws_ref[g], 0.0).astype(BF16)
        rhs = jnp.concatenate(
            [vln[c * CHUNK:(c + 1) * CHUNK, cols] for c in range(n_chunks)], axis=1)
        mixed = jnp.dot(w, rhs, preferred_element_type=F32) + bst_ref[:, g:g + 1]
        for c in range(n_chunks):
            rows = slice(c * CHUNK, (c + 1) * CHUNK)
            gate = _silu(ag_ref[rows, cols])
            mix_ref[rows, cols] = (u_ref[rows, cols] * mixed[:, c * CHUNK:(c + 1) * CHUNK]
                                   * gate).astype(BF16)
        q_tiles.append(
            (_rope_tile(q_ref[:, cols], cosc, sinc, take_upper) * scale).astype(BF16))
        if g in (1, 5):
            tail.ln_half()
        if g in (2, 4, 6):
            tail.emit()

    k_all = jnp.concatenate(
        [_rope_tile(kvp_ref[:, :B_KV_WIDTH], cosp_ref[...], sinp_ref[...], take_upper),
         _rope_tile(kvc_ref[:, :B_KV_WIDTH], cosc, sinc, take_upper)], axis=0)
    k_swap = pltpu.roll(k_all, B_HEAD_DIM, 1)
    zero = jnp.zeros_like(k_all)
    k_lo = (jnp.where(low_half, k_all, zero).astype(BF16), jnp.where(low_half, k_swap, zero).astype(BF16))
    k_hi = (jnp.where(low_half, zero, k_swap).astype(BF16), jnp.where(low_half, zero, k_all).astype(BF16))
    v_all = jnp.concatenate([kvp_ref[:, B_KV_WIDTH:], kvc_ref[:, B_KV_WIDTH:]], axis=0)
    vt_all = v_all.T.astype(BF16)

    tiles_per_kv = B_WIDTH // LANES // B_KV_HEADS
    band_keys = 2 * WINDOW
    kj = lax.broadcasted_iota(jnp.int32, (band_keys, WINDOW), 0)
    qi = lax.broadcasted_iota(jnp.int32, (band_keys, WINDOW), 1)
    diff = qi + WINDOW - kj
    band = (diff >= 0) & (diff < WINDOW)
    first_key = jnp.where(seq_tile == 0, WINDOW, 0)
    neg_inf = jnp.full((band_keys, WINDOW), -jnp.inf, F32)
    mask_any = jnp.where(band, 0.0, neg_inf)
    mask_first = jnp.where(band & (kj >= first_key), 0.0, neg_inf)

    for n in range(t_rows // WINDOW):
        rows = slice(n * WINDOW, (n + 1) * WINDOW)
        band_rows = slice(n * WINDOW, (n + 2) * WINDOW)
        mask = jnp.concatenate([mask_first if n == 0 else mask_any] * tiles_per_kv, axis=1)
        for kv in range(B_KV_HEADS):
            tail.emit()
            q4 = jnp.concatenate(
                [q_tiles[kv * tiles_per_kv + j][rows] for j in range(tiles_per_kv)], axis=0)
            k_cat = jnp.concatenate([k_lo[kv][band_rows], k_hi[kv][band_rows]], axis=0)
            st = lax.dot_general(k_cat, q4, (((1,), (1,)), ((), ())), preferred_element_type=F32)
            vt = vt_all[kv * B_HEAD_DIM:(kv + 1) * B_HEAD_DIM, band_rows]
            o_halves = []
            for half in range(2):
                sh = st[half * band_keys:(half + 1) * band_keys] + mask
                sink = jnp.concatenate(
                    [jnp.full((1, WINDOW), sinks_ref[layer, 2 * (kv * tiles_per_kv + j) + half], F32)
                     for j in range(tiles_per_kv)], axis=1)
                m = jnp.maximum(jnp.max(sh, axis=0, keepdims=True), sink)
                p = jnp.exp(sh - m)
                l = jnp.sum(p, axis=0, keepdims=True) + jnp.exp(sink - m)
                o_half = jnp.dot(vt, p.astype(BF16), preferred_element_type=F32)
                o_halves.append(o_half * (1.0 / l))
            ot = jnp.concatenate(o_halves, axis=0)
            for j in range(tiles_per_kv):
                jj = kv * tiles_per_kv + j
                gate_ref = bg_refs[jj // 2]
                gate_cols = slice((jj % 2) * LANES, (jj % 2 + 1) * LANES)
                out_cols = slice(A_WIDTH + jj * LANES, A_WIDTH + (jj + 1) * LANES)
                o_tile = ot[:, j * WINDOW:(j + 1) * WINDOW].T
                mix_ref[rows, out_cols] = (o_tile * _silu(gate_ref[rows, gate_cols])).astype(BF16)

    tail.finish()


def _even_layer(h, x, sinks, rope_tab, a_ln_g, a_ln_b, ws, bst, w_out_bf, ln_g, ln_b, layer, seq):
    m = h.shape[0]
    t = MIX_T
    n_tiles = m // t
    n_s = seq // t
    blocks_per_tile = t // WINDOW
    cur, res, out = _lagged_tiles(n_tiles)

    def prev(i):
        return jnp.maximum(cur(i) * blocks_per_tile - 1, 0)

    tail_in, out_specs, out_shape, tail_scratch, tail_args = _tail_specs(x, w_out_bf, ln_g, ln_b, res, out)
    return pl.pallas_call(
        partial(_even_layer_kernel, layer=layer, n_tiles=n_tiles, n_s=n_s),
        grid=(n_tiles + 2,),
        in_specs=[
            pl.BlockSpec(memory_space=pltpu.SMEM),
            pl.BlockSpec((t, h.shape[1]), lambda i: (cur(i), 0)),
            pl.BlockSpec((WINDOW, KV_BLOCK), lambda i: (prev(i), KV_COL)),
            pl.BlockSpec((t, 2 * LANES), lambda i: (cur(i), 0)),
            pl.BlockSpec((WINDOW, 2 * LANES), lambda i: (prev(i), 0)),
            _whole(a_ln_g),
            _whole(a_ln_b),
            _layer_block((A_GROUPS, CHUNK, CHUNK), layer),
            _layer_block((CHUNK, A_GROUPS), layer),
        ] + tail_in,
        out_specs=out_specs,
        out_shape=out_shape,
        scratch_shapes=tail_scratch,
        compiler_params=_compiler_params("arbitrary"),
        name="even_layer",
    )(sinks, h, h, rope_tab, rope_tab, a_ln_g, a_ln_b, ws, bst, *tail_args)


def _odd_layer_kernel(*refs, layer, n_tiles, n_s):
    *io_refs, mix_a, mix_b, z_a, z_b, wout_vmem, wout_sem = refs[:21]
    (ext_c, ext_d, lvl2, lvl4, hstate, sa, sb, sh, wax_bf, wpool_bf) = scratch = refs[21:]
    wout_hbm = io_refs[ODD_WOUT_ARG]
    io_refs[ODD_WOUT_ARG] = wout_vmem
    wa_ref, wx_ref, wpool_ref = io_refs[3], io_refs[4], io_refs[8]
    t_rows = io_refs[0].shape[0]
    step = pl.program_id(0)
    s_idx = jnp.minimum(step, n_tiles - 1) % n_s

    @pl.when(s_idx == 0)
    def _():
        ext_c[:, 0:CONV_HALO, :] = jnp.zeros((C_HEADS, CONV_HALO, LANES), F32)
        ext_d[:, 0:POOL_HALO, :] = jnp.zeros((D_WIDTH // LANES, POOL_HALO, LANES), F32)
        hstate[...] = jnp.zeros_like(hstate)
        wax_bf[:, :, :C_HEAD_DIM] = wa_ref[...].astype(BF16)
        wax_bf[:, :, C_HEAD_DIM:] = wx_ref[...].astype(BF16)
        wpool_bf[...] = wpool_ref[...].astype(BF16)

    @pl.when(s_idx > 0)
    def _():
        ext_c[:, 0:CONV_HALO, :] = ext_c[:, t_rows:t_rows + CONV_HALO, :]
        ext_d[:, 0:POOL_HALO, :] = ext_d[:, t_rows:t_rows + POOL_HALO, :]

    _run_stages(step, n_tiles, partial(_odd_step, *io_refs, *scratch, layer=layer, s_idx=s_idx),
                wout_hbm, wout_vmem, wout_sem, mix_a, mix_b, z_a, z_b)


def _odd_step(h_ref, convw_ref, convb_ref, wa_ref, wx_ref, ba_ref, bx_ref,
              lam_ref, wpool_ref, dscale_ref, x_ref, wout_ref, g_ref, b_ref, o_ref,
              ext_c, ext_d, lvl2, lvl4, hstate, sa, sb, sh, wax_bf, wpool_bf,
              mix_ref, mix_prev, z_ref, z_prev, *, layer, s_idx, mix, proj, ln):
    xc_ref, cg_ref, xd_ref, dg_ref = (h_ref.at[:, c * C_WIDTH:(c + 1) * C_WIDTH] for c in range(4))
    t_rows = h_ref.shape[0]
    row = slice(layer, layer + 1)
    n_groups = t_rows // SUBLANES
    tail = _Tail(mix_prev, wout_ref, x_ref, z_ref, z_prev, g_ref, b_ref, layer, o_ref, proj, ln)
    if not mix:
        tail.finish()
        return
    tail.emit()

    z = -lam_ref[row, :]
    softplus = jnp.maximum(z, 0.0) + jnp.log1p(jnp.exp(-jnp.abs(z)))
    for hd in range(C_HEADS):
        cols = slice(hd * C_HEAD_DIM, (hd + 1) * C_HEAD_DIM)
        ext_c[hd, CONV_HALO:, :] = xc_ref[:, cols]
        xconv = convb_ref[row, cols]
        for j in range(CONV_WIDTH):
            lo = CONV_HALO - (CONV_WIDTH - 1) + j
            xconv = xconv + ext_c[hd, lo:lo + t_rows, :] * convw_ref[j:j + 1, cols]
        ri = jnp.dot(xconv.astype(BF16), wax_bf[hd], preferred_element_type=F32)
        r = _sigmoid(ri[:, :C_HEAD_DIM] + ba_ref[row, cols])
        i = _sigmoid(ri[:, C_HEAD_DIM:] + bx_ref[row, cols])
        log_a = -LRU_C * r * softplus[:, cols]
        a = jnp.exp(log_a)
        m2 = -jnp.tanh(log_a) * (a * a + 1.0)
        mult = jnp.where(m2 == 0.0, 0.0, m2 * lax.rsqrt(m2))
        b = mult * (i * xconv)
        for g in range(n_groups):
            base = (g * C_HEADS + hd) * SCAN_PITCH
            sa[base:base + SUBLANES, :] = a[g * SUBLANES:(g + 1) * SUBLANES]
            sb[base:base + SUBLANES, :] = b[g * SUBLANES:(g + 1) * SUBLANES]
        if hd in (2, 5):
            tail.ln_half()
        if hd in (1, 3, 4, 6, 7):
            tail.emit()

    h = hstate[...]
    for t in range(t_rows):
        g, s = divmod(t, SUBLANES)
        at_t = pl.ds(g * C_HEADS * SCAN_PITCH + s, C_HEADS, stride=SCAN_PITCH)
        h = sa[at_t, :] * h + sb[at_t, :]
        sh[at_t, :] = h
    hstate[...] = h
    tail.emit()
    for hd in range(C_HEADS):
        cols = slice(hd * C_HEAD_DIM, (hd + 1) * C_HEAD_DIM)
        hcol = jnp.concatenate(
            [sh[(g * C_HEADS + hd) * SCAN_PITCH:(g * C_HEADS + hd) * SCAN_PITCH + SUBLANES, :]
             for g in range(n_groups)], axis=0)
        mix_ref[:, cols] = (hcol * _silu(cg_ref[:, cols])).astype(BF16)

    n_ext = POOL_HALO + t_rows
    tiles_per_group = D_GROUP_DIM // LANES
    pos1 = s_idx * t_rows + lax.broadcasted_iota(jnp.int32, (t_rows, LANES), 0) + 1
    for g, w in enumerate(POOL_WINDOWS):
        if g == 2:
            tail.emit()
        inv = 1.0 / jnp.minimum(pos1, w).astype(F32)
        pooled = []
        for k in range(tiles_per_group):
            tile = g * tiles_per_group + k
            cols = slice(tile * LANES, (tile + 1) * LANES)
            ext_d[tile, POOL_HALO:, :] = xd_ref[:, cols]
            lo = SUBLANES
            win = ext_d[tile, lo:n_ext, :] + ext_d[tile, lo - 1:n_ext - 1, :]
            if w >= 4:
                lvl2[k, lo:n_ext, :] = win
                lo += SUBLANES
                win = lvl2[k, lo:n_ext, :] + lvl2[k, lo - 2:n_ext - 2, :]
            if w >= 8:
                lvl4[k, lo:n_ext, :] = win
                lo += SUBLANES
                win = lvl4[k, lo:n_ext, :] + lvl4[k, lo - 4:n_ext - 4, :]
            if w >= 16:
                lo += SUBLANES
                win = win[SUBLANES:] + win[:-SUBLANES]
            pooled.append((win[POOL_HALO - lo:] * inv - xd_ref[:, cols]).astype(BF16))
        cols = slice(g * D_GROUP_DIM, (g + 1) * D_GROUP_DIM)
        mixed = jnp.dot(jnp.concatenate(pooled, axis=1), wpool_bf[g], preferred_element_type=F32)
        out_cols = slice(C_WIDTH + g * D_GROUP_DIM, C_WIDTH + (g + 1) * D_GROUP_DIM)
        mix_ref[:, out_cols] = (mixed * dscale_ref[row, cols] * _silu(dg_ref[:, cols])).astype(BF16)

    tail.finish()


def _odd_layer(h, x, conv_w, conv_b, w_a, w_x, b_a, b_x, lam, w_pool, d_scale, w_out_bf, ln_g, ln_b,
               layer, seq):
    m = h.shape[0]
    t = MIX_T
    n_tiles = m // t
    n_s = seq // t
    cur, res, out = _lagged_tiles(n_tiles)
    n_pool = len(POOL_WINDOWS)
    tail_in, out_specs, out_shape, tail_scratch, tail_args = _tail_specs(x, w_out_bf, ln_g, ln_b, res, out)
    return pl.pallas_call(
        partial(_odd_layer_kernel, layer=layer, n_tiles=n_tiles, n_s=n_s),
        grid=(n_tiles + 2,),
        in_specs=[
            pl.BlockSpec((t, h.shape[1]), lambda i: (cur(i), 0)),
            _layer_block((CONV_WIDTH, C_WIDTH), layer),
            _whole(conv_b),
            _layer_block((C_HEADS, C_HEAD_DIM, C_HEAD_DIM), layer),
            _layer_block((C_HEADS, C_HEAD_DIM, C_HEAD_DIM), layer),
            _whole(b_a), _whole(b_x), _whole(lam),
            _layer_block((n_pool, D_GROUP_DIM, D_GROUP_DIM), layer),
            _whole(d_scale),
        ] + tail_in,
        out_specs=out_specs,
        out_shape=out_shape,
        scratch_shapes=tail_scratch + [
            pltpu.VMEM((C_WIDTH // LANES, CONV_HALO + t, LANES), F32),
            pltpu.VMEM((D_WIDTH // LANES, POOL_HALO + t, LANES), F32),
            pltpu.VMEM((D_GROUP_DIM // LANES, POOL_HALO + t, LANES), F32),
            pltpu.VMEM((D_GROUP_DIM // LANES, POOL_HALO + t, LANES), F32),
            pltpu.VMEM((SUBLANES, LANES), F32),
            pltpu.VMEM((t // SUBLANES * C_HEADS * SCAN_PITCH, LANES), F32),
            pltpu.VMEM((t // SUBLANES * C_HEADS * SCAN_PITCH, LANES), F32),
            pltpu.VMEM((t // SUBLANES * C_HEADS * SCAN_PITCH, LANES), F32),
            pltpu.VMEM((C_HEADS, C_HEAD_DIM, 2 * C_HEAD_DIM), BF16),
            pltpu.VMEM((n_pool, D_GROUP_DIM, D_GROUP_DIM), BF16),
        ],
        compiler_params=_compiler_params("arbitrary"),
        name="odd_layer",
    )(h, conv_w, conv_b, w_a, w_x, b_a, b_x, lam, w_pool, d_scale, *tail_args)


def _rope_coefficients():
    half = ROT_DIM // 2
    inv_freq = ROPE_THETA ** (-jnp.arange(0, ROT_DIM, 2, dtype=F32) / ROT_DIM)
    rest = jnp.zeros((B_HEAD_DIM - ROT_DIM,), F32)
    freq_head = jnp.concatenate([inv_freq, inv_freq, rest])
    sign_head = jnp.concatenate([-jnp.ones((half,), F32), jnp.ones((half,), F32), rest])
    reps = LANES // B_HEAD_DIM
    return jnp.stack([jnp.tile(freq_head, reps), jnp.tile(sign_head, reps)])


def kernel(x, positions, even_w_in, even_a_ln_g, even_a_ln_b, even_a_ws, even_a_bs, even_b_sinks, even_w_out, even_ln_g, even_ln_b, odd_w_in, odd_conv_w, odd_conv_b, odd_w_a, odd_b_a, odd_w_x, odd_b_x, odd_lam, odd_w_pool, odd_d_scale, odd_w_out, odd_ln_g, odd_ln_b):
    batch, seq, d = x.shape
    m = batch * seq
    xf = x.reshape(m, d)
    rope_tab = None
    even_bst = jnp.swapaxes(even_a_bs, 1, 2)
    w_in, w_layer = even_w_in, 0
    for layer in range(DEPTH):
        j = layer // 2
        if layer % 2 == 0:
            h, w_out, w_in, tab = _proj_in(xf, w_in, w_layer, even_w_out, j, odd_w_in, j,
                                           positions if rope_tab is None else None)
            rope_tab = tab if rope_tab is None else rope_tab
            xf = _even_layer(h, xf, even_b_sinks, rope_tab, even_a_ln_g, even_a_ln_b, even_a_ws,
                             even_bst, w_out, even_ln_g, even_ln_b, j, seq)
        else:
            w_next = even_w_in if layer + 1 < DEPTH else None
            h, w_out, w_in, _ = _proj_in(xf, w_in, w_layer, odd_w_out, j, w_next, j + 1)
            xf = _odd_layer(h, xf, odd_conv_w, odd_conv_b, odd_w_a, odd_w_x, odd_b_a, odd_b_x, odd_lam,
                            odd_w_pool, odd_d_scale, w_out, odd_ln_g, odd_ln_b, j, seq)
        w_layer = None
    return xf.reshape(batch, seq, d)
```

```python
from functools import partial

import jax
import jax.numpy as jnp
from jax import lax
from jax.experimental import pallas as pl
from jax.experimental.pallas import tpu as pltpu

D_MODEL = 2048
DEPTH = 4
A_WIDTH = 1024
A_GROUPS = 8
CHUNK = 128
B_HEAD_DIM = 64
B_Q_HEADS = 16
B_KV_HEADS = 2
B_WIDTH = 1024
B_KV_WIDTH = 128
WINDOW = 128
ROT_DIM = 16
ROPE_THETA = 500000.0
C_WIDTH = 1024
C_HEADS = 8
C_HEAD_DIM = 128
CONV_WIDTH = 4
LRU_C = 8.0
D_WIDTH = 1024
POOL_WINDOWS = (2, 4, 8, 16)
D_GROUP_DIM = 256
MIX_WIDTH = 2048
DN_ALPHA = (2 * DEPTH) ** 0.25
LN_EPS = 1e-5

LANES = 128
SUBLANES = 8
VMEM_LIMIT_BYTES = 56 * 1024 * 1024

PROJ_TM = 256
PROJ_VMEM_BYTES = 44 * 1024 * 1024
MIX_T = 256
OUT_BLOCK = 256
OUT_BLOCKS = D_MODEL // OUT_BLOCK
W_CHUNK = 256
CONV_HALO = SUBLANES
POOL_HALO = 4 * SUBLANES
SCAN_PITCH = SUBLANES + 1
ROW_PAD = LANES

KV_BLOCK = 2 * B_KV_WIDTH
KV_COL = (3 * A_WIDTH + B_WIDTH) // KV_BLOCK
GATE_B_COL = KV_COL + 1

F32 = jnp.float32
BF16 = jnp.bfloat16


def _sigmoid(x):
    return 0.5 * jnp.tanh(0.5 * x) + 0.5


def _silu(x):
    hx = 0.5 * x
    return hx * jnp.tanh(hx) + hx


def _resident(shape):
    return pl.BlockSpec(shape, lambda *_: (0,) * len(shape), pipeline_mode=pl.Buffered(1))


def _whole(arr):
    return pl.BlockSpec(arr.shape, lambda *_: (0,) * arr.ndim)


def _layer_block(shape, layer):
    return pl.BlockSpec((None,) + shape, lambda *_: (layer,) + (0,) * len(shape))


def _compiler_params(*semantics):
    return pltpu.CompilerParams(dimension_semantics=semantics, vmem_limit_bytes=VMEM_LIMIT_BYTES)


def _proj_in_kernel(*refs, cast_next, w_layer, rope):
    n_in = 3 + cast_next + 2 * rope
    n_out = 2 + cast_next + rope
    x_ref, w_hbm, wout_ref = refs[:3]
    opad_ref, wout_bf_ref = refs[n_in:n_in + 2]
    staged = w_hbm.dtype != BF16
    w_vmem, *stage, sems = refs[n_in + n_out:]
    n = w_vmem.shape[1]
    o_ref = opad_ref.at[:, :n]

    def side_jobs():
        opad_ref[:, n:] = jnp.zeros((opad_ref.shape[0], ROW_PAD), F32)
        if cast_next:
            refs[n_in + 2][...] = refs[3][...].astype(BF16)
        for p in range(OUT_BLOCKS):
            wout_bf_ref[p] = wout_ref[:, p * OUT_BLOCK:(p + 1) * OUT_BLOCK].astype(BF16)
        if rope:
            pos_ref, coef_ref = refs[n_in - 2:n_in]
            tab_ref = refs[n_in + n_out - 1]
            ang = pos_ref[...].astype(F32) * coef_ref[0:1, :]
            tab_ref[:, :LANES] = jnp.cos(ang)
            tab_ref[:, LANES:] = jnp.sin(ang) * coef_ref[1:2, :]

    n_chunks = w_vmem.shape[0] // W_CHUNK
    w_src = w_hbm if w_layer is None else w_hbm.at[w_layer]

    def chunk_copy(c):
        rows = pl.ds(c * W_CHUNK, W_CHUNK)
        if staged:
            return pltpu.make_async_copy(w_src.at[rows, :], stage[0].at[c % 2], sems.at[c % 2])
        return pltpu.make_async_copy(w_src.at[rows, :], w_vmem.at[rows, :], sems.at[c])

    step = pl.program_id(0)

    @pl.when(step == 0)
    def _():
        xb = x_ref[...].astype(BF16)
        for c in range(2 if staged else n_chunks):
            chunk_copy(c).start()
        side_jobs()
        for c in range(n_chunks):
            rows = slice(c * W_CHUNK, (c + 1) * W_CHUNK)
            chunk_copy(c).wait()
            if staged:
                w_vmem[rows, :] = stage[0][c % 2].astype(BF16)
                if c + 2 < n_chunks:
                    chunk_copy(c + 2).start()
            part = jnp.dot(xb[:, rows], w_vmem[rows, :], preferred_element_type=F32)
            if c == 0:
                o_ref[...] = part
            else:
                o_ref[...] += part

    @pl.when(step > 0)
    def _():
        side_jobs()
        o_ref[...] = jnp.dot(x_ref[...].astype(BF16), w_vmem[...], preferred_element_type=F32)


def _proj_in(x, w, w_layer, w_out, layer, w_next, layer_next, positions=None):
    m, k = x.shape
    n = w.shape[-1]
    wide = k * n * 2 + 2 * (2 * PROJ_TM) * (k + n + ROW_PAD) * 4 <= PROJ_VMEM_BYTES
    tm = 2 * PROJ_TM if wide and m % (2 * PROJ_TM) == 0 else PROJ_TM
    steps = m // tm

    def slab_specs(stacked, which):
        rows, cols = stacked.shape[1:]
        slab = rows // steps
        return (pl.BlockSpec((None, slab, cols), lambda i: (which, i, 0)),
                pl.BlockSpec((slab, cols), lambda i: (i, 0)),
                jax.ShapeDtypeStruct((rows, cols), BF16))

    k_out = w_out.shape[1]
    slab_out = k_out // steps
    casts = [(pl.BlockSpec((None, slab_out, D_MODEL), lambda i: (layer, i, 0)),
              pl.BlockSpec((OUT_BLOCKS, slab_out, OUT_BLOCK), lambda i: (0, i, 0)),
              jax.ShapeDtypeStruct((OUT_BLOCKS, k_out, OUT_BLOCK), BF16))]
    args = [x, w, w_out]
    if w_next is not None:
        casts.append(slab_specs(w_next, layer_next))
        args.append(w_next)
    rope = positions is not None
    if rope:
        casts.append((pl.BlockSpec((tm, 1), lambda i: (i, 0)), None, None))
        coef = _rope_coefficients()
        casts.append((_whole(coef), pl.BlockSpec((tm, 2 * LANES), lambda i: (i, 0)),
                      jax.ShapeDtypeStruct((m, 2 * LANES), F32)))
        args += [positions.reshape(m, 1), coef]
    staged = w.dtype != BF16
    scratch = [pltpu.VMEM((k, n), BF16)]
    if staged:
        scratch.append(pltpu.VMEM((2, W_CHUNK, n), w.dtype))
    scratch.append(pltpu.SemaphoreType.DMA((2 if staged else k // W_CHUNK,)))
    outs = pl.pallas_call(
        partial(_proj_in_kernel, cast_next=w_next is not None, w_layer=w_layer, rope=rope),
        grid=(steps,),
        in_specs=[pl.BlockSpec((tm, k), lambda i: (i, 0)), pl.BlockSpec(memory_space=pl.ANY)]
        + [c[0] for c in casts],
        out_specs=[pl.BlockSpec((tm, n + ROW_PAD), lambda i: (i, 0))] + [c[1] for c in casts if c[1]],
        out_shape=[jax.ShapeDtypeStruct((m, n + ROW_PAD), F32)] + [c[2] for c in casts if c[2]],
        scratch_shapes=scratch,
        compiler_params=_compiler_params("arbitrary"),
        name="proj_in",
    )(*args)
    h, w_out_bf, *rest = outs
    return h, w_out_bf, rest.pop(0) if w_next is not None else None, rest.pop(0) if rope else None


class _Tail:
    def __init__(self, mix_prev, wout_ref, x_ref, z_ref, z_prev, g_ref, b_ref, layer, o_ref,
                 proj, ln):
        self.mix_ref = mix_prev.at[:, :MIX_WIDTH]
        self.proj_refs = (wout_ref, x_ref, z_ref)
        self.ln_refs = (z_prev, g_ref, b_ref, layer, o_ref)
        self.blocks_left = list(range(OUT_BLOCKS)) if proj else []
        rows = z_ref.shape[0]
        self.halves_left = [slice(r * rows // 2, (r + 1) * rows // 2) for r in range(2)] if ln else []

    def emit(self, count=1):
        wout_ref, x_ref, z_ref = self.proj_refs
        for _ in range(min(count, len(self.blocks_left))):
            blk = self.blocks_left.pop(0)
            cols = slice(blk * OUT_BLOCK, (blk + 1) * OUT_BLOCK)
            z_ref[:, cols] = DN_ALPHA * x_ref[:, cols] + jnp.dot(
                self.mix_ref[...], wout_ref[blk], preferred_element_type=F32)

    def ln_half(self):
        if self.halves_left:
            z_prev, g_ref, b_ref, layer, o_ref = self.ln_refs
            rows = self.halves_left.pop(0)
            z = z_prev[rows, :D_MODEL]
            mu = jnp.mean(z, axis=-1, keepdims=True)
            zc = z - mu
            var = jnp.mean(zc * zc, axis=-1, keepdims=True)
            o_ref[rows, :] = (zc * lax.rsqrt(var + LN_EPS) * g_ref[layer:layer + 1, :]
                              + b_ref[layer:layer + 1, :])

    def finish(self):
        self.ln_half()
        self.ln_half()
        self.emit(OUT_BLOCKS)


def _lagged_tiles(n_tiles):
    def clamp(i):
        return jnp.clip(i, 0, n_tiles - 1)

    return clamp, (lambda i: clamp(i - 1)), (lambda i: clamp(i - 2))


def _run_stages(step, n_tiles, body, wout_hbm, wout_vmem, wout_sem, mix_a, mix_b, z_a, z_b):
    wout_copy = pltpu.make_async_copy(wout_hbm, wout_vmem, wout_sem.at[0])

    @pl.when(step == 0)
    def _():
        wout_copy.start()

    @pl.when(step == 1)
    def _():
        wout_copy.wait()

    def stage(pred, parity, **stages):
        bufs = (mix_a, mix_b, z_a, z_b) if parity == 0 else (mix_b, mix_a, z_b, z_a)

        @pl.when(pred)
        def _():
            body(*bufs, **stages)

    steady = (step >= 2) & (step < n_tiles)
    stage(step == 0, 0, mix=True, proj=False, ln=False)
    stage(step == 1, 1, mix=True, proj=True, ln=False)
    stage(steady & (step % 2 == 0), 0, mix=True, proj=True, ln=True)
    stage(steady & (step % 2 == 1), 1, mix=True, proj=True, ln=True)
    stage(step == n_tiles, n_tiles % 2, mix=False, proj=True, ln=True)
    stage(step == n_tiles + 1, (n_tiles + 1) % 2, mix=False, proj=False, ln=True)


def _tail_specs(x, w_out_bf, ln_g, ln_b, res, out):
    m, d = x.shape
    in_specs = [
        pl.BlockSpec((MIX_T, d), lambda i: (res(i), 0)),
        pl.BlockSpec(memory_space=pl.ANY),
        _whole(ln_g),
        _whole(ln_b),
    ]
    out_spec = pl.BlockSpec((MIX_T, d), lambda i: (out(i), 0))
    out_shape = jax.ShapeDtypeStruct((m, d), F32)
    scratch = [pltpu.VMEM((MIX_T, MIX_WIDTH + ROW_PAD), BF16), pltpu.VMEM((MIX_T, MIX_WIDTH + ROW_PAD), BF16),
               pltpu.VMEM((MIX_T, d + ROW_PAD), F32), pltpu.VMEM((MIX_T, d + ROW_PAD), F32),
               pltpu.VMEM(w_out_bf.shape, BF16), pltpu.SemaphoreType.DMA((1,))]
    return in_specs, out_spec, out_shape, scratch, [x, w_out_bf, ln_g, ln_b]


def _rope_tile(t, cos, sin, take_upper):
    upper = pltpu.roll(t, LANES - ROT_DIM // 2, 1)
    lower = pltpu.roll(t, ROT_DIM // 2, 1)
    return t * cos + jnp.where(take_upper, upper, lower) * sin


EVEN_WOUT_ARG = 10
ODD_WOUT_ARG = 11


def _even_layer_kernel(*refs, layer, n_tiles, n_s):
    *io_refs, mix_a, mix_b, z_a, z_b, wout_vmem, wout_sem = refs
    wout_hbm = io_refs[EVEN_WOUT_ARG]
    io_refs[EVEN_WOUT_ARG] = wout_vmem
    step = pl.program_id(0)
    seq_tile = jnp.minimum(step, n_tiles - 1) % n_s
    _run_stages(step, n_tiles, partial(_even_step, *io_refs, layer=layer, seq_tile=seq_tile),
                wout_hbm, wout_vmem, wout_sem, mix_a, mix_b, z_a, z_b)


def _even_step(sinks_ref, h_ref, kvp_ref, tabc_ref, tabp_ref, lng_ref, lnb_ref, ws_ref, bst_ref,
               x_ref, wout_ref, g_ref, b_ref, o_ref, mix_ref, mix_prev, z_ref, z_prev,
               *, layer, seq_tile, mix, proj, ln):
    u_ref, v_ref, ag_ref, q_ref = (h_ref.at[:, c * A_WIDTH:(c + 1) * A_WIDTH] for c in range(4))
    kvc_ref = h_ref.at[:, KV_COL * KV_BLOCK:(KV_COL + 1) * KV_BLOCK]
    bg_refs = [h_ref.at[:, (GATE_B_COL + c) * KV_BLOCK:(GATE_B_COL + c + 1) * KV_BLOCK]
               for c in range(B_WIDTH // KV_BLOCK)]
    cosc_ref, sinc_ref = tabc_ref.at[:, :LANES], tabc_ref.at[:, LANES:]
    cosp_ref, sinp_ref = tabp_ref.at[:, :LANES], tabp_ref.at[:, LANES:]
    t_rows = h_ref.shape[0]
    n_chunks = t_rows // CHUNK
    tail = _Tail(mix_prev, wout_ref, x_ref, z_ref, z_prev, g_ref, b_ref, layer, o_ref, proj, ln)
    if not mix:
        tail.finish()
        return
    tail.emit()

    v = v_ref[...]
    mu = jnp.mean(v, axis=-1, keepdims=True)
    vc = v - mu
    var = jnp.mean(vc * vc, axis=-1, keepdims=True)
    vln = (vc * lax.rsqrt(var + LN_EPS) * lng_ref[layer:layer + 1, :]
           + lnb_ref[layer:layer + 1, :]).astype(BF16)
    causal = (lax.broadcasted_iota(jnp.int32, (CHUNK, CHUNK), 0)
              >= lax.broadcasted_iota(jnp.int32, (CHUNK, CHUNK), 1))
    lane = lax.broadcasted_iota(jnp.int32, (1, LANES), 1)
    take_upper = (lane % B_HEAD_DIM) < (ROT_DIM // 2)
    low_half = lane < B_HEAD_DIM
    cosc = cosc_ref[...]
    sinc = sinc_ref[...]
    scale = B_HEAD_DIM ** -0.5
    q_tiles = []
    for g in range(A_GROUPS):
        cols = slice(g * CHUNK, (g + 1) * CHUNK)
        w = jnp.where(causal, ws_ref[g], 0.0).astype(BF16)
        rhs = jnp.concatenate(
            [vln[c * CHUNK:(c + 1) * CHUNK, cols] for c in range(n_chunks)], axis=1)
        mixed = jnp.dot(w, rhs, preferred_element_type=F32) + bst_ref[:, g:g + 1]
        for c in range(n_chunks):
            rows = slice(c * CHUNK, (c + 1) * CHUNK)
            gate = _silu(ag_ref[rows, cols])
            mix_ref[rows, cols] = (u_ref[rows, cols] * mixed[:, c * CHUNK:(c + 1) * CHUNK]
                                   * gate).astype(BF16)
        q_tiles.append(
            (_rope_tile(q_ref[:, cols], cosc, sinc, take_upper) * scale).astype(BF16))
        if g in (1, 5):
            tail.ln_half()
        if g in (2, 4, 6):
            tail.emit()

    k_all = jnp.concatenate(
        [_rope_tile(kvp_ref[:, :B_KV_WIDTH], cosp_ref[...], sinp_ref[...], take_upper),
         _rope_tile(kvc_ref[:, :B_KV_WIDTH], cosc, sinc, take_upper)], axis=0)
    k_swap = pltpu.roll(k_all, B_HEAD_DIM, 1)
    zero = jnp.zeros_like(k_all)
    k_lo = (jnp.where(low_half, k_all, zero).astype(BF16), jnp.where(low_half, k_swap, zero).astype(BF16))
    k_hi = (jnp.where(low_half, zero, k_swap).astype(BF16), jnp.where(low_half, zero, k_all).astype(BF16))
    v_all = jnp.concatenate([kvp_ref[:, B_KV_WIDTH:], kvc_ref[:, B_KV_WIDTH:]], axis=0)
    vt_all = v_all.T.astype(BF16)

    tiles_per_kv = B_WIDTH // LANES // B_KV_HEADS
    band_keys = 2 * WINDOW
    kj = lax.broadcasted_iota(jnp.int32, (band_keys, WINDOW), 0)
    qi = lax.broadcasted_iota(jnp.int32, (band_keys, WINDOW), 1)
    diff = qi + WINDOW - kj
    band = (diff >= 0) & (diff < WINDOW)
    first_key = jnp.where(seq_tile == 0, WINDOW, 0)
    neg_inf = jnp.full((band_keys, WINDOW), -jnp.inf, F32)
    mask_any = jnp.where(band, 0.0, neg_inf)
    mask_first = jnp.where(band & (kj >= first_key), 0.0, neg_inf)

    for n in range(t_rows // WINDOW):
        rows = slice(n * WINDOW, (n + 1) * WINDOW)
        band_rows = slice(n * WINDOW, (n + 2) * WINDOW)
        mask = jnp.concatenate([mask_first if n == 0 else mask_any] * tiles_per_kv, axis=1)
        for kv in range(B_KV_HEADS):
            tail.emit()
            q4 = jnp.concatenate(
                [q_tiles[kv * tiles_per_kv + j][rows] for j in range(tiles_per_kv)], axis=0)
            k_cat = jnp.concatenate([k_lo[kv][band_rows], k_hi[kv][band_rows]], axis=0)
            st = lax.dot_general(k_cat, q4, (((1,), (1,)), ((), ())), preferred_element_type=F32)
            vt = vt_all[kv * B_HEAD_DIM:(kv + 1) * B_HEAD_DIM, band_rows]
            o_halves = []
            for half in range(2):
                sh = st[half * band_keys:(half + 1) * band_keys] + mask
                sink = jnp.concatenate(
                    [jnp.full((1, WINDOW), sinks_ref[layer, 2 * (kv * tiles_per_kv + j) + half], F32)
                     for j in range(tiles_per_kv)], axis=1)
                m = jnp.maximum(jnp.max(sh, axis=0, keepdims=True), sink)
                p = jnp.exp(sh - m)
                l = jnp.sum(p, axis=0, keepdims=True) + jnp.exp(sink - m)
                o_half = jnp.dot(vt, p.astype(BF16), preferred_element_type=F32)
                o_halves.append(o_half * (1.0 / l))
            ot = jnp.concatenate(o_halves, axis=0)
            for j in range(tiles_per_kv):
                jj = kv * tiles_per_kv + j
                gate_ref = bg_refs[jj // 2]
                gate_cols = slice((jj % 2) * LANES, (jj % 2 + 1) * LANES)
                out_cols = slice(A_WIDTH + jj * LANES, A_WIDTH + (jj + 1) * LANES)
                o_tile = ot[:, j * WINDOW:(j + 1) * WINDOW].T
                mix_ref[rows, out_cols] = (o_tile * _silu(gate_ref[rows, gate_cols])).astype(BF16)

    tail.finish()


def _even_layer(h, x, sinks, rope_tab, a_ln_g, a_ln_b, ws, bst, w_out_bf, ln_g, ln_b, layer, seq):
    m = h.shape[0]
    t = MIX_T
    n_tiles = m // t
    n_s = seq // t
    blocks_per_tile = t // WINDOW
    cur, res, out = _lagged_tiles(n_tiles)

    def prev(i):
        return jnp.maximum(cur(i) * blocks_per_tile - 1, 0)

    tail_in, out_specs, out_shape, tail_scratch, tail_args = _tail_specs(x, w_out_bf, ln_g, ln_b, res, out)
    return pl.pallas_call(
        partial(_even_layer_kernel, layer=layer, n_tiles=n_tiles, n_s=n_s),
        grid=(n_tiles + 2,),
        in_specs=[
            pl.BlockSpec(memory_space=pltpu.SMEM),
            pl.BlockSpec((t, h.shape[1]), lambda i: (cur(i), 0)),
            pl.BlockSpec((WINDOW, KV_BLOCK), lambda i: (prev(i), KV_COL)),
            pl.BlockSpec((t, 2 * LANES), lambda i: (cur(i), 0)),
            pl.BlockSpec((WINDOW, 2 * LANES), lambda i: (prev(i), 0)),
            _whole(a_ln_g),
            _whole(a_ln_b),
            _layer_block((A_GROUPS, CHUNK, CHUNK), layer),
            _layer_block((CHUNK, A_GROUPS), layer),
        ] + tail_in,
        out_specs=out_specs,
        out_shape=out_shape,
        scratch_shapes=tail_scratch,
        compiler_params=_compiler_params("arbitrary"),
        name="even_layer",
    )(sinks, h, h, rope_tab, rope_tab, a_ln_g, a_ln_b, ws, bst, *tail_args)


def _odd_layer_kernel(*refs, layer, n_tiles, n_s):
    *io_refs, mix_a, mix_b, z_a, z_b, wout_vmem, wout_sem = refs[:21]
    (ext_c, ext_d, lvl2, lvl4, hstate, sa, sb, sh, wax_bf, wpool_bf) = scratch = refs[21:]
    wout_hbm = io_refs[ODD_WOUT_ARG]
    io_refs[ODD_WOUT_ARG] = wout_vmem
    wa_ref, wx_ref, wpool_ref = io_refs[3], io_refs[4], io_refs[8]
    t_rows = io_refs[0].shape[0]
    step = pl.program_id(0)
    s_idx = jnp.minimum(step, n_tiles - 1) % n_s

    @pl.when(s_idx == 0)
    def _():
        ext_c[:, 0:CONV_HALO, :] = jnp.zeros((C_HEADS, CONV_HALO, LANES), F32)
        ext_d[:, 0:POOL_HALO, :] = jnp.zeros((D_WIDTH // LANES, POOL_HALO, LANES), F32)
        hstate[...] = jnp.zeros_like(hstate)
        wax_bf[:, :, :C_HEAD_DIM] = wa_ref[...].astype(BF16)
        wax_bf[:, :, C_HEAD_DIM:] = wx_ref[...].astype(BF16)
        wpool_bf[...] = wpool_ref[...].astype(BF16)

    @pl.when(s_idx > 0)
    def _():
        ext_c[:, 0:CONV_HALO, :] = ext_c[:, t_rows:t_rows + CONV_HALO, :]
        ext_d[:, 0:POOL_HALO, :] = ext_d[:, t_rows:t_rows + POOL_HALO, :]

    _run_stages(step, n_tiles, partial(_odd_step, *io_refs, *scratch, layer=layer, s_idx=s_idx),
                wout_hbm, wout_vmem, wout_sem, mix_a, mix_b, z_a, z_b)


def _odd_step(h_ref, convw_ref, convb_ref, wa_ref, wx_ref, ba_ref, bx_ref,
              lam_ref, wpool_ref, dscale_ref, x_ref, wout_ref, g_ref, b_ref, o_ref,
              ext_c, ext_d, lvl2, lvl4, hstate, sa, sb, sh, wax_bf, wpool_bf,
              mix_ref, mix_prev, z_ref, z_prev, *, layer, s_idx, mix, proj, ln):
    xc_ref, cg_ref, xd_ref, dg_ref = (h_ref.at[:, c * C_WIDTH:(c + 1) * C_WIDTH] for c in range(4))
    t_rows = h_ref.shape[0]
    row = slice(layer, layer + 1)
    n_groups = t_rows // SUBLANES
    tail = _Tail(mix_prev, wout_ref, x_ref, z_ref, z_prev, g_ref, b_ref, layer, o_ref, proj, ln)
    if not mix:
        tail.finish()
        return
    tail.emit()

    z = -lam_ref[row, :]
    softplus = jnp.maximum(z, 0.0) + jnp.log1p(jnp.exp(-jnp.abs(z)))
    for hd in range(C_HEADS):
        cols = slice(hd * C_HEAD_DIM, (hd + 1) * C_HEAD_DIM)
        ext_c[hd, CONV_HALO:, :] = xc_ref[:, cols]
        xconv = convb_ref[row, cols]
        for j in range(CONV_WIDTH):
            lo = CONV_HALO - (CONV_WIDTH - 1) + j
            xconv = xconv + ext_c[hd, lo:lo + t_rows, :] * convw_ref[j:j + 1, cols]
        ri = jnp.dot(xconv.astype(BF16), wax_bf[hd], preferred_element_type=F32)
        r = _sigmoid(ri[:, :C_HEAD_DIM] + ba_ref[row, cols])
        i = _sigmoid(ri[:, C_HEAD_DIM:] + bx_ref[row, cols])
        log_a = -LRU_C * r * softplus[:, cols]
        a = jnp.exp(log_a)
        m2 = -jnp.tanh(log_a) * (a * a + 1.0)
        mult = jnp.where(m2 == 0.0, 0.0, m2 * lax.rsqrt(m2))
        b = mult * (i * xconv)
        for g in range(n_groups):
            base = (g * C_HEADS + hd) * SCAN_PITCH
            sa[base:base + SUBLANES, :] = a[g * SUBLANES:(g + 1) * SUBLANES]
            sb[base:base + SUBLANES, :] = b[g * SUBLANES:(g + 1) * SUBLANES]
        if hd in (2, 5):
            tail.ln_half()
        if hd in (1, 3, 4, 6, 7):
            tail.emit()

    h = hstate[...]
    for t in range(t_rows):
        g, s = divmod(t, SUBLANES)
        at_t = pl.ds(g * C_HEADS * SCAN_PITCH + s, C_HEADS, stride=SCAN_PITCH)
        h = sa[at_t, :] * h + sb[at_t, :]
        sh[at_t, :] = h
    hstate[...] = h
    tail.emit()
    for hd in range(C_HEADS):
        cols = slice(hd * C_HEAD_DIM, (hd + 1) * C_HEAD_DIM)
        hcol = jnp.concatenate(
            [sh[(g * C_HEADS + hd) * SCAN_PITCH:(g * C_HEADS + hd) * SCAN_PITCH + SUBLANES, :]
             for g in range(n_groups)], axis=0)
        mix_ref[:, cols] = (hcol * _silu(cg_ref[:, cols])).astype(BF16)

    n_ext = POOL_HALO + t_rows
    tiles_per_group = D_GROUP_DIM // LANES
    pos1 = s_idx * t_rows + lax.broadcasted_iota(jnp.int32, (t_rows, LANES), 0) + 1
    for g, w in enumerate(POOL_WINDOWS):
        if g == 2:
            tail.emit()
        inv = 1.0 / jnp.minimum(pos1, w).astype(F32)
        pooled = []
        for k in range(tiles_per_group):
            tile = g * tiles_per_group + k
            cols = slice(tile * LANES, (tile + 1) * LANES)
            ext_d[tile, POOL_HALO:, :] = xd_ref[:, cols]
            lo = SUBLANES
            win = ext_d[tile, lo:n_ext, :] + ext_d[tile, lo - 1:n_ext - 1, :]
            if w >= 4:
                lvl2[k, lo:n_ext, :] = win
                lo += SUBLANES
                win = lvl2[k, lo:n_ext, :] + lvl2[k, lo - 2:n_ext - 2, :]
            if w >= 8:
                lvl4[k, lo:n_ext, :] = win
                lo += SUBLANES
                win = lvl4[k, lo:n_ext, :] + lvl4[k, lo - 4:n_ext - 4, :]
            if w >= 16:
                lo += SUBLANES
                win = win[SUBLANES:] + win[:-SUBLANES]
            pooled.append((win[POOL_HALO - lo:] * inv - xd_ref[:, cols]).astype(BF16))
        cols = slice(g * D_GROUP_DIM, (g + 1) * D_GROUP_DIM)
        mixed = jnp.dot(jnp.concatenate(pooled, axis=1), wpool_bf[g], preferred_element_type=F32)
        out_cols = slice(C_WIDTH + g * D_GROUP_DIM, C_WIDTH + (g + 1) * D_GROUP_DIM)
        mix_ref[:, out_cols] = (mixed * dscale_ref[row, cols] * _silu(dg_ref[:, cols])).astype(BF16)

    tail.finish()


def _odd_layer(h, x, conv_w, conv_b, w_a, w_x, b_a, b_x, lam, w_pool, d_scale, w_out_bf, ln_g, ln_b,
               layer, seq):
    m = h.shape[0]
    t = MIX_T
    n_tiles = m // t
    n_s = seq // t
    cur, res, out = _lagged_tiles(n_tiles)
    n_pool = len(POOL_WINDOWS)
    tail_in, out_specs, out_shape, tail_scratch, tail_args = _tail_specs(x, w_out_bf, ln_g, ln_b, res, out)
    return pl.pallas_call(
        partial(_odd_layer_kernel, layer=layer, n_tiles=n_tiles, n_s=n_s),
        grid=(n_tiles + 2,),
        in_specs=[
            pl.BlockSpec((t, h.shape[1]), lambda i: (cur(i), 0)),
            _layer_block((CONV_WIDTH, C_WIDTH), layer),
            _whole(conv_b),
            _layer_block((C_HEADS, C_HEAD_DIM, C_HEAD_DIM), layer),
            _layer_block((C_HEADS, C_HEAD_DIM, C_HEAD_DIM), layer),
            _whole(b_a), _whole(b_x), _whole(lam),
            _layer_block((n_pool, D_GROUP_DIM, D_GROUP_DIM), layer),
            _whole(d_scale),
        ] + tail_in,
        out_specs=out_specs,
        out_shape=out_shape,
        scratch_shapes=tail_scratch + [
            pltpu.VMEM((C_WIDTH // LANES, CONV_HALO + t, LANES), F32),
            pltpu.VMEM((D_WIDTH // LANES, POOL_HALO + t, LANES), F32),
            pltpu.VMEM((D_GROUP_DIM // LANES, POOL_HALO + t, LANES), F32),
            pltpu.VMEM((D_GROUP_DIM // LANES, POOL_HALO + t, LANES), F32),
            pltpu.VMEM((SUBLANES, LANES), F32),
            pltpu.VMEM((t // SUBLANES * C_HEADS * SCAN_PITCH, LANES), F32),
            pltpu.VMEM((t // SUBLANES * C_HEADS * SCAN_PITCH, LANES), F32),
            pltpu.VMEM((t // SUBLANES * C_HEADS * SCAN_PITCH, LANES), F32),
            pltpu.VMEM((C_HEADS, C_HEAD_DIM, 2 * C_HEAD_DIM), BF16),
            pltpu.VMEM((n_pool, D_GROUP_DIM, D_GROUP_DIM), BF16),
        ],
        compiler_params=_compiler_params("arbitrary"),
        name="odd_layer",
    )(h, conv_w, conv_b, w_a, w_x, b_a, b_x, lam, w_pool, d_scale, *tail_args)


def _rope_coefficients():
    half = ROT_DIM // 2
    inv_freq = ROPE_THETA ** (-jnp.arange(0, ROT_DIM, 2, dtype=F32) / ROT_DIM)
    rest = jnp.zeros((B_HEAD_DIM - ROT_DIM,), F32)
    freq_head = jnp.concatenate([inv_freq, inv_freq, rest])
    sign_head = jnp.concatenate([-jnp.ones((half,), F32), jnp.ones((half,), F32), rest])
    reps = LANES // B_HEAD_DIM
    return jnp.stack([jnp.tile(freq_head, reps), jnp.tile(sign_head, reps)])


def kernel(x, positions, even_w_in, even_a_ln_g, even_a_ln_b, even_a_ws, even_a_bs, even_b_sinks, even_w_out, even_ln_g, even_ln_b, odd_w_in, odd_conv_w, odd_conv_b, odd_w_a, odd_b_a, odd_w_x, odd_b_x, odd_lam, odd_w_pool, odd_d_scale, odd_w_out, odd_ln_g, odd_ln_b):
    batch, seq, d = x.shape
    m = batch * seq
    xf = x.reshape(m, d)
    rope_tab = None
    even_bst = jnp.swapaxes(even_a_bs, 1, 2)
    w_in, w_layer = even_w_in, 0
    for layer in range(DEPTH):
        j = layer // 2
        if layer % 2 == 0:
            h, w_out, w_in, tab = _proj_in(xf, w_in, w_layer, even_w_out, j, odd_w_in, j,
                                           positions if rope_tab is None else None)
            rope_tab = tab if rope_tab is None else rope_tab
            xf = _even_layer(h, xf, even_b_sinks, rope_tab, even_a_ln_g, even_a_ln_b, even_a_ws,
                             even_bst, w_out, even_ln_g, even_ln_b, j, seq)
        else:
            w_next = even_w_in if layer + 1 < DEPTH else None
            h, w_out, w_in, _ = _proj_in(xf, w_in, w_layer, odd_w_out, j, w_next, j + 1)
            xf = _odd_layer(h, xf, odd_conv_w, odd_conv_b, odd_w_a, odd_w_x, odd_b_a, odd_b_x, odd_lam,
                            odd_w_pool, odd_d_scale, w_out, odd_ln_g, odd_ln_b, j, seq)
        w_layer = None
    return xf.reshape(batch, seq, d)
```

```python
from functools import partial

import jax
import jax.numpy as jnp
from jax import lax
from jax.experimental import pallas as pl
from jax.experimental.pallas import tpu as pltpu

D_MODEL = 2048
DEPTH = 4
A_WIDTH = 1024
A_GROUPS = 8
CHUNK = 128
B_HEAD_DIM = 64
B_Q_HEADS = 16
B_KV_HEADS = 2
B_WIDTH = 1024
B_KV_WIDTH = 128
WINDOW = 128
ROT_DIM = 16
ROPE_THETA = 500000.0
C_WIDTH = 1024
C_HEADS = 8
C_HEAD_DIM = 128
CONV_WIDTH = 4
LRU_C = 8.0
D_WIDTH = 1024
POOL_WINDOWS = (2, 4, 8, 16)
D_GROUP_DIM = 256
MIX_WIDTH = 2048
DN_ALPHA = (2 * DEPTH) ** 0.25
LN_EPS = 1e-5

LANES = 128
SUBLANES = 8
VMEM_LIMIT_BYTES = 56 * 1024 * 1024

PROJ_TM = 256
MIX_T = 256
OUT_BLOCK = 256
OUT_BLOCKS = D_MODEL // OUT_BLOCK
W_CHUNK = 256
CONV_HALO = SUBLANES
POOL_HALO = 4 * SUBLANES
SCAN_PITCH = SUBLANES + 1
ROW_PAD = LANES

KV_BLOCK = 2 * B_KV_WIDTH
KV_COL = (3 * A_WIDTH + B_WIDTH) // KV_BLOCK
GATE_B_COL = KV_COL + 1

F32 = jnp.float32
BF16 = jnp.bfloat16


def _sigmoid(x):
    return 0.5 * jnp.tanh(0.5 * x) + 0.5


def _silu(x):
    hx = 0.5 * x
    return hx * jnp.tanh(hx) + hx


def _resident(shape):
    return pl.BlockSpec(shape, lambda *_: (0,) * len(shape), pipeline_mode=pl.Buffered(1))


def _whole(arr):
    return pl.BlockSpec(arr.shape, lambda *_: (0,) * arr.ndim)


def _layer_block(shape, layer):
    return pl.BlockSpec((None,) + shape, lambda *_: (layer,) + (0,) * len(shape))


def _compiler_params(*semantics):
    return pltpu.CompilerParams(dimension_semantics=semantics, vmem_limit_bytes=VMEM_LIMIT_BYTES)


def _proj_in_kernel(*refs, cast_next, w_layer, rope):
    n_in = 3 + cast_next + 2 * rope
    n_out = 2 + cast_next + rope
    x_ref, w_hbm, wout_ref = refs[:3]
    opad_ref, wout_bf_ref = refs[n_in:n_in + 2]
    staged = w_hbm.dtype != BF16
    w_vmem, *stage, sems = refs[n_in + n_out:]
    n = w_vmem.shape[1]
    o_ref = opad_ref.at[:, :n]

    def side_jobs():
        opad_ref[:, n:] = jnp.zeros((opad_ref.shape[0], ROW_PAD), F32)
        if cast_next:
            refs[n_in + 2][...] = refs[3][...].astype(BF16)
        for p in range(OUT_BLOCKS):
            wout_bf_ref[p] = wout_ref[:, p * OUT_BLOCK:(p + 1) * OUT_BLOCK].astype(BF16)
        if rope:
            pos_ref, coef_ref = refs[n_in - 2:n_in]
            tab_ref = refs[n_in + n_out - 1]
            ang = pos_ref[...].astype(F32) * coef_ref[0:1, :]
            tab_ref[:, :LANES] = jnp.cos(ang)
            tab_ref[:, LANES:] = jnp.sin(ang) * coef_ref[1:2, :]

    n_chunks = w_vmem.shape[0] // W_CHUNK
    w_src = w_hbm if w_layer is None else w_hbm.at[w_layer]

    def chunk_copy(c):
        rows = pl.ds(c * W_CHUNK, W_CHUNK)
        if staged:
            return pltpu.make_async_copy(w_src.at[rows, :], stage[0].at[c % 2], sems.at[c % 2])
        return pltpu.make_async_copy(w_src.at[rows, :], w_vmem.at[rows, :], sems.at[c])

    step = pl.program_id(0)

    @pl.when(step == 0)
    def _():
        xb = x_ref[...].astype(BF16)
        for c in range(2 if staged else n_chunks):
            chunk_copy(c).start()
        side_jobs()
        for c in range(n_chunks):
            rows = slice(c * W_CHUNK, (c + 1) * W_CHUNK)
            chunk_copy(c).wait()
            if staged:
                w_vmem[rows, :] = stage[0][c % 2].astype(BF16)
                if c + 2 < n_chunks:
                    chunk_copy(c + 2).start()
            part = jnp.dot(xb[:, rows], w_vmem[rows, :], preferred_element_type=F32)
            if c == 0:
                o_ref[...] = part
            else:
                o_ref[...] += part

    @pl.when(step > 0)
    def _():
        side_jobs()
        o_ref[...] = jnp.dot(x_ref[...].astype(BF16), w_vmem[...], preferred_element_type=F32)


def _proj_in(x, w, w_layer, w_out, layer, w_next, layer_next, positions=None):
    m, k = x.shape
    n = w.shape[-1]
    steps = m // PROJ_TM

    def slab_specs(stacked, which):
        rows, cols = stacked.shape[1:]
        slab = rows // steps
        return (pl.BlockSpec((None, slab, cols), lambda i: (which, i, 0)),
                pl.BlockSpec((slab, cols), lambda i: (i, 0)),
                jax.ShapeDtypeStruct((rows, cols), BF16))

    k_out = w_out.shape[1]
    slab_out = k_out // steps
    casts = [(pl.BlockSpec((None, slab_out, D_MODEL), lambda i: (layer, i, 0)),
              pl.BlockSpec((OUT_BLOCKS, slab_out, OUT_BLOCK), lambda i: (0, i, 0)),
              jax.ShapeDtypeStruct((OUT_BLOCKS, k_out, OUT_BLOCK), BF16))]
    args = [x, w, w_out]
    if w_next is not None:
        casts.append(slab_specs(w_next, layer_next))
        args.append(w_next)
    rope = positions is not None
    if rope:
        casts.append((pl.BlockSpec((PROJ_TM, 1), lambda i: (i, 0)), None, None))
        coef = _rope_coefficients()
        casts.append((_whole(coef), pl.BlockSpec((PROJ_TM, 2 * LANES), lambda i: (i, 0)),
                      jax.ShapeDtypeStruct((m, 2 * LANES), F32)))
        args += [positions.reshape(m, 1), coef]
    staged = w.dtype != BF16
    scratch = [pltpu.VMEM((k, n), BF16)]
    if staged:
        scratch.append(pltpu.VMEM((2, W_CHUNK, n), w.dtype))
    scratch.append(pltpu.SemaphoreType.DMA((2 if staged else k // W_CHUNK,)))
    outs = pl.pallas_call(
        partial(_proj_in_kernel, cast_next=w_next is not None, w_layer=w_layer, rope=rope),
        grid=(steps,),
        in_specs=[pl.BlockSpec((PROJ_TM, k), lambda i: (i, 0)), pl.BlockSpec(memory_space=pl.ANY)]
        + [c[0] for c in casts],
        out_specs=[pl.BlockSpec((PROJ_TM, n + ROW_PAD), lambda i: (i, 0))] + [c[1] for c in casts if c[1]],
        out_shape=[jax.ShapeDtypeStruct((m, n + ROW_PAD), F32)] + [c[2] for c in casts if c[2]],
        scratch_shapes=scratch,
        compiler_params=_compiler_params("arbitrary"),
        name="proj_in",
    )(*args)
    h, w_out_bf, *rest = outs
    return h, w_out_bf, rest.pop(0) if w_next is not None else None, rest.pop(0) if rope else None


class _Tail:
    def __init__(self, mix_prev, wout_ref, x_ref, z_ref, z_prev, g_ref, b_ref, layer, o_ref,
                 proj, ln, load_once=False):
        self.mix_ref = mix_prev.at[:, :MIX_WIDTH]
        self.mix = self.mix_ref[...] if proj and load_once else None
        self.proj_refs = (wout_ref, x_ref, z_ref)
        self.ln_refs = (z_prev, g_ref, b_ref, layer, o_ref)
        self.blocks_left = list(range(OUT_BLOCKS)) if proj else []
        rows = z_ref.shape[0]
        self.halves_left = [slice(r * rows // 2, (r + 1) * rows // 2) for r in range(2)] if ln else []

    def emit(self, count=1):
        wout_ref, x_ref, z_ref = self.proj_refs
        for _ in range(min(count, len(self.blocks_left))):
            blk = self.blocks_left.pop(0)
            cols = slice(blk * OUT_BLOCK, (blk + 1) * OUT_BLOCK)
            lhs = self.mix_ref[...] if self.mix is None else self.mix
            z_ref[:, cols] = DN_ALPHA * x_ref[:, cols] + jnp.dot(
                lhs, wout_ref[blk], preferred_element_type=F32)

    def ln_half(self):
        if self.halves_left:
            z_prev, g_ref, b_ref, layer, o_ref = self.ln_refs
            rows = self.halves_left.pop(0)
            z = z_prev[rows, :D_MODEL]
            mu = jnp.mean(z, axis=-1, keepdims=True)
            zc = z - mu
            var = jnp.mean(zc * zc, axis=-1, keepdims=True)
            o_ref[rows, :] = (zc * lax.rsqrt(var + LN_EPS) * g_ref[layer:layer + 1, :]
                              + b_ref[layer:layer + 1, :])

    def finish(self):
        self.ln_half()
        self.ln_half()
        self.emit(OUT_BLOCKS)


def _lagged_tiles(n_tiles):
    def clamp(i):
        return jnp.clip(i, 0, n_tiles - 1)

    return clamp, (lambda i: clamp(i - 1)), (lambda i: clamp(i - 2))


def _run_stages(step, n_tiles, body, wout_hbm, wout_vmem, wout_sem, mix_a, mix_b, z_a, z_b):
    wout_copy = pltpu.make_async_copy(wout_hbm, wout_vmem, wout_sem.at[0])

    @pl.when(step == 0)
    def _():
        wout_copy.start()

    @pl.when(step == 1)
    def _():
        wout_copy.wait()

    def stage(pred, parity, **stages):
        bufs = (mix_a, mix_b, z_a, z_b) if parity == 0 else (mix_b, mix_a, z_b, z_a)

        @pl.when(pred)
        def _():
            body(*bufs, **stages)

    steady = (step >= 2) & (step < n_tiles)
    stage(step == 0, 0, mix=True, proj=False, ln=False)
    stage(step == 1, 1, mix=True, proj=True, ln=False)
    stage(steady & (step % 2 == 0), 0, mix=True, proj=True, ln=True)
    stage(steady & (step % 2 == 1), 1, mix=True, proj=True, ln=True)
    stage(step == n_tiles, n_tiles % 2, mix=False, proj=True, ln=True)
    stage(step == n_tiles + 1, (n_tiles + 1) % 2, mix=False, proj=False, ln=True)


def _tail_specs(x, w_out_bf, ln_g, ln_b, res, out):
    m, d = x.shape
    in_specs = [
        pl.BlockSpec((MIX_T, d), lambda i: (res(i), 0)),
        pl.BlockSpec(memory_space=pl.ANY),
        _whole(ln_g),
        _whole(ln_b),
    ]
    out_spec = pl.BlockSpec((MIX_T, d), lambda i: (out(i), 0))
    out_shape = jax.ShapeDtypeStruct((m, d), F32)
    scratch = [pltpu.VMEM((MIX_T, MIX_WIDTH + ROW_PAD), BF16), pltpu.VMEM((MIX_T, MIX_WIDTH + ROW_PAD), BF16),
               pltpu.VMEM((MIX_T, d + ROW_PAD), F32), pltpu.VMEM((MIX_T, d + ROW_PAD), F32),
               pltpu.VMEM(w_out_bf.shape, BF16), pltpu.SemaphoreType.DMA((1,))]
    return in_specs, out_spec, out_shape, scratch, [x, w_out_bf, ln_g, ln_b]


def _rope_tile(t, cos, sin, take_upper):
    upper = pltpu.roll(t, LANES - ROT_DIM // 2, 1)
    lower = pltpu.roll(t, ROT_DIM // 2, 1)
    return t * cos + jnp.where(take_upper, upper, lower) * sin


EVEN_WOUT_ARG = 10
ODD_WOUT_ARG = 11


def _even_layer_kernel(*refs, layer, n_tiles, n_s):
    *io_refs, mix_a, mix_b, z_a, z_b, wout_vmem, wout_sem = refs
    wout_hbm = io_refs[EVEN_WOUT_ARG]
    io_refs[EVEN_WOUT_ARG] = wout_vmem
    step = pl.program_id(0)
    seq_tile = jnp.minimum(step, n_tiles - 1) % n_s
    _run_stages(step, n_tiles, partial(_even_step, *io_refs, layer=layer, seq_tile=seq_tile),
                wout_hbm, wout_vmem, wout_sem, mix_a, mix_b, z_a, z_b)


def _even_step(sinks_ref, h_ref, kvp_ref, tabc_ref, tabp_ref, lng_ref, lnb_ref, ws_ref, bst_ref,
               x_ref, wout_ref, g_ref, b_ref, o_ref, mix_ref, mix_prev, z_ref, z_prev,
               *, layer, seq_tile, mix, proj, ln):
    u_ref, v_ref, ag_ref, q_ref = (h_ref.at[:, c * A_WIDTH:(c + 1) * A_WIDTH] for c in range(4))
    kvc_ref = h_ref.at[:, KV_COL * KV_BLOCK:(KV_COL + 1) * KV_BLOCK]
    bg_refs = [h_ref.at[:, (GATE_B_COL + c) * KV_BLOCK:(GATE_B_COL + c + 1) * KV_BLOCK]
               for c in range(B_WIDTH // KV_BLOCK)]
    cosc_ref, sinc_ref = tabc_ref.at[:, :LANES], tabc_ref.at[:, LANES:]
    cosp_ref, sinp_ref = tabp_ref.at[:, :LANES], tabp_ref.at[:, LANES:]
    t_rows = h_ref.shape[0]
    n_chunks = t_rows // CHUNK
    tail = _Tail(mix_prev, wout_ref, x_ref, z_ref, z_prev, g_ref, b_ref, layer, o_ref, proj, ln,
                 load_once=True)
    if not mix:
        tail.finish()
        return
    tail.emit()

    v = v_ref[...]
    mu = jnp.mean(v, axis=-1, keepdims=True)
    vc = v - mu
    var = jnp.mean(vc * vc, axis=-1, keepdims=True)
    vln = (vc * lax.rsqrt(var + LN_EPS) * lng_ref[layer:layer + 1, :]
           + lnb_ref[layer:layer + 1, :]).astype(BF16)
    causal = (lax.broadcasted_iota(jnp.int32, (CHUNK, CHUNK), 0)
              >= lax.broadcasted_iota(jnp.int32, (CHUNK, CHUNK), 1))
    lane = lax.broadcasted_iota(jnp.int32, (1, LANES), 1)
    take_upper = (lane % B_HEAD_DIM) < (ROT_DIM // 2)
    low_half = lane < B_HEAD_DIM
    cosc = cosc_ref[...]
    sinc = sinc_ref[...]
    scale = B_HEAD_DIM ** -0.5
    q_tiles = []
    for g in range(A_GROUPS):
        cols = slice(g * CHUNK, (g + 1) * CHUNK)
        w = jnp.where(causal, ws_ref[g], 0.0).astype(BF16)
        rhs = jnp.concatenate(
            [vln[c * CHUNK:(c + 1) * CHUNK, cols] for c in range(n_chunks)], axis=1)
        mixed = jnp.dot(w, rhs, preferred_element_type=F32) + bst_ref[:, g:g + 1]
        for c in range(n_chunks):
            rows = slice(c * CHUNK, (c + 1) * CHUNK)
            gate = _silu(ag_ref[rows, cols])
            mix_ref[rows, cols] = (u_ref[rows, cols] * mixed[:, c * CHUNK:(c + 1) * CHUNK]
                                   * gate).astype(BF16)
        q_tiles.append(
            (_rope_tile(q_ref[:, cols], cosc, sinc, take_upper) * scale).astype(BF16))
        if g in (1, 5):
            tail.ln_half()
        if g in (2, 4, 6):
            tail.emit()

    k_all = jnp.concatenate(
        [_rope_tile(kvp_ref[:, :B_KV_WIDTH], cosp_ref[...], sinp_ref[...], take_upper),
         _rope_tile(kvc_ref[:, :B_KV_WIDTH], cosc, sinc, take_upper)], axis=0)
    k_swap = pltpu.roll(k_all, B_HEAD_DIM, 1)
    zero = jnp.zeros_like(k_all)
    k_lo = (jnp.where(low_half, k_all, zero).astype(BF16), jnp.where(low_half, k_swap, zero).astype(BF16))
    k_hi = (jnp.where(low_half, zero, k_swap).astype(BF16), jnp.where(low_half, zero, k_all).astype(BF16))
    v_all = jnp.concatenate([kvp_ref[:, B_KV_WIDTH:], kvc_ref[:, B_KV_WIDTH:]], axis=0)
    vt_all = v_all.T.astype(BF16)

    tiles_per_kv = B_WIDTH // LANES // B_KV_HEADS
    band_keys = 2 * WINDOW
    kj = lax.broadcasted_iota(jnp.int32, (band_keys, WINDOW), 0)
    qi = lax.broadcasted_iota(jnp.int32, (band_keys, WINDOW), 1)
    diff = qi + WINDOW - kj
    band = (diff >= 0) & (diff < WINDOW)
    first_key = jnp.where(seq_tile == 0, WINDOW, 0)
    neg_inf = jnp.full((band_keys, WINDOW), -jnp.inf, F32)
    mask_any = jnp.where(band, 0.0, neg_inf)
    mask_first = jnp.where(band & (kj >= first_key), 0.0, neg_inf)

    for n in range(t_rows // WINDOW):
        rows = slice(n * WINDOW, (n + 1) * WINDOW)
        band_rows = slice(n * WINDOW, (n + 2) * WINDOW)
        mask = jnp.concatenate([mask_first if n == 0 else mask_any] * tiles_per_kv, axis=1)
        for kv in range(B_KV_HEADS):
            tail.emit()
            q4 = jnp.concatenate(
                [q_tiles[kv * tiles_per_kv + j][rows] for j in range(tiles_per_kv)], axis=0)
            k_cat = jnp.concatenate([k_lo[kv][band_rows], k_hi[kv][band_rows]], axis=0)
            st = lax.dot_general(k_cat, q4, (((1,), (1,)), ((), ())), preferred_element_type=F32)
            vt = vt_all[kv * B_HEAD_DIM:(kv + 1) * B_HEAD_DIM, band_rows]
            o_halves = []
            for half in range(2):
                sh = st[half * band_keys:(half + 1) * band_keys] + mask
                sink = jnp.concatenate(
                    [jnp.full((1, WINDOW), sinks_ref[layer, 2 * (kv * tiles_per_kv + j) + half], F32)
                     for j in range(tiles_per_kv)], axis=1)
                m = jnp.maximum(jnp.max(sh, axis=0, keepdims=True), sink)
                p = jnp.exp(sh - m)
                l = jnp.sum(p, axis=0, keepdims=True) + jnp.exp(sink - m)
                o_half = jnp.dot(vt, p.astype(BF16), preferred_element_type=F32)
                o_halves.append(o_half * (1.0 / l))
            ot = jnp.concatenate(o_halves, axis=0)
            for j in range(tiles_per_kv):
                jj = kv * tiles_per_kv + j
                gate_ref = bg_refs[jj // 2]
                gate_cols = slice((jj % 2) * LANES, (jj % 2 + 1) * LANES)
                out_cols = slice(A_WIDTH + jj * LANES, A_WIDTH + (jj + 1) * LANES)
                o_tile = ot[:, j * WINDOW:(j + 1) * WINDOW].T
                mix_ref[rows, out_cols] = (o_tile * _silu(gate_ref[rows, gate_cols])).astype(BF16)

    tail.finish()


def _even_layer(h, x, sinks, rope_tab, a_ln_g, a_ln_b, ws, bst, w_out_bf, ln_g, ln_b, layer, seq):
    m = h.shape[0]
    t = MIX_T
    n_tiles = m // t
    n_s = seq // t
    blocks_per_tile = t // WINDOW
    cur, res, out = _lagged_tiles(n_tiles)

    def prev(i):
        return jnp.maximum(cur(i) * blocks_per_tile - 1, 0)

    tail_in, out_specs, out_shape, tail_scratch, tail_args = _tail_specs(x, w_out_bf, ln_g, ln_b, res, out)
    return pl.pallas_call(
        partial(_even_layer_kernel, layer=layer, n_tiles=n_tiles, n_s=n_s),
        grid=(n_tiles + 2,),
        in_specs=[
            pl.BlockSpec(memory_space=pltpu.SMEM),
            pl.BlockSpec((t, h.shape[1]), lambda i: (cur(i), 0)),
            pl.BlockSpec((WINDOW, KV_BLOCK), lambda i: (prev(i), KV_COL)),
            pl.BlockSpec((t, 2 * LANES), lambda i: (cur(i), 0)),
            pl.BlockSpec((WINDOW, 2 * LANES), lambda i: (prev(i), 0)),
            _whole(a_ln_g),
            _whole(a_ln_b),
            _layer_block((A_GROUPS, CHUNK, CHUNK), layer),
            _layer_block((CHUNK, A_GROUPS), layer),
        ] + tail_in,
        out_specs=out_specs,
        out_shape=out_shape,
        scratch_shapes=tail_scratch,
        compiler_params=_compiler_params("arbitrary"),
        name="even_layer",
    )(sinks, h, h, rope_tab, rope_tab, a_ln_g, a_ln_b, ws, bst, *tail_args)


def _odd_layer_kernel(*refs, layer, n_tiles, n_s):
    *io_refs, mix_a, mix_b, z_a, z_b, wout_vmem, wout_sem = refs[:21]
    (ext_c, ext_d, lvl2, lvl4, hstate, sa, sb, sh, wax_bf, wpool_bf) = scratch = refs[21:]
    wout_hbm = io_refs[ODD_WOUT_ARG]
    io_refs[ODD_WOUT_ARG] = wout_vmem
    wa_ref, wx_ref, wpool_ref = io_refs[3], io_refs[4], io_refs[8]
    t_rows = io_refs[0].shape[0]
    step = pl.program_id(0)
    s_idx = jnp.minimum(step, n_tiles - 1) % n_s

    @pl.when(s_idx == 0)
    def _():
        ext_c[:, 0:CONV_HALO, :] = jnp.zeros((C_HEADS, CONV_HALO, LANES), F32)
        ext_d[:, 0:POOL_HALO, :] = jnp.zeros((D_WIDTH // LANES, POOL_HALO, LANES), F32)
        hstate[...] = jnp.zeros_like(hstate)
        wax_bf[:, :, :C_HEAD_DIM] = wa_ref[...].astype(BF16)
        wax_bf[:, :, C_HEAD_DIM:] = wx_ref[...].astype(BF16)
        wpool_bf[...] = wpool_ref[...].astype(BF16)

    @pl.when(s_idx > 0)
    def _():
        ext_c[:, 0:CONV_HALO, :] = ext_c[:, t_rows:t_rows + CONV_HALO, :]
        ext_d[:, 0:POOL_HALO, :] = ext_d[:, t_rows:t_rows + POOL_HALO, :]

    _run_stages(step, n_tiles, partial(_odd_step, *io_refs, *scratch, layer=layer, s_idx=s_idx),
                wout_hbm, wout_vmem, wout_sem, mix_a, mix_b, z_a, z_b)


def _odd_step(h_ref, convw_ref, convb_ref, wa_ref, wx_ref, ba_ref, bx_ref,
              lam_ref, wpool_ref, dscale_ref, x_ref, wout_ref, g_ref, b_ref, o_ref,
              ext_c, ext_d, lvl2, lvl4, hstate, sa, sb, sh, wax_bf, wpool_bf,
              mix_ref, mix_prev, z_ref, z_prev, *, layer, s_idx, mix, proj, ln):
    xc_ref, cg_ref, xd_ref, dg_ref = (h_ref.at[:, c * C_WIDTH:(c + 1) * C_WIDTH] for c in range(4))
    t_rows = h_ref.shape[0]
    row = slice(layer, layer + 1)
    n_groups = t_rows // SUBLANES
    tail = _Tail(mix_prev, wout_ref, x_ref, z_ref, z_prev, g_ref, b_ref, layer, o_ref, proj, ln)
    if not mix:
        tail.finish()
        return
    tail.emit()

    z = -lam_ref[row, :]
    softplus = jnp.maximum(z, 0.0) + jnp.log1p(jnp.exp(-jnp.abs(z)))
    for hd in range(C_HEADS):
        cols = slice(hd * C_HEAD_DIM, (hd + 1) * C_HEAD_DIM)
        ext_c[hd, CONV_HALO:, :] = xc_ref[:, cols]
        xconv = convb_ref[row, cols]
        for j in range(CONV_WIDTH):
            lo = CONV_HALO - (CONV_WIDTH - 1) + j
            xconv = xconv + ext_c[hd, lo:lo + t_rows, :] * convw_ref[j:j + 1, cols]
        ri = jnp.dot(xconv.astype(BF16), wax_bf[hd], preferred_element_type=F32)
        r = _sigmoid(ri[:, :C_HEAD_DIM] + ba_ref[row, cols])
        i = _sigmoid(ri[:, C_HEAD_DIM:] + bx_ref[row, cols])
        log_a = -LRU_C * r * softplus[:, cols]
        a = jnp.exp(log_a)
        m2 = -jnp.tanh(log_a) * (a * a + 1.0)
        mult = jnp.where(m2 == 0.0, 0.0, m2 * lax.rsqrt(m2))
        b = mult * (i * xconv)
        for g in range(n_groups):
            base = (g * C_HEADS + hd) * SCAN_PITCH
            sa[base:base + SUBLANES, :] = a[g * SUBLANES:(g + 1) * SUBLANES]
            sb[base:base + SUBLANES, :] = b[g * SUBLANES:(g + 1) * SUBLANES]
        if hd in (2, 5):
            tail.ln_half()
        if hd in (1, 3, 4, 6, 7):
            tail.emit()

    h = hstate[...]
    for t in range(t_rows):
        g, s = divmod(t, SUBLANES)
        at_t = pl.ds(g * C_HEADS * SCAN_PITCH + s, C_HEADS, stride=SCAN_PITCH)
        h = sa[at_t, :] * h + sb[at_t, :]
        sh[at_t, :] = h
    hstate[...] = h
    tail.emit()
    for hd in range(C_HEADS):
        cols = slice(hd * C_HEAD_DIM, (hd + 1) * C_HEAD_DIM)
        hcol = jnp.concatenate(
            [sh[(g * C_HEADS + hd) * SCAN_PITCH:(g * C_HEADS + hd) * SCAN_PITCH + SUBLANES, :]
             for g in range(n_groups)], axis=0)
        mix_ref[:, cols] = (hcol * _silu(cg_ref[:, cols])).astype(BF16)

    n_ext = POOL_HALO + t_rows
    tiles_per_group = D_GROUP_DIM // LANES
    pos1 = s_idx * t_rows + lax.broadcasted_iota(jnp.int32, (t_rows, LANES), 0) + 1
    for g, w in enumerate(POOL_WINDOWS):
        if g == 2:
            tail.emit()
        inv = 1.0 / jnp.minimum(pos1, w).astype(F32)
        pooled = []
        for k in range(tiles_per_group):
            tile = g * tiles_per_group + k
            cols = slice(tile * LANES, (tile + 1) * LANES)
            ext_d[tile, POOL_HALO:, :] = xd_ref[:, cols]
            lo = SUBLANES
            win = ext_d[tile, lo:n_ext, :] + ext_d[tile, lo - 1:n_ext - 1, :]
            if w >= 4:
                lvl2[k, lo:n_ext, :] = win
                lo += SUBLANES
                win = lvl2[k, lo:n_ext, :] + lvl2[k, lo - 2:n_ext - 2, :]
            if w >= 8:
                lvl4[k, lo:n_ext, :] = win
                lo += SUBLANES
                win = lvl4[k, lo:n_ext, :] + lvl4[k, lo - 4:n_ext - 4, :]
            if w >= 16:
                lo += SUBLANES
                win = win[SUBLANES:] + win[:-SUBLANES]
            pooled.append((win[POOL_HALO - lo:] * inv - xd_ref[:, cols]).astype(BF16))
        cols = slice(g * D_GROUP_DIM, (g + 1) * D_GROUP_DIM)
        mixed = jnp.dot(jnp.concatenate(pooled, axis=1), wpool_bf[g], preferred_element_type=F32)
        out_cols = slice(C_WIDTH + g * D_GROUP_DIM, C_WIDTH + (g + 1) * D_GROUP_DIM)
        mix_ref[:, out_cols] = (mixed * dscale_ref[row, cols] * _silu(dg_ref[:, cols])).astype(BF16)

    tail.finish()


def _odd_layer(h, x, conv_w, conv_b, w_a, w_x, b_a, b_x, lam, w_pool, d_scale, w_out_bf, ln_g, ln_b,
               layer, seq):
    m = h.shape[0]
    t = MIX_T
    n_tiles = m // t
    n_s = seq // t
    cur, res, out = _lagged_tiles(n_tiles)
    n_pool = len(POOL_WINDOWS)
    tail_in, out_specs, out_shape, tail_scratch, tail_args = _tail_specs(x, w_out_bf, ln_g, ln_b, res, out)
    return pl.pallas_call(
        partial(_odd_layer_kernel, layer=layer, n_tiles=n_tiles, n_s=n_s),
        grid=(n_tiles + 2,),
        in_specs=[
            pl.BlockSpec((t, h.shape[1]), lambda i: (cur(i), 0)),
            _layer_block((CONV_WIDTH, C_WIDTH), layer),
            _whole(conv_b),
            _layer_block((C_HEADS, C_HEAD_DIM, C_HEAD_DIM), layer),
            _layer_block((C_HEADS, C_HEAD_DIM, C_HEAD_DIM), layer),
            _whole(b_a), _whole(b_x), _whole(lam),
            _layer_block((n_pool, D_GROUP_DIM, D_GROUP_DIM), layer),
            _whole(d_scale),
        ] + tail_in,
        out_specs=out_specs,
        out_shape=out_shape,
        scratch_shapes=tail_scratch + [
            pltpu.VMEM((C_WIDTH // LANES, CONV_HALO + t, LANES), F32),
            pltpu.VMEM((D_WIDTH // LANES, POOL_HALO + t, LANES), F32),
            pltpu.VMEM((D_GROUP_DIM // LANES, POOL_HALO + t, LANES), F32),
            pltpu.VMEM((D_GROUP_DIM // LANES, POOL_HALO + t, LANES), F32),
            pltpu.VMEM((SUBLANES, LANES), F32),
            pltpu.VMEM((t // SUBLANES * C_HEADS * SCAN_PITCH, LANES), F32),
            pltpu.VMEM((t // SUBLANES * C_HEADS * SCAN_PITCH, LANES), F32),
            pltpu.VMEM((t // SUBLANES * C_HEADS * SCAN_PITCH, LANES), F32),
            pltpu.VMEM((C_HEADS, C_HEAD_DIM, 2 * C_HEAD_DIM), BF16),
            pltpu.VMEM((n_pool, D_GROUP_DIM, D_GROUP_DIM), BF16),
        ],
        compiler_params=_compiler_params("arbitrary"),
        name="odd_layer",
    )(h, conv_w, conv_b, w_a, w_x, b_a, b_x, lam, w_pool, d_scale, *tail_args)


def _rope_coefficients():
    half = ROT_DIM // 2
    inv_freq = ROPE_THETA ** (-jnp.arange(0, ROT_DIM, 2, dtype=F32) / ROT_DIM)
    rest = jnp.zeros((B_HEAD_DIM - ROT_DIM,), F32)
    freq_head = jnp.concatenate([inv_freq, inv_freq, rest])
    sign_head = jnp.concatenate([-jnp.ones((half,), F32), jnp.ones((half,), F32), rest])
    reps = LANES // B_HEAD_DIM
    return jnp.stack([jnp.tile(freq_head, reps), jnp.tile(sign_head, reps)])


def kernel(x, positions, even_w_in, even_a_ln_g, even_a_ln_b, even_a_ws, even_a_bs, even_b_sinks, even_w_out, even_ln_g, even_ln_b, odd_w_in, odd_conv_w, odd_conv_b, odd_w_a, odd_b_a, odd_w_x, odd_b_x, odd_lam, odd_w_pool, odd_d_scale, odd_w_out, odd_ln_g, odd_ln_b):
    batch, seq, d = x.shape
    m = batch * seq
    xf = x.reshape(m, d)
    rope_tab = None
    even_bst = jnp.swapaxes(even_a_bs, 1, 2)
    w_in, w_layer = even_w_in, 0
    for layer in range(DEPTH):
        j = layer // 2
        if layer % 2 == 0:
            h, w_out, w_in, tab = _proj_in(xf, w_in, w_layer, even_w_out, j, odd_w_in, j,
                                           positions if rope_tab is None else None)
            rope_tab = tab if rope_tab is None else rope_tab
            xf = _even_layer(h, xf, even_b_sinks, rope_tab, even_a_ln_g, even_a_ln_b, even_a_ws,
                             even_bst, w_out, even_ln_g, even_ln_b, j, seq)
        else:
            w_next = even_w_in if layer + 1 < DEPTH else None
            h, w_out, w_in, _ = _proj_in(xf, w_in, w_layer, odd_w_out, j, w_next, j + 1)
            xf = _odd_layer(h, xf, odd_conv_w, odd_conv_b, odd_w_a, odd_w_x, odd_b_a, odd_b_x, odd_lam,
                            odd_w_pool, odd_d_scale, w_out, odd_ln_g, odd_ln_b, j, seq)
        w_layer = None
    return xf.reshape(batch, seq, d)
```

```python
from functools import partial

import jax
import jax.numpy as jnp
from jax import lax
from jax.experimental import pallas as pl
from jax.experimental.pallas import tpu as pltpu

D_MODEL = 2048
DEPTH = 4
A_WIDTH = 1024
A_GROUPS = 8
CHUNK = 128
B_HEAD_DIM = 64
B_Q_HEADS = 16
B_KV_HEADS = 2
B_WIDTH = 1024
B_KV_WIDTH = 128
WINDOW = 128
ROT_DIM = 16
ROPE_THETA = 500000.0
C_WIDTH = 1024
C_HEADS = 8
C_HEAD_DIM = 128
CONV_WIDTH = 4
LRU_C = 8.0
D_WIDTH = 1024
POOL_WINDOWS = (2, 4, 8, 16)
D_GROUP_DIM = 256
MIX_WIDTH = 2048
DN_ALPHA = (2 * DEPTH) ** 0.25
LN_EPS = 1e-5

LANES = 128
SUBLANES = 8
VMEM_LIMIT_BYTES = 56 * 1024 * 1024

PROJ_TM = 256
PROJ_VMEM_BYTES = 44 * 1024 * 1024
MIX_T = 256
OUT_BLOCK = 256
OUT_BLOCKS = D_MODEL // OUT_BLOCK
W_CHUNK = 256
CONV_HALO = SUBLANES
POOL_HALO = 4 * SUBLANES
SCAN_PITCH = SUBLANES + 1
ROW_PAD = LANES

KV_BLOCK = 2 * B_KV_WIDTH
KV_COL = (3 * A_WIDTH + B_WIDTH) // KV_BLOCK
GATE_B_COL = KV_COL + 1

F32 = jnp.float32
BF16 = jnp.bfloat16


def _sigmoid(x):
    return 0.5 * jnp.tanh(0.5 * x) + 0.5


def _silu(x):
    hx = 0.5 * x
    return hx * jnp.tanh(hx) + hx


def _resident(shape):
    return pl.BlockSpec(shape, lambda *_: (0,) * len(shape), pipeline_mode=pl.Buffered(1))


def _whole(arr):
    return pl.BlockSpec(arr.shape, lambda *_: (0,) * arr.ndim)


def _layer_block(shape, layer):
    return pl.BlockSpec((None,) + shape, lambda *_: (layer,) + (0,) * len(shape))


def _compiler_params(*semantics):
    return pltpu.CompilerParams(dimension_semantics=semantics, vmem_limit_bytes=VMEM_LIMIT_BYTES)


def _proj_in_kernel(*refs, cast_next, w_layer, rope):
    n_in = 3 + cast_next + 2 * rope
    n_out = 2 + cast_next + rope
    x_ref, w_hbm, wout_ref = refs[:3]
    opad_ref, wout_bf_ref = refs[n_in:n_in + 2]
    staged = w_hbm.dtype != BF16
    w_vmem, *stage, sems = refs[n_in + n_out:]
    n = w_vmem.shape[1]
    o_ref = opad_ref.at[:, :n]

    def side_jobs():
        opad_ref[:, n:] = jnp.zeros((opad_ref.shape[0], ROW_PAD), F32)
        if cast_next:
            refs[n_in + 2][...] = refs[3][...].astype(BF16)
        for p in range(OUT_BLOCKS):
            wout_bf_ref[p] = wout_ref[:, p * OUT_BLOCK:(p + 1) * OUT_BLOCK].astype(BF16)
        if rope:
            pos_ref, coef_ref = refs[n_in - 2:n_in]
            tab_ref = refs[n_in + n_out - 1]
            ang = pos_ref[...].astype(F32) * coef_ref[0:1, :]
            tab_ref[:, :LANES] = jnp.cos(ang)
            tab_ref[:, LANES:] = jnp.sin(ang) * coef_ref[1:2, :]

    n_chunks = w_vmem.shape[0] // W_CHUNK
    w_src = w_hbm if w_layer is None else w_hbm.at[w_layer]

    def chunk_copy(c):
        rows = pl.ds(c * W_CHUNK, W_CHUNK)
        if staged:
            return pltpu.make_async_copy(w_src.at[rows, :], stage[0].at[c % 2], sems.at[c % 2])
        return pltpu.make_async_copy(w_src.at[rows, :], w_vmem.at[rows, :], sems.at[c])

    step = pl.program_id(0)

    @pl.when(step == 0)
    def _():
        xb = x_ref[...].astype(BF16)
        for c in range(2 if staged else n_chunks):
            chunk_copy(c).start()
        side_jobs()
        for c in range(n_chunks):
            rows = slice(c * W_CHUNK, (c + 1) * W_CHUNK)
            chunk_copy(c).wait()
            if staged:
                w_vmem[rows, :] = stage[0][c % 2].astype(BF16)
                if c + 2 < n_chunks:
                    chunk_copy(c + 2).start()
            part = jnp.dot(xb[:, rows], w_vmem[rows, :], preferred_element_type=F32)
            if c == 0:
                o_ref[...] = part
            else:
                o_ref[...] += part

    @pl.when(step > 0)
    def _():
        side_jobs()
        o_ref[...] = jnp.dot(x_ref[...].astype(BF16), w_vmem[...], preferred_element_type=F32)


def _proj_in(x, w, w_layer, w_out, layer, w_next, layer_next, positions=None):
    m, k = x.shape
    n = w.shape[-1]
    wide = k * n * 2 + 2 * (2 * PROJ_TM) * (k + n + ROW_PAD) * 4 <= PROJ_VMEM_BYTES
    tm = 2 * PROJ_TM if wide and m % (2 * PROJ_TM) == 0 else PROJ_TM
    steps = m // tm

    def slab_specs(stacked, which):
        rows, cols = stacked.shape[1:]
        slab = rows // steps
        return (pl.BlockSpec((None, slab, cols), lambda i: (which, i, 0)),
                pl.BlockSpec((slab, cols), lambda i: (i, 0)),
                jax.ShapeDtypeStruct((rows, cols), BF16))

    k_out = w_out.shape[1]
    slab_out = k_out // steps
    casts = [(pl.BlockSpec((None, slab_out, D_MODEL), lambda i: (layer, i, 0)),
              pl.BlockSpec((OUT_BLOCKS, slab_out, OUT_BLOCK), lambda i: (0, i, 0)),
              jax.ShapeDtypeStruct((OUT_BLOCKS, k_out, OUT_BLOCK), BF16))]
    args = [x, w, w_out]
    if w_next is not None:
        casts.append(slab_specs(w_next, layer_next))
        args.append(w_next)
    rope = positions is not None
    if rope:
        casts.append((pl.BlockSpec((tm, 1), lambda i: (i, 0)), None, None))
        coef = _rope_coefficients()
        casts.append((_whole(coef), pl.BlockSpec((tm, 2 * LANES), lambda i: (i, 0)),
                      jax.ShapeDtypeStruct((m, 2 * LANES), F32)))
        args += [positions.reshape(m, 1), coef]
    staged = w.dtype != BF16
    scratch = [pltpu.VMEM((k, n), BF16)]
    if staged:
        scratch.append(pltpu.VMEM((2, W_CHUNK, n), w.dtype))
    scratch.append(pltpu.SemaphoreType.DMA((2 if staged else k // W_CHUNK,)))
    outs = pl.pallas_call(
        partial(_proj_in_kernel, cast_next=w_next is not None, w_layer=w_layer, rope=rope),
        grid=(steps,),
        in_specs=[pl.BlockSpec((tm, k), lambda i: (i, 0)), pl.BlockSpec(memory_space=pl.ANY)]
        + [c[0] for c in casts],
        out_specs=[pl.BlockSpec((tm, n + ROW_PAD), lambda i: (i, 0))] + [c[1] for c in casts if c[1]],
        out_shape=[jax.ShapeDtypeStruct((m, n + ROW_PAD), F32)] + [c[2] for c in casts if c[2]],
        scratch_shapes=scratch,
        compiler_params=_compiler_params("arbitrary"),
        name="proj_in",
    )(*args)
    h, w_out_bf, *rest = outs
    return h, w_out_bf, rest.pop(0) if w_next is not None else None, rest.pop(0) if rope else None


class _Tail:
    def __init__(self, mix_prev, wout_ref, x_ref, z_ref, z_prev, g_ref, b_ref, layer, o_ref,
                 proj, ln, load_once=False):
        self.mix_ref = mix_prev.at[:, :MIX_WIDTH]
        self.mix = self.mix_ref[...] if proj and load_once else None
        self.proj_refs = (wout_ref, x_ref, z_ref)
        self.ln_refs = (z_prev, g_ref, b_ref, layer, o_ref)
        self.blocks_left = list(range(OUT_BLOCKS)) if proj else []
        rows = z_ref.shape[0]
        self.halves_left = [slice(r * rows // 2, (r + 1) * rows // 2) for r in range(2)] if ln else []

    def emit(self, count=1):
        wout_ref, x_ref, z_ref = self.proj_refs
        for _ in range(min(count, len(self.blocks_left))):
            blk = self.blocks_left.pop(0)
            cols = slice(blk * OUT_BLOCK, (blk + 1) * OUT_BLOCK)
            lhs = self.mix_ref[...] if self.mix is None else self.mix
            z_ref[:, cols] = DN_ALPHA * x_ref[:, cols] + jnp.dot(
                lhs, wout_ref[blk], preferred_element_type=F32)

    def ln_half(self):
        if self.halves_left:
            z_prev, g_ref, b_ref, layer, o_ref = self.ln_refs
            rows = self.halves_left.pop(0)
            z = z_prev[rows, :D_MODEL]
            mu = jnp.mean(z, axis=-1, keepdims=True)
            zc = z - mu
            var = jnp.mean(zc * zc, axis=-1, keepdims=True)
            o_ref[rows, :] = (zc * lax.rsqrt(var + LN_EPS) * g_ref[layer:layer + 1, :]
                              + b_ref[layer:layer + 1, :])

    def finish(self):
        self.ln_half()
        self.ln_half()
        self.emit(OUT_BLOCKS)


def _lagged_tiles(n_tiles):
    def clamp(i):
        return jnp.clip(i, 0, n_tiles - 1)

    return clamp, (lambda i: clamp(i - 1)), (lambda i: clamp(i - 2))


def _run_stages(step, n_tiles, body, wout_hbm, wout_vmem, wout_sem, mix_a, mix_b, z_a, z_b):
    wout_copy = pltpu.make_async_copy(wout_hbm, wout_vmem, wout_sem.at[0])

    @pl.when(step == 0)
    def _():
        wout_copy.start()

    @pl.when(step == 1)
    def _():
        wout_copy.wait()

    def stage(pred, parity, **stages):
        bufs = (mix_a, mix_b, z_a, z_b) if parity == 0 else (mix_b, mix_a, z_b, z_a)

        @pl.when(pred)
        def _():
            body(*bufs, **stages)

    steady = (step >= 2) & (step < n_tiles)
    stage(step == 0, 0, mix=True, proj=False, ln=False)
    stage(step == 1, 1, mix=True, proj=True, ln=False)
    stage(steady & (step % 2 == 0), 0, mix=True, proj=True, ln=True)
    stage(steady & (step % 2 == 1), 1, mix=True, proj=True, ln=True)
    stage(step == n_tiles, n_tiles % 2, mix=False, proj=True, ln=True)
    stage(step == n_tiles + 1, (n_tiles + 1) % 2, mix=False, proj=False, ln=True)


def _tail_specs(x, w_out_bf, ln_g, ln_b, res, out):
    m, d = x.shape
    in_specs = [
        pl.BlockSpec((MIX_T, d), lambda i: (res(i), 0)),
        pl.BlockSpec(memory_space=pl.ANY),
        _whole(ln_g),
        _whole(ln_b),
    ]
    out_spec = pl.BlockSpec((MIX_T, d), lambda i: (out(i), 0))
    out_shape = jax.ShapeDtypeStruct((m, d), F32)
    scratch = [pltpu.VMEM((MIX_T, MIX_WIDTH + ROW_PAD), BF16), pltpu.VMEM((MIX_T, MIX_WIDTH + ROW_PAD), BF16),
               pltpu.VMEM((MIX_T, d + ROW_PAD), F32), pltpu.VMEM((MIX_T, d + ROW_PAD), F32),
               pltpu.VMEM(w_out_bf.shape, BF16), pltpu.SemaphoreType.DMA((1,))]
    return in_specs, out_spec, out_shape, scratch, [x, w_out_bf, ln_g, ln_b]


def _rope_tile(t, cos, sin, take_upper):
    upper = pltpu.roll(t, LANES - ROT_DIM // 2, 1)
    lower = pltpu.roll(t, ROT_DIM // 2, 1)
    return t * cos + jnp.where(take_upper, upper, lower) * sin


EVEN_WOUT_ARG = 10
ODD_WOUT_ARG = 11


def _even_layer_kernel(*refs, layer, n_tiles, n_s):
    *io_refs, mix_a, mix_b, z_a, z_b, wout_vmem, wout_sem = refs
    wout_hbm = io_refs[EVEN_WOUT_ARG]
    io_refs[EVEN_WOUT_ARG] = wout_vmem
    step = pl.program_id(0)
    seq_tile = jnp.minimum(step, n_tiles - 1) % n_s
    _run_stages(step, n_tiles, partial(_even_step, *io_refs, layer=layer, seq_tile=seq_tile),
                wout_hbm, wout_vmem, wout_sem, mix_a, mix_b, z_a, z_b)


def _even_step(sinks_ref, h_ref, kvp_ref, tabc_ref, tabp_ref, lng_ref, lnb_ref, ws_ref, bst_ref,
               x_ref, wout_ref, g_ref, b_ref, o_ref, mix_ref, mix_prev, z_ref, z_prev,
               *, layer, seq_tile, mix, proj, ln):
    u_ref, v_ref, ag_ref, q_ref = (h_ref.at[:, c * A_WIDTH:(c + 1) * A_WIDTH] for c in range(4))
    kvc_ref = h_ref.at[:, KV_COL * KV_BLOCK:(KV_COL + 1) * KV_BLOCK]
    bg_refs = [h_ref.at[:, (GATE_B_COL + c) * KV_BLOCK:(GATE_B_COL + c + 1) * KV_BLOCK]
               for c in range(B_WIDTH // KV_BLOCK)]
    cosc_ref, sinc_ref = tabc_ref.at[:, :LANES], tabc_ref.at[:, LANES:]
    cosp_ref, sinp_ref = tabp_ref.at[:, :LANES], tabp_ref.at[:, LANES:]
    t_rows = h_ref.shape[0]
    n_chunks = t_rows // CHUNK
    tail = _Tail(mix_prev, wout_ref, x_ref, z_ref, z_prev, g_ref, b_ref, layer, o_ref, proj, ln,
                 load_once=True)
    if not mix:
        tail.finish()
        return
    tail.emit()

    v = v_ref[...]
    mu = jnp.mean(v, axis=-1, keepdims=True)
    vc = v - mu
    var = jnp.mean(vc * vc, axis=-1, keepdims=True)
    vln = (vc * lax.rsqrt(var + LN_EPS) * lng_ref[layer:layer + 1, :]
           + lnb_ref[layer:layer + 1, :]).astype(BF16)
    causal = (lax.broadcasted_iota(jnp.int32, (CHUNK, CHUNK), 0)
              >= lax.broadcasted_iota(jnp.int32, (CHUNK, CHUNK), 1))
    lane = lax.broadcasted_iota(jnp.int32, (1, LANES), 1)
    take_upper = (lane % B_HEAD_DIM) < (ROT_DIM // 2)
    low_half = lane < B_HEAD_DIM
    cosc = cosc_ref[...]
    sinc = sinc_ref[...]
    scale = B_HEAD_DIM ** -0.5
    q_tiles = []
    for g in range(A_GROUPS):
        cols = slice(g * CHUNK, (g + 1) * CHUNK)
        w = jnp.where(causal, ws_ref[g], 0.0).astype(BF16)
        rhs = jnp.concatenate(
            [vln[c * CHUNK:(c + 1) * CHUNK, cols] for c in range(n_chunks)], axis=1)
        mixed = jnp.dot(w, rhs, preferred_element_type=F32) + bst_ref[:, g:g + 1]
        for c in range(n_chunks):
            rows = slice(c * CHUNK, (c + 1) * CHUNK)
            gate = _silu(ag_ref[rows, cols])
            mix_ref[rows, cols] = (u_ref[rows, cols] * mixed[:, c * CHUNK:(c + 1) * CHUNK]
                                   * gate).astype(BF16)
        q_tiles.append(
            (_rope_tile(q_ref[:, cols], cosc, sinc, take_upper) * scale).astype(BF16))
        if g in (1, 5):
            tail.ln_half()
        if g in (2, 4, 6):
            tail.emit()

    k_all = jnp.concatenate(
        [_rope_tile(kvp_ref[:, :B_KV_WIDTH], cosp_ref[...], sinp_ref[...], take_upper),
         _rope_tile(kvc_ref[:, :B_KV_WIDTH], cosc, sinc, take_upper)], axis=0)
    k_swap = pltpu.roll(k_all, B_HEAD_DIM, 1)
    zero = jnp.zeros_like(k_all)
    k_lo = (jnp.where(low_half, k_all, zero).astype(BF16), jnp.where(low_half, k_swap, zero).astype(BF16))
    k_hi = (jnp.where(low_half, zero, k_swap).astype(BF16), jnp.where(low_half, zero, k_all).astype(BF16))
    v_all = jnp.concatenate([kvp_ref[:, B_KV_WIDTH:], kvc_ref[:, B_KV_WIDTH:]], axis=0)
    vt_all = v_all.T.astype(BF16)

    tiles_per_kv = B_WIDTH // LANES // B_KV_HEADS
    band_keys = 2 * WINDOW
    kj = lax.broadcasted_iota(jnp.int32, (band_keys, WINDOW), 0)
    qi = lax.broadcasted_iota(jnp.int32, (band_keys, WINDOW), 1)
    diff = qi + WINDOW - kj
    band = (diff >= 0) & (diff < WINDOW)
    first_key = jnp.where(seq_tile == 0, WINDOW, 0)
    neg_inf = jnp.full((band_keys, WINDOW), -jnp.inf, F32)
    mask_any = jnp.where(band, 0.0, neg_inf)
    mask_first = jnp.where(band & (kj >= first_key), 0.0, neg_inf)

    for n in range(t_rows // WINDOW):
        rows = slice(n * WINDOW, (n + 1) * WINDOW)
        band_rows = slice(n * WINDOW, (n + 2) * WINDOW)
        mask = jnp.concatenate([mask_first if n == 0 else mask_any] * tiles_per_kv, axis=1)
        for kv in range(B_KV_HEADS):
            tail.emit()
            q4 = jnp.concatenate(
                [q_tiles[kv * tiles_per_kv + j][rows] for j in range(tiles_per_kv)], axis=0)
            k_cat = jnp.concatenate([k_lo[kv][band_rows], k_hi[kv][band_rows]], axis=0)
            st = lax.dot_general(k_cat, q4, (((1,), (1,)), ((), ())), preferred_element_type=F32)
            vt = vt_all[kv * B_HEAD_DIM:(kv + 1) * B_HEAD_DIM, band_rows]
            o_halves = []
            for half in range(2):
                sh = st[half * band_keys:(half + 1) * band_keys] + mask
                sink = jnp.concatenate(
                    [jnp.full((1, WINDOW), sinks_ref[layer, 2 * (kv * tiles_per_kv + j) + half], F32)
                     for j in range(tiles_per_kv)], axis=1)
                m = jnp.maximum(jnp.max(sh, axis=0, keepdims=True), sink)
                p = jnp.exp(sh - m)
                l = jnp.sum(p, axis=0, keepdims=True) + jnp.exp(sink - m)
                o_half = jnp.dot(vt, p.astype(BF16), preferred_element_type=F32)
                o_halves.append(o_half * (1.0 / l))
            ot = jnp.concatenate(o_halves, axis=0)
            for j in range(tiles_per_kv):
                jj = kv * tiles_per_kv + j
                gate_ref = bg_refs[jj // 2]
                gate_cols = slice((jj % 2) * LANES, (jj % 2 + 1) * LANES)
                out_cols = slice(A_WIDTH + jj * LANES, A_WIDTH + (jj + 1) * LANES)
                o_tile = ot[:, j * WINDOW:(j + 1) * WINDOW].T
                mix_ref[rows, out_cols] = (o_tile * _silu(gate_ref[rows, gate_cols])).astype(BF16)

    tail.finish()


def _even_layer(h, x, sinks, rope_tab, a_ln_g, a_ln_b, ws, bst, w_out_bf, ln_g, ln_b, layer, seq):
    m = h.shape[0]
    t = MIX_T
    n_tiles = m // t
    n_s = seq // t
    blocks_per_tile = t // WINDOW
    cur, res, out = _lagged_tiles(n_tiles)

    def prev(i):
        return jnp.maximum(cur(i) * blocks_per_tile - 1, 0)

    tail_in, out_specs, out_shape, tail_scratch, tail_args = _tail_specs(x, w_out_bf, ln_g, ln_b, res, out)
    return pl.pallas_call(
        partial(_even_layer_kernel, layer=layer, n_tiles=n_tiles, n_s=n_s),
        grid=(n_tiles + 2,),
        in_specs=[
            pl.BlockSpec(memory_space=pltpu.SMEM),
            pl.BlockSpec((t, h.shape[1]), lambda i: (cur(i), 0)),
            pl.BlockSpec((WINDOW, KV_BLOCK), lambda i: (prev(i), KV_COL)),
            pl.BlockSpec((t, 2 * LANES), lambda i: (cur(i), 0)),
            pl.BlockSpec((WINDOW, 2 * LANES), lambda i: (prev(i), 0)),
            _whole(a_ln_g),
            _whole(a_ln_b),
            _layer_block((A_GROUPS, CHUNK, CHUNK), layer),
            _layer_block((CHUNK, A_GROUPS), layer),
        ] + tail_in,
        out_specs=out_specs,
        out_shape=out_shape,
        scratch_shapes=tail_scratch,
        compiler_params=_compiler_params("arbitrary"),
        name="even_layer",
    )(sinks, h, h, rope_tab, rope_tab, a_ln_g, a_ln_b, ws, bst, *tail_args)


def _odd_layer_kernel(*refs, layer, n_tiles, n_s):
    *io_refs, mix_a, mix_b, z_a, z_b, wout_vmem, wout_sem = refs[:21]
    (ext_c, ext_d, lvl2, lvl4, hstate, sa, sb, sh, wax_bf, wpool_bf) = scratch = refs[21:]
    wout_hbm = io_refs[ODD_WOUT_ARG]
    io_refs[ODD_WOUT_ARG] = wout_vmem
    wa_ref, wx_ref, wpool_ref = io_refs[3], io_refs[4], io_refs[8]
    t_rows = io_refs[0].shape[0]
    step = pl.program_id(0)
    s_idx = jnp.minimum(step, n_tiles - 1) % n_s

    @pl.when(s_idx == 0)
    def _():
        ext_c[:, 0:CONV_HALO, :] = jnp.zeros((C_HEADS, CONV_HALO, LANES), F32)
        ext_d[:, 0:POOL_HALO, :] = jnp.zeros((D_WIDTH // LANES, POOL_HALO, LANES), F32)
        hstate[...] = jnp.zeros_like(hstate)
        wax_bf[:, :, :C_HEAD_DIM] = wa_ref[...].astype(BF16)
        wax_bf[:, :, C_HEAD_DIM:] = wx_ref[...].astype(BF16)
        wpool_bf[...] = wpool_ref[...].astype(BF16)

    @pl.when(s_idx > 0)
    def _():
        ext_c[:, 0:CONV_HALO, :] = ext_c[:, t_rows:t_rows + CONV_HALO, :]
        ext_d[:, 0:POOL_HALO, :] = ext_d[:, t_rows:t_rows + POOL_HALO, :]

    _run_stages(step, n_tiles, partial(_odd_step, *io_refs, *scratch, layer=layer, s_idx=s_idx),
                wout_hbm, wout_vmem, wout_sem, mix_a, mix_b, z_a, z_b)


def _odd_step(h_ref, convw_ref, convb_ref, wa_ref, wx_ref, ba_ref, bx_ref,
              lam_ref, wpool_ref, dscale_ref, x_ref, wout_ref, g_ref, b_ref, o_ref,
              ext_c, ext_d, lvl2, lvl4, hstate, sa, sb, sh, wax_bf, wpool_bf,
              mix_ref, mix_prev, z_ref, z_prev, *, layer, s_idx, mix, proj, ln):
    xc_ref, cg_ref, xd_ref, dg_ref = (h_ref.at[:, c * C_WIDTH:(c + 1) * C_WIDTH] for c in range(4))
    t_rows = h_ref.shape[0]
    row = slice(layer, layer + 1)
    n_groups = t_rows // SUBLANES
    tail = _Tail(mix_prev, wout_ref, x_ref, z_ref, z_prev, g_ref, b_ref, layer, o_ref, proj, ln)
    if not mix:
        tail.finish()
        return
    tail.emit()

    z = -lam_ref[row, :]
    softplus = jnp.maximum(z, 0.0) + jnp.log1p(jnp.exp(-jnp.abs(z)))
    for hd in range(C_HEADS):
        cols = slice(hd * C_HEAD_DIM, (hd + 1) * C_HEAD_DIM)
        ext_c[hd, CONV_HALO:, :] = xc_ref[:, cols]
        xconv = convb_ref[row, cols]
        for j in range(CONV_WIDTH):
            lo = CONV_HALO - (CONV_WIDTH - 1) + j
            xconv = xconv + ext_c[hd, lo:lo + t_rows, :] * convw_ref[j:j + 1, cols]
        ri = jnp.dot(xconv.astype(BF16), wax_bf[hd], preferred_element_type=F32)
        r = _sigmoid(ri[:, :C_HEAD_DIM] + ba_ref[row, cols])
        i = _sigmoid(ri[:, C_HEAD_DIM:] + bx_ref[row, cols])
        log_a = -LRU_C * r * softplus[:, cols]
        a = jnp.exp(log_a)
        m2 = -jnp.tanh(log_a) * (a * a + 1.0)
        mult = jnp.where(m2 == 0.0, 0.0, m2 * lax.rsqrt(m2))
        b = mult * (i * xconv)
        for g in range(n_groups):
            base = (g * C_HEADS + hd) * SCAN_PITCH
            sa[base:base + SUBLANES, :] = a[g * SUBLANES:(g + 1) * SUBLANES]
            sb[base:base + SUBLANES, :] = b[g * SUBLANES:(g + 1) * SUBLANES]
        if hd in (2, 5):
            tail.ln_half()
        if hd in (1, 3, 4, 6, 7):
            tail.emit()

    h = hstate[...]
    for t in range(t_rows):
        g, s = divmod(t, SUBLANES)
        at_t = pl.ds(g * C_HEADS * SCAN_PITCH + s, C_HEADS, stride=SCAN_PITCH)
        h = sa[at_t, :] * h + sb[at_t, :]
        sh[at_t, :] = h
    hstate[...] = h
    tail.emit()
    for hd in range(C_HEADS):
        cols = slice(hd * C_HEAD_DIM, (hd + 1) * C_HEAD_DIM)
        hcol = jnp.concatenate(
            [sh[(g * C_HEADS + hd) * SCAN_PITCH:(g * C_HEADS + hd) * SCAN_PITCH + SUBLANES, :]
             for g in range(n_groups)], axis=0)
        mix_ref[:, cols] = (hcol * _silu(cg_ref[:, cols])).astype(BF16)

    n_ext = POOL_HALO + t_rows
    tiles_per_group = D_GROUP_DIM // LANES
    pos1 = s_idx * t_rows + lax.broadcasted_iota(jnp.int32, (t_rows, LANES), 0) + 1
    for g, w in enumerate(POOL_WINDOWS):
        if g == 2:
            tail.emit()
        inv = 1.0 / jnp.minimum(pos1, w).astype(F32)
        pooled = []
        for k in range(tiles_per_group):
            tile = g * tiles_per_group + k
            cols = slice(tile * LANES, (tile + 1) * LANES)
            ext_d[tile, POOL_HALO:, :] = xd_ref[:, cols]
            lo = SUBLANES
            win = ext_d[tile, lo:n_ext, :] + ext_d[tile, lo - 1:n_ext - 1, :]
            if w >= 4:
                lvl2[k, lo:n_ext, :] = win
                lo += SUBLANES
                win = lvl2[k, lo:n_ext, :] + lvl2[k, lo - 2:n_ext - 2, :]
            if w >= 8:
                lvl4[k, lo:n_ext, :] = win
                lo += SUBLANES
                win = lvl4[k, lo:n_ext, :] + lvl4[k, lo - 4:n_ext - 4, :]
            if w >= 16:
                lo += SUBLANES
                win = win[SUBLANES:] + win[:-SUBLANES]
            pooled.append((win[POOL_HALO - lo:] * inv - xd_ref[:, cols]).astype(BF16))
        cols = slice(g * D_GROUP_DIM, (g + 1) * D_GROUP_DIM)
        mixed = jnp.dot(jnp.concatenate(pooled, axis=1), wpool_bf[g], preferred_element_type=F32)
        out_cols = slice(C_WIDTH + g * D_GROUP_DIM, C_WIDTH + (g + 1) * D_GROUP_DIM)
        mix_ref[:, out_cols] = (mixed * dscale_ref[row, cols] * _silu(dg_ref[:, cols])).astype(BF16)

    tail.finish()


def _odd_layer(h, x, conv_w, conv_b, w_a, w_x, b_a, b_x, lam, w_pool, d_scale, w_out_bf, ln_g, ln_b,
               layer, seq):
    m = h.shape[0]
    t = MIX_T
    n_tiles = m // t
    n_s = seq // t
    cur, res, out = _lagged_tiles(n_tiles)
    n_pool = len(POOL_WINDOWS)
    tail_in, out_specs, out_shape, tail_scratch, tail_args = _tail_specs(x, w_out_bf, ln_g, ln_b, res, out)
    return pl.pallas_call(
        partial(_odd_layer_kernel, layer=layer, n_tiles=n_tiles, n_s=n_s),
        grid=(n_tiles + 2,),
        in_specs=[
            pl.BlockSpec((t, h.shape[1]), lambda i: (cur(i), 0)),
            _layer_block((CONV_WIDTH, C_WIDTH), layer),
            _whole(conv_b),
            _layer_block((C_HEADS, C_HEAD_DIM, C_HEAD_DIM), layer),
            _layer_block((C_HEADS, C_HEAD_DIM, C_HEAD_DIM), layer),
            _whole(b_a), _whole(b_x), _whole(lam),
            _layer_block((n_pool, D_GROUP_DIM, D_GROUP_DIM), layer),
            _whole(d_scale),
        ] + tail_in,
        out_specs=out_specs,
        out_shape=out_shape,
        scratch_shapes=tail_scratch + [
            pltpu.VMEM((C_WIDTH // LANES, CONV_HALO + t, LANES), F32),
            pltpu.VMEM((D_WIDTH // LANES, POOL_HALO + t, LANES), F32),
            pltpu.VMEM((D_GROUP_DIM // LANES, POOL_HALO + t, LANES), F32),
            pltpu.VMEM((D_GROUP_DIM // LANES, POOL_HALO + t, LANES), F32),
            pltpu.VMEM((SUBLANES, LANES), F32),
            pltpu.VMEM((t // SUBLANES * C_HEADS * SCAN_PITCH, LANES), F32),
            pltpu.VMEM((t // SUBLANES * C_HEADS * SCAN_PITCH, LANES), F32),
            pltpu.VMEM((t // SUBLANES * C_HEADS * SCAN_PITCH, LANES), F32),
            pltpu.VMEM((C_HEADS, C_HEAD_DIM, 2 * C_HEAD_DIM), BF16),
            pltpu.VMEM((n_pool, D_GROUP_DIM, D_GROUP_DIM), BF16),
        ],
        compiler_params=_compiler_params("arbitrary"),
        name="odd_layer",
    )(h, conv_w, conv_b, w_a, w_x, b_a, b_x, lam, w_pool, d_scale, *tail_args)


def _rope_coefficients():
    half = ROT_DIM // 2
    inv_freq = ROPE_THETA ** (-jnp.arange(0, ROT_DIM, 2, dtype=F32) / ROT_DIM)
    rest = jnp.zeros((B_HEAD_DIM - ROT_DIM,), F32)
    freq_head = jnp.concatenate([inv_freq, inv_freq, rest])
    sign_head = jnp.concatenate([-jnp.ones((half,), F32), jnp.ones((half,), F32), rest])
    reps = LANES // B_HEAD_DIM
    return jnp.stack([jnp.tile(freq_head, reps), jnp.tile(sign_head, reps)])


def kernel(x, positions, even_w_in, even_a_ln_g, even_a_ln_b, even_a_ws, even_a_bs, even_b_sinks, even_w_out, even_ln_g, even_ln_b, odd_w_in, odd_conv_w, odd_conv_b, odd_w_a, odd_b_a, odd_w_x, odd_b_x, odd_lam, odd_w_pool, odd_d_scale, odd_w_out, odd_ln_g, odd_ln_b):
    batch, seq, d = x.shape
    m = batch * seq
    xf = x.reshape(m, d)
    rope_tab = None
    even_bst = jnp.swapaxes(even_a_bs, 1, 2)
    w_in, w_layer = even_w_in, 0
    for layer in range(DEPTH):
        j = layer // 2
        if layer % 2 == 0:
            h, w_out, w_in, tab = _proj_in(xf, w_in, w_layer, even_w_out, j, odd_w_in, j,
                                           positions if rope_tab is None else None)
            rope_tab = tab if rope_tab is None else rope_tab
            xf = _even_layer(h, xf, even_b_sinks, rope_tab, even_a_ln_g, even_a_ln_b, even_a_ws,
                             even_bst, w_out, even_ln_g, even_ln_b, j, seq)
        else:
            w_next = even_w_in if layer + 1 < DEPTH else None
            h, w_out, w_in, _ = _proj_in(xf, w_in, w_layer, odd_w_out, j, w_next, j + 1)
            xf = _odd_layer(h, xf, odd_conv_w, odd_conv_b, odd_w_a, odd_w_x, odd_b_a, odd_b_x, odd_lam,
                            odd_w_pool, odd_d_scale, w_out, odd_ln_g, odd_ln_b, j, seq)
        w_layer = None
    return xf.reshape(batch, seq, d)
```
